```python
import jax, jax.numpy as jnp
from jax import lax
import numpy as np

D_MODEL = 1024
BATCH = 16
SEQ = 4096
DEPTH = 2
DEC_BATCH = 16
DEC_SEQ = 32
PAST_LEN = 4096

F32 = jnp.float32
CHUNK = 64
D_MIX = D_MODEL
HEAD_DIM = 64
D_A = D_MIX // 2
A_HEADS = D_A // HEAD_DIM
A_KV_HEADS = 2
A_GROUP = A_HEADS // A_KV_HEADS
D_KV = A_KV_HEADS * HEAD_DIM
D_A_IN = D_A + 2 * D_KV
WINDOW = 128
WINDOW_CHUNKS = WINDOW // CHUNK
BAND = WINDOW + CHUNK
D_POOL = D_MIX // 4
POOL_WINDOWS = (2, 4, 8, 16)
POOL_GROUPS = 4
POOL_GW = D_POOL // POOL_GROUPS
POOL_CTX = 15
D_R = D_MIX - D_A - D_POOL
R_HEADS = D_R // HEAD_DIM
R_W = 64
R_A = 64
R_G = 128
D_R_IN = 3 * D_R + R_W + R_A + R_G
D_IN = D_A_IN + D_POOL + D_R_IN
D_FF = 7 * D_MODEL // 2
N_EXPERTS = 8
TOP_K = 2
D_FF_EXPERT = 7 * D_MODEL // 2
N_DENSE = (DEPTH + 1) // 2
N_MOE = DEPTH // 2
NORM_EPS = 1e-6
GN_EPS = 64e-5
NEG_INF = -1e30

kernel_name = 'hymba_style_streaming_encoder_step'


def rms_norm(x, g):
    xf = x.astype(F32)
    y = xf * lax.rsqrt(jnp.mean(xf * xf, axis=-1, keepdims=True) + NORM_EPS)
    return (y * g.astype(F32)).astype(x.dtype)


def swiglu(h, wg, wu, wd):
    return (jax.nn.silu(h @ wg) * (h @ wu)) @ wd


def alibi_bias(n_q):
    i = jnp.arange(n_q)[:, None]
    j = jnp.arange(WINDOW + n_q)[None, :]
    dist = jnp.abs(WINDOW + i - j).astype(F32)
    slopes = jnp.exp2(-8.0 * jnp.arange(1, A_HEADS + 1, dtype=F32) / A_HEADS)
    return (-slopes[:, None, None] * dist).reshape(A_KV_HEADS, A_GROUP, n_q, WINDOW + n_q)


def sink_attention(q, k, v, bias, sink):
    s = jnp.einsum('...qkgd,...skd->...kgqs', q.astype(F32), k.astype(F32)) * (HEAD_DIM ** -0.5) + bias
    sink_col = jnp.broadcast_to(sink.astype(F32)[:, :, None, None], s.shape[:-1] + (1,))
    p = jax.nn.softmax(jnp.concatenate([s, sink_col], axis=-1), axis=-1)[..., :-1]
    o = jnp.einsum('...kgqs,...skd->...qkgd', p, v.astype(F32))
    return o.astype(v.dtype)


def key_band(x, n_chunks):
    b = x.shape[0]
    xc = x.reshape(b, n_chunks, CHUNK, A_KV_HEADS, HEAD_DIM)
    xp = jnp.pad(xc, ((0, 0), (WINDOW_CHUNKS, 0), (0, 0), (0, 0), (0, 0)))
    return jnp.concatenate([xp[:, i:i + n_chunks] for i in range(WINDOW_CHUNKS + 1)], axis=2)


def attn_prompt(q, k, v, sink):
    b, t = q.shape[:2]
    nc = t // CHUNK
    qc = q.reshape(b, nc, CHUNK, A_KV_HEADS, A_GROUP, HEAD_DIM)
    kpos = jnp.arange(nc)[:, None] * CHUNK - WINDOW + jnp.arange(BAND)[None, :]
    valid = jnp.where(kpos >= 0, 0.0, NEG_INF).astype(F32)[:, None, None, None, :]
    bias = alibi_bias(CHUNK)[None] + valid
    o = sink_attention(qc, key_band(k, nc), key_band(v, nc), bias, sink)
    return o.reshape(b, t, D_A)


def attn_sample(q, k, v, ck, cv, sink):
    b, s = q.shape[:2]
    qg = q.reshape(b, s, A_KV_HEADS, A_GROUP, HEAD_DIM)
    kk = jnp.concatenate([ck.astype(k.dtype), k], axis=1)
    vv = jnp.concatenate([cv.astype(v.dtype), v], axis=1)
    o = sink_attention(qg, kk, vv, alibi_bias(s), sink)
    return o.reshape(b, s, D_A)


def pool_mix(u_ext, valid_ext, pool_w, pool_scale):
    b, te, _ = u_ext.shape
    t = te - POOL_CTX
    uf = u_ext.astype(F32) * valid_ext[None, :, None]
    cs = jnp.concatenate([jnp.zeros((b, 1, D_POOL), F32), jnp.cumsum(uf, axis=1)], axis=1)
    cnt = jnp.concatenate([jnp.zeros((1,), F32), jnp.cumsum(valid_ext)])
    u_now = uf[:, POOL_CTX:]
    outs = []
    for gi, w in enumerate(POOL_WINDOWS):
        sl = slice(gi * POOL_GW, (gi + 1) * POOL_GW)
        lo = POOL_CTX + 1 - w
        total = cs[:, POOL_CTX + 1:, sl] - cs[:, lo:lo + t, sl]
        n = cnt[POOL_CTX + 1:] - cnt[lo:lo + t]
        d = total / n[None, :, None] - u_now[:, :, sl]
        outs.append(d @ pool_w[gi].astype(F32))
    return jnp.concatenate(outs, axis=-1) * pool_scale.astype(F32)


def rwkv7_mix(p, prev, s0, mu, w0, w2, a0, a2, g2, k_k, k_a, r_k, lnx_g, lnx_b):
    b, t, _ = p.shape
    pf = p.astype(F32)
    p_prev = jnp.concatenate([prev.astype(F32), pf[:, :-1]], axis=1)
    xs = pf + mu.astype(F32) * (p_prev - pf)
    r, k, v, wd, ad, gd = jnp.split(xs, [D_R, 2 * D_R, 3 * D_R, 3 * D_R + R_W, 3 * D_R + R_W + R_A], axis=-1)
    w_log = -jax.nn.softplus(-(w0 + jnp.tanh(wd) @ w2)) - 0.5
    decay = jnp.exp(-jnp.exp(w_log))
    a = jax.nn.sigmoid(a0 + ad @ a2)
    g = jax.nn.sigmoid(gd) @ g2
    heads = lambda z: z.reshape(b, t, R_HEADS, HEAD_DIM)
    kk = heads(k * k_k)
    kk = kk / jnp.maximum(jnp.sqrt(jnp.sum(kk * kk, axis=-1, keepdims=True)), 1e-12)
    k = k * (1.0 + (a - 1.0) * k_a)
    r_h, k_h, v_h, w_h, a_h = heads(r), heads(k), heads(v), heads(decay), heads(a)
    b_h = kk * a_h

    def step(s, inp):
        r_t, w_t, k_t, v_t, kk_t, b_t = inp
        sa = jnp.einsum('bhvk,bhk->bhv', s, -kk_t)
        s = s * w_t[:, :, None, :] + sa[..., None] * b_t[:, :, None, :] + v_t[..., None] * k_t[:, :, None, :]
        return s, jnp.einsum('bhvk,bhk->bhv', s, r_t)

    seq = tuple(jnp.moveaxis(z, 1, 0) for z in (r_h, w_h, k_h, v_h, kk, b_h))
    s_fin, y = lax.scan(step, s0.astype(F32), seq)
    y = jnp.moveaxis(y, 0, 1)
    mean = jnp.mean(y, axis=-1, keepdims=True)
    var = jnp.mean(jnp.square(y - mean), axis=-1, keepdims=True)
    y = ((y - mean) * lax.rsqrt(var + GN_EPS)).reshape(b, t, D_R) * lnx_g + lnx_b
    bonus = jnp.sum(r_h * k_h * r_k, axis=-1, keepdims=True) * v_h
    return (y + bonus.reshape(b, t, D_R)) * g, s_fin


def moe_ffn(h, router_w, wg, wu, wd):
    shp = h.shape
    tok = h.reshape(-1, D_MODEL)
    logits = tok.astype(F32) @ router_w.astype(F32)
    top_v, top_i = lax.top_k(logits, TOP_K)
    gates = jax.nn.softmax(top_v, axis=-1)
    dense_gate = jnp.sum(jax.nn.one_hot(top_i, N_EXPERTS, dtype=F32) * gates[..., None], axis=1)
    out = jnp.zeros(tok.shape, F32)
    for e in range(N_EXPERTS):
        out = out + dense_gate[:, e:e + 1] * swiglu(tok, wg[e], wu[e], wd[e]).astype(F32)
    return out.astype(h.dtype).reshape(shp)


def setup_inputs(seed: int = 0) -> dict:
    key = jax.random.key(seed)
    keys = iter(jax.random.split(key, 48))

    def nrm(shape, scale):
        return scale * jax.random.normal(next(keys), shape, F32)

    d = {}
    d['x_prompt'] = nrm((BATCH, SEQ, D_MODEL), 1.0)
    d['x_sample'] = nrm((DEC_BATCH, DEC_SEQ, D_MODEL), 1.0)
    d['cache_k'] = nrm((DEPTH, DEC_BATCH, WINDOW, A_KV_HEADS, HEAD_DIM), 1.0)
    d['cache_v'] = nrm((DEPTH, DEC_BATCH, WINDOW, A_KV_HEADS, HEAD_DIM), 1.0)
    d['state_pool'] = nrm((DEPTH, DEC_BATCH, POOL_CTX, D_POOL), 1.0)
    d['state_shift'] = nrm((DEPTH, DEC_BATCH, 1, D_R_IN), 1.0)
    d['state_wkv'] = nrm((DEPTH, DEC_BATCH, R_HEADS, HEAD_DIM, HEAD_DIM), 0.5)
    d['norm1_g'] = 1.0 + nrm((DEPTH, D_MODEL), 0.02)
    d['w_in'] = nrm((DEPTH, D_MODEL, D_IN), D_MODEL ** -0.5)
    d['q_gain'] = 1.0 + nrm((DEPTH, HEAD_DIM), 0.02)
    d['k_gain'] = 1.0 + nrm((DEPTH, HEAD_DIM), 0.02)
    d['attn_sink'] = nrm((DEPTH, A_HEADS), 0.5)
    d['pool_w'] = nrm((DEPTH, POOL_GROUPS, POOL_GW, POOL_GW), POOL_GW ** -0.5)
    d['pool_scale'] = 0.5 + nrm((DEPTH, D_POOL), 0.1)
    d['shift_mu'] = jax.random.uniform(next(keys), (DEPTH, D_R_IN), F32)
    d['decay_w0'] = nrm((DEPTH, D_R), 0.5)
    d['decay_w2'] = nrm((DEPTH, R_W, D_R), R_W ** -0.5)
    d['iclr_a0'] = nrm((DEPTH, D_R), 0.5)
    d['iclr_a2'] = nrm((DEPTH, R_A, D_R), R_A ** -0.5)
    d['gate_g2'] = nrm((DEPTH, R_G, D_R), R_G ** -0.5)
    d['k_k'] = 0.85 + nrm((DEPTH, D_R), 0.05)
    d['k_a'] = 1.0 + nrm((DEPTH, D_R), 0.05)
    d['r_k'] = nrm((DEPTH, R_HEADS, HEAD_DIM), 0.1)
    d['lnx_g'] = 1.0 + nrm((DEPTH, D_R), 0.02)
    d['lnx_b'] = nrm((DEPTH, D_R), 0.02)
    d['w_out'] = nrm((DEPTH, D_MIX, D_MODEL), D_MIX ** -0.5)
    d['norm2_g'] = 1.0 + nrm((DEPTH, D_MODEL), 0.02)
    d['ffn_wg'] = nrm((N_DENSE, D_MODEL, D_FF), D_MODEL ** -0.5)
    d['ffn_wu'] = nrm((N_DENSE, D_MODEL, D_FF), D_MODEL ** -0.5)
    d['ffn_wd'] = nrm((N_DENSE, D_FF, D_MODEL), D_FF ** -0.5)
    d['router_w'] = nrm((N_MOE, D_MODEL, N_EXPERTS), D_MODEL ** -0.5)
    d['moe_wg'] = nrm((N_MOE, N_EXPERTS, D_MODEL, D_FF_EXPERT), D_MODEL ** -0.5)
    d['moe_wu'] = nrm((N_MOE, N_EXPERTS, D_MODEL, D_FF_EXPERT), D_MODEL ** -0.5)
    d['moe_wd'] = nrm((N_MOE, N_EXPERTS, D_FF_EXPERT, D_MODEL), D_FF_EXPERT ** -0.5)
    return d


def reference(x_prompt, x_sample, cache_k, cache_v, state_pool, state_shift, state_wkv,
              norm1_g, w_in, q_gain, k_gain, attn_sink, pool_w, pool_scale,
              shift_mu, decay_w0, decay_w2, iclr_a0, iclr_a2, gate_g2, k_k, k_a, r_k,
              lnx_g, lnx_b, w_out, norm2_g, ffn_wg, ffn_wu, ffn_wd,
              router_w, moe_wg, moe_wu, moe_wd):

    def trunk(x, prompt):
        b, t = x.shape[:2]
        nk, nv, npool, nshift, nwkv = [], [], [], [], []
        for l in range(DEPTH):
            h = rms_norm(x, norm1_g[l])
            proj = h @ w_in[l]
            pa, pb, pc = jnp.split(proj, [D_A_IN, D_A_IN + D_POOL], axis=-1)
            q, k, v = jnp.split(pa, [D_A, D_A + D_KV], axis=-1)
            q = rms_norm(q.reshape(b, t, A_HEADS, HEAD_DIM), q_gain[l])
            k = rms_norm(k.reshape(b, t, A_KV_HEADS, HEAD_DIM), k_gain[l])
            v = v.reshape(b, t, A_KV_HEADS, HEAD_DIM)
            sink = attn_sink[l].reshape(A_KV_HEADS, A_GROUP)
            if prompt:
                ya = attn_prompt(q, k, v, sink)
                nk.append(k[:, -WINDOW:])
                nv.append(v[:, -WINDOW:])
                pool_prefix = jnp.zeros((b, POOL_CTX, D_POOL), pb.dtype)
                pool_valid = jnp.concatenate([jnp.zeros((POOL_CTX,), F32), jnp.ones((t,), F32)])
                shift_prev = jnp.zeros((b, 1, D_R_IN), pc.dtype)
                wkv0 = jnp.zeros((b, R_HEADS, HEAD_DIM, HEAD_DIM), F32)
            else:
                ya = attn_sample(q, k, v, cache_k[l], cache_v[l], sink)
                nk.append(k)
                nv.append(v)
                pool_prefix = state_pool[l].astype(pb.dtype)
                pool_valid = jnp.ones((POOL_CTX + t,), F32)
                shift_prev = state_shift[l]
                wkv0 = state_wkv[l]
            u_ext = jnp.concatenate([pool_prefix, pb], axis=1)
            yb = pool_mix(u_ext, pool_valid, pool_w[l], pool_scale[l])
            npool.append(u_ext[:, -POOL_CTX:])
            yc, s_fin = rwkv7_mix(pc, shift_prev, wkv0, shift_mu[l], decay_w0[l], decay_w2[l],
                                  iclr_a0[l], iclr_a2[l], gate_g2[l], k_k[l], k_a[l], r_k[l],
                                  lnx_g[l], lnx_b[l])
            nshift.append(pc[:, -1:])
            nwkv.append(s_fin)
            mix = jnp.concatenate([ya.astype(F32), yb, yc], axis=-1).astype(x.dtype)
            x = x + mix @ w_out[l]
            h2 = rms_norm(x, norm2_g[l])
            if l % 2 == 0:
                f = swiglu(h2, ffn_wg[l // 2], ffn_wu[l // 2], ffn_wd[l // 2])
            else:
                f = moe_ffn(h2, router_w[l // 2], moe_wg[l // 2], moe_wu[l // 2], moe_wd[l // 2])
            x = x + f.astype(x.dtype)
        return x, jnp.stack(nk), jnp.stack(nv), jnp.stack(npool), jnp.stack(nshift), jnp.stack(nwkv)

    y_prompt, pk, pv, ppool, pshift, pwkv = trunk(x_prompt, True)
    y_sample, sk, sv, spool, sshift, swkv = trunk(x_sample, False)
    return (y_prompt, y_sample, pk, pv, ppool, pshift, pwkv, sk, sv, spool, sshift, swkv)
```

```python
import functools

import jax
import jax.numpy as jnp
from jax import lax
from jax.experimental import pallas as pl
from jax.experimental.pallas import tpu as pltpu

F32 = jnp.float32
BF16 = jnp.bfloat16

HEAD_DIM = 64
A_HEADS = 8
A_KV_HEADS = 2
A_GROUP = A_HEADS // A_KV_HEADS
D_A = A_HEADS * HEAD_DIM
D_KV = A_KV_HEADS * HEAD_DIM
WINDOW = 128
CHUNK = 64
POOL_WINDOWS = (2, 4, 8, 16)
POOL_CTX = 15
POOL_HALO = 16
D_POOL = 256
POOL_GW = D_POOL // len(POOL_WINDOWS)
D_R = 256
R_HEADS = D_R // HEAD_DIM
R_WA = 128
R_G = 128
D_R_IN = 3 * D_R + R_WA + R_G
TOP_K = 2
NORM_EPS = 1e-6
GN_EPS = 64e-5
NEG_INF = -1e30
VMEM_LIMIT_BYTES = 56 * 1024 * 1024


def _cparams(*sem):
    return pltpu.CompilerParams(dimension_semantics=sem, vmem_limit_bytes=VMEM_LIMIT_BYTES)


def _mm(a, b):
    return jnp.dot(a.astype(BF16), b.astype(BF16), preferred_element_type=F32)


def _mm_nt(a, b):
    return lax.dot_general(a.astype(BF16), b.astype(BF16), (((1,), (1,)), ((), ())),
                           preferred_element_type=F32)


def _split2(x):
    hi = x.astype(BF16)
    lo = (x - hi.astype(F32)).astype(BF16)
    return hi, lo


def _split3(x):
    hi = x.astype(BF16)
    r1 = x - hi.astype(F32)
    mid = r1.astype(BF16)
    lo = (r1 - mid.astype(F32)).astype(BF16)
    return hi, mid, lo


def _mm_exact_rhs(x, b):
    bb = b.astype(BF16)
    hi, mid, lo = _split3(x)
    dot = lambda u: jnp.dot(u, bb, preferred_element_type=F32)
    return dot(hi) + dot(mid) + dot(lo)


def _mm_exact_lhs(a, x):
    ab = a.astype(BF16)
    hi, mid, lo = _split3(x)
    dot = lambda u: jnp.dot(ab, u, preferred_element_type=F32)
    return dot(hi) + dot(mid) + dot(lo)


def _mm_hi(a, b):
    ah, al = _split2(a)
    bh, bl = _split2(b)
    dot = lambda u, v: jnp.dot(u, v, preferred_element_type=F32)
    return dot(ah, bh) + (dot(ah, bl) + dot(al, bh))


def _sigmoid(x):
    return 1.0 / (1.0 + jnp.exp(-x))


def _silu(x):
    return x * _sigmoid(x)


def _norm_matmul_kernel(x_ref, g_ref, w_ref, o_ref):
    x = x_ref[...]
    h = x * lax.rsqrt(jnp.mean(x * x, axis=-1, keepdims=True) + NORM_EPS) * g_ref[...]
    o_ref[...] = jnp.dot(h.astype(BF16), w_ref[...], preferred_element_type=F32)


def _norm_matmul(x, g, w, tm):
    n, d = x.shape
    dout = w.shape[1]
    return pl.pallas_call(
        _norm_matmul_kernel,
        grid=(n // tm,),
        in_specs=[pl.BlockSpec((tm, d), lambda i: (i, 0)),
                  pl.BlockSpec((1, d), lambda i: (0, 0)),
                  pl.BlockSpec((d, dout), lambda i: (0, 0))],
        out_specs=pl.BlockSpec((tm, dout), lambda i: (i, 0)),
        out_shape=jax.ShapeDtypeStruct((n, dout), F32),
        compiler_params=_cparams("parallel"),
        name="norm_in_proj",
    )(x, g, w)


def _head_norm(z, g):
    return z * lax.rsqrt(jnp.mean(z * z, axis=-1, keepdims=True) + NORM_EPS) * g


def _attn_kernel(sink_ref, q_ref, kv_ref, prev_ref, bias_ref, qg_ref, kg_ref, ya_ref, kn_ref,
                 *, cq, qb, prev_is_raw):
    i = pl.program_id(1)
    band = WINDOW + cq
    q = q_ref[0]
    kv = kv_ref[0]
    pv = prev_ref[0]
    qg = qg_ref[...] * (HEAD_DIM ** -0.5)
    kg = kg_ref[...]
    for g in range(A_KV_HEADS):
        ks = slice(g * HEAD_DIM, (g + 1) * HEAD_DIM)
        vs = slice(D_KV + g * HEAD_DIM, D_KV + (g + 1) * HEAD_DIM)
        k_cur = _head_norm(kv[:, ks], kg)
        kn_ref[0, :, ks] = k_cur
        k_prev = _head_norm(pv[:, ks], kg) if prev_is_raw else pv[:, ks]
        k_all = jnp.concatenate([k_prev, k_cur], axis=0).astype(BF16)
        v_all = jnp.concatenate([pv[:, vs], kv[:, vs]], axis=0).astype(BF16)
        heads = [g * A_GROUP + hh for hh in range(A_GROUP)]
        qn = [_head_norm(q[:, h * HEAD_DIM:(h + 1) * HEAD_DIM], qg).astype(BF16) for h in heads]
        sink_col = jnp.concatenate([jnp.full((cq, 1), sink_ref[h], F32) for h in heads], axis=0)
        bias = bias_ref[g]
        for c in range(qb):
            rows = slice(c * cq, (c + 1) * cq)
            qs = jnp.concatenate([z[rows] for z in qn], axis=0)
            kb = k_all[c * cq:c * cq + band]
            vb = v_all[c * cq:c * cq + band]
            s = lax.dot_general(qs, kb, (((1,), (1,)), ((), ())), preferred_element_type=F32) + bias
            if prev_is_raw:
                kpos = (i * qb + c) * cq - WINDOW + lax.broadcasted_iota(jnp.int32, (1, band), 1)
                s = s + jnp.where(kpos >= 0, 0.0, NEG_INF)
            m = jnp.maximum(jnp.max(s, axis=-1, keepdims=True), sink_col)
            p = jnp.exp(s - m)
            den = jnp.sum(p, axis=-1, keepdims=True) + jnp.exp(sink_col - m)
            o = jnp.dot(p.astype(BF16), vb, preferred_element_type=F32) / den
            for hh, h in enumerate(heads):
                ya_ref[0, rows, h * HEAD_DIM:(h + 1) * HEAD_DIM] = o[hh * cq:(hh + 1) * cq].astype(BF16)


def _alibi_bias(cq):
    i = jnp.arange(cq)[:, None]
    j = jnp.arange(WINDOW + cq)[None, :]
    dist = jnp.abs(WINDOW + i - j).astype(F32)
    slopes = jnp.exp2(-8.0 * jnp.arange(1, A_HEADS + 1, dtype=F32) / A_HEADS)
    return (-slopes[:, None, None] * dist).reshape(A_KV_HEADS, A_GROUP * cq, WINDOW + cq)


def _attention(proj, cache_kv, q_gain, k_gain, sink, cq, qb):
    b, t, _ = proj.shape
    tq = cq * qb
    prompt = cache_kv is None
    kv_col = D_A // (2 * D_KV)
    if prompt:
        prev_arr = proj
        prev_spec = pl.BlockSpec((1, WINDOW, 2 * D_KV),
                                 lambda bi, i: (bi, jnp.maximum(i * (tq // WINDOW) - 1, 0), kv_col))
    else:
        prev_arr = cache_kv
        prev_spec = pl.BlockSpec((1, WINDOW, 2 * D_KV), lambda bi, i: (bi, 0, 0))
    kern = functools.partial(_attn_kernel, cq=cq, qb=qb, prev_is_raw=prompt)
    return pl.pallas_call(
        kern,
        grid=(b, t // tq),
        in_specs=[pl.BlockSpec(memory_space=pltpu.SMEM),
                  pl.BlockSpec((1, tq, D_A), lambda bi, i: (bi, i, 0)),
                  pl.BlockSpec((1, tq, 2 * D_KV), lambda bi, i: (bi, i, kv_col)),
                  prev_spec,
                  pl.BlockSpec((A_KV_HEADS, A_GROUP * cq, WINDOW + cq), lambda bi, i: (0, 0, 0)),
                  pl.BlockSpec((1, HEAD_DIM), lambda bi, i: (0, 0)),
                  pl.BlockSpec((1, HEAD_DIM), lambda bi, i: (0, 0))],
        out_specs=[pl.BlockSpec((1, tq, D_A), lambda bi, i: (bi, i, 0)),
                   pl.BlockSpec((1, tq, D_KV), lambda bi, i: (bi, i, 0))],
        out_shape=[jax.ShapeDtypeStruct((b, t, D_A), BF16),
                   jax.ShapeDtypeStruct((b, t, D_KV), F32)],
        compiler_params=_cparams("parallel", "arbitrary"),
        name="swa_attention",
    )(sink, proj, proj, prev_arr, _alibi_bias(cq), q_gain, k_gain)


def _pool_kernel(u_ref, halo_ref, prefix_ref, w_ref, scale_ref, yb_ref, *, n_prefix):
    i = pl.program_id(1)
    u = u_ref[0]
    tb = u.shape[0]
    halo = jnp.where(i == 0, prefix_ref[0], halo_ref[0])
    ext = jnp.concatenate([halo, u], axis=0)
    col = lax.broadcasted_iota(jnp.int32, (1, D_POOL), 1)
    pos = i * tb + lax.broadcasted_iota(jnp.int32, (tb, 1), 0)
    total = None
    count = None
    acc = ext
    span = 1
    for gi, w in enumerate(POOL_WINDOWS):
        while span < w:
            acc = acc + pltpu.roll(acc, span, axis=0)
            span *= 2
        in_group = (col >= gi * POOL_GW) & (col < (gi + 1) * POOL_GW)
        tail = acc[POOL_HALO:]
        total = jnp.where(in_group, tail, 0.0 if total is None else total)
        cnt = jnp.minimum(pos + (1 + n_prefix), w).astype(F32)
        count = jnp.where(in_group, cnt, 1.0 if count is None else count)
    d = total / count - u
    yb_ref[0] = (_mm(d, w_ref[...]) * scale_ref[...]).astype(BF16)


def _pool_mix(proj, prefix, w_blockdiag, scale, n_prefix, tb):
    b, t, _ = proj.shape
    col = (D_A + 2 * D_KV) // D_POOL
    kern = functools.partial(_pool_kernel, n_prefix=n_prefix)
    return pl.pallas_call(
        kern,
        grid=(b, t // tb),
        in_specs=[pl.BlockSpec((1, tb, D_POOL), lambda bi, i: (bi, i, col)),
                  pl.BlockSpec((1, POOL_HALO, D_POOL),
                               lambda bi, i: (bi, jnp.maximum(i * (tb // POOL_HALO) - 1, 0), col)),
                  pl.BlockSpec((1, POOL_HALO, D_POOL), lambda bi, i: (bi, 0, 0)),
                  pl.BlockSpec((D_POOL, D_POOL), lambda bi, i: (0, 0)),
                  pl.BlockSpec((1, D_POOL), lambda bi, i: (0, 0))],
        out_specs=pl.BlockSpec((1, tb, D_POOL), lambda bi, i: (bi, i, 0)),
        out_shape=jax.ShapeDtypeStruct((b, t, D_POOL), BF16),
        compiler_params=_cparams("parallel", "arbitrary"),
        name="pool_mix",
    )(proj, proj, prefix, w_blockdiag, scale)


def _rwkv_kernel(p_ref, prev_ref, st0_ref, mu_ref, w0_ref, a0_ref, w2_ref, a2_ref, g2_ref,
                 kk_ref, ka_ref, rk_ref, lng_ref, lnb_ref, yc_ref, st_ref, carry_ref, y_ref,
                 *, chunk):
    j = pl.program_id(1)

    @pl.when(j == 0)
    def _():
        carry_ref[...] = prev_ref[0]
        st_ref[0] = st0_ref[0]

    p = p_ref[0]
    tb = p.shape[0]
    n_chunk = tb // chunk
    hl = R_HEADS * chunk

    row = lax.broadcasted_iota(jnp.int32, (tb, 1), 0)
    p_prev = jnp.where(row == 0, carry_ref[...], pltpu.roll(p, 1, axis=0))
    carry_ref[...] = p[tb - 1:tb]
    xs = p + mu_ref[...] * (p_prev - p)
    r = xs[:, 0:D_R]
    k = xs[:, D_R:2 * D_R]
    v = xs[:, 2 * D_R:3 * D_R]
    wa = xs[:, 3 * D_R:3 * D_R + R_WA]
    gd = xs[:, 3 * D_R + R_WA:]

    z = -(w0_ref[...] + _mm(jnp.tanh(wa), w2_ref[...]))
    softplus = jnp.maximum(z, 0.0) + jnp.log(1.0 + jnp.exp(-jnp.abs(z)))
    lw = -jnp.exp(-softplus - 0.5)
    a = _sigmoid(a0_ref[...] + _mm(wa, a2_ref[...]))
    g = _mm(_sigmoid(gd), g2_ref[...])

    lane_r = lax.broadcasted_iota(jnp.int32, (D_R, D_R), 0) // HEAD_DIM
    lane_c = lax.broadcasted_iota(jnp.int32, (D_R, D_R), 1) // HEAD_DIM
    head_ones = (lane_r == lane_c).astype(F32)
    seg_sum = lambda t: _mm_exact_rhs(t, head_ones)

    kk = k * kk_ref[...]
    kk = kk / jnp.maximum(jnp.sqrt(seg_sum(kk * kk)), 1e-12)
    k2 = k * (1.0 + (a - 1.0) * ka_ref[...])
    bb = kk * a

    ti = lax.broadcasted_iota(jnp.int32, (tb, tb), 0)
    tj = lax.broadcasted_iota(jnp.int32, (tb, tb), 1)
    cum = _mm_exact_lhs(((ti // chunk == tj // chunk) & (tj <= ti)).astype(F32), lw)

    si = lax.broadcasted_iota(jnp.int32, (hl, hl), 0)
    sj = lax.broadcasted_iota(jnp.int32, (hl, hl), 1)
    same = (si // chunk) == (sj // chunk)
    strict = same & (sj < si)
    incl = same & (sj <= si)
    eye_hl = (si == sj).astype(F32)
    eye_k = lax.broadcasted_iota(jnp.int32, (D_R, D_R), 0) == lax.broadcasted_iota(jnp.int32, (D_R, D_R), 1)
    lane = lax.broadcasted_iota(jnp.int32, (1, D_R), 1) // HEAD_DIM
    head_mask = [(lane == h).astype(F32) for h in range(R_HEADS)]
    stack = lambda t: jnp.concatenate([t * m for m in head_mask], axis=0)

    for c in range(n_chunk):
        sl = slice(c * chunk, (c + 1) * chunk)
        cum_c = cum[sl]
        cum_last = cum_c[chunk - 1:chunk]
        g_in = jnp.exp(cum_c)
        g_prev = jnp.exp(cum_c - lw[sl])
        g_inv = jnp.exp(-cum_c)
        g_out = jnp.exp(cum_last - cum_c)
        a_s = stack(kk[sl] * g_prev)
        r_s = stack(r[sl] * g_in)
        b_s = stack(bb[sl] * g_inv)
        k_s = stack(k2[sl] * g_inv)
        bo_s = stack(bb[sl] * g_out)
        ko_s = stack(k2[sl] * g_out)
        v_s = stack(v[sl])

        ar = jnp.concatenate([a_s, r_s], axis=0)
        m_b = _mm_nt(ar, b_s)
        m_k = _mm_nt(ar, k_s)
        m_ab = jnp.where(strict, m_b[:hl], 0.0)
        m_rb = jnp.where(incl, m_b[hl:], 0.0)
        m_ak = jnp.where(strict, m_k[:hl], 0.0)
        m_rk = jnp.where(incl, m_k[hl:], 0.0)

        t_inv = eye_hl - m_ab
        pw = m_ab
        n = 1
        while 2 * n < chunk:
            pw = _mm_hi(pw, pw)
            t_inv = t_inv + _mm_hi(t_inv, pw)
            n *= 2

        a_bar = _mm_hi(t_inv, a_s)
        u0 = -_mm_hi(t_inv, _mm(m_ak, v_s))
        r_bar = r_s - _mm(m_rb, a_bar)
        y0 = _mm(m_rk, v_s) + _mm(m_rb, u0)
        g_t = jnp.where(eye_k, jnp.exp(cum_last), 0.0) - _mm_hi(bo_s.T, a_bar)
        h_t = _mm(bo_s.T, u0) + _mm(ko_s.T, v_s)

        st = st_ref[0]
        y_s = y0 + _mm(r_bar, st)
        st_ref[0] = _mm_hi(g_t, st) + h_t
        y_c = y_s[0:chunk]
        for h in range(1, R_HEADS):
            y_c = y_c + y_s[h * chunk:(h + 1) * chunk]
        y_ref[sl, :] = y_c

    y = y_ref[...]
    mean = seg_sum(y) * (1.0 / HEAD_DIM)
    d = y - mean
    var = seg_sum(d * d) * (1.0 / HEAD_DIM)
    yn = d * lax.rsqrt(var + GN_EPS) * lng_ref[...] + lnb_ref[...]
    bonus = seg_sum(r * k2 * rk_ref[...]) * v
    yc_ref[0] = ((yn + bonus) * g).astype(BF16)


def _rwkv_mix(proj, prev, st0, lp, chunk, tb):
    b, t, _ = proj.shape
    col = (D_A + 2 * D_KV + D_POOL) // D_R_IN
    row = lambda n: pl.BlockSpec((1, n), lambda bi, i: (0, 0))
    full = lambda s: pl.BlockSpec(s, lambda bi, i: (0,) * len(s))
    kern = functools.partial(_rwkv_kernel, chunk=chunk)
    return pl.pallas_call(
        kern,
        grid=(b, t // tb),
        in_specs=[pl.BlockSpec((1, tb, D_R_IN), lambda bi, i: (bi, i, col)),
                  pl.BlockSpec((1, 1, D_R_IN), lambda bi, i: (bi, 0, 0)),
                  pl.BlockSpec((1, D_R, D_R), lambda bi, i: (bi, 0, 0)),
                  row(D_R_IN), row(D_R), row(D_R),
                  full((R_WA, D_R)), full((R_WA, D_R)), full((R_G, D_R)),
                  row(D_R), row(D_R), row(D_R), row(D_R), row(D_R)],
        out_specs=[pl.BlockSpec((1, tb, D_R), lambda bi, i: (bi, i, 0)),
                   pl.BlockSpec((1, D_R, D_R), lambda bi, i: (bi, 0, 0))],
        out_shape=[jax.ShapeDtypeStruct((b, t, D_R), BF16),
                   jax.ShapeDtypeStruct((b, D_R, D_R), F32)],
        scratch_shapes=[pltpu.VMEM((1, D_R_IN), F32), pltpu.VMEM((tb, D_R), F32)],
        compiler_params=_cparams("parallel", "arbitrary"),
        name="rwkv7_mix",
    )(proj, prev, st0, lp["mu"], lp["w0"], lp["a0"], lp["w2"], lp["a2"], lp["g2"],
      lp["k_k"], lp["k_a"], lp["r_k"], lp["lnx_g"], lp["lnx_b"])


def _state_to_blockdiag(s):
    b = s.shape[0]
    st = jnp.swapaxes(s, -1, -2)
    eye = jnp.eye(R_HEADS, dtype=s.dtype)
    return jnp.einsum("bhkv,hg->bhkgv", st, eye).reshape(b, D_R, D_R)


def _blockdiag_to_state(st):
    b = st.shape[0]
    blocks = st.reshape(b, R_HEADS, HEAD_DIM, R_HEADS, HEAD_DIM)
    diag = jnp.stack([blocks[:, h, :, h, :] for h in range(R_HEADS)], axis=1)
    return jnp.swapaxes(diag, -1, -2)


def _mix_out_kernel(x_ref, ya_ref, yb_ref, yc_ref, w_ref, g_ref, *rest, route):
    if route:
        rw_ref, xm_ref, h_ref, gate_ref = rest
    else:
        xm_ref, h_ref = rest
    dot = lambda u, lo, hi: jnp.dot(u[...], w_ref[lo:hi, :], preferred_element_type=F32)
    xm = (x_ref[...] + dot(ya_ref, 0, D_A) + dot(yb_ref, D_A, D_A + D_POOL)
          + dot(yc_ref, D_A + D_POOL, D_A + D_POOL + D_R))
    xm_ref[...] = xm
    h = xm * lax.rsqrt(jnp.mean(xm * xm, axis=-1, keepdims=True) + NORM_EPS) * g_ref[...]
    h_ref[...] = h.astype(BF16)
    if route:
        logits = _mm_hi(h, rw_ref[...])
        n_e = logits.shape[-1]
        lane = lax.broadcasted_iota(jnp.int32, logits.shape, 1).astype(F32)
        m1 = jnp.max(logits, axis=-1, keepdims=True)
        i1 = jnp.min(jnp.where(logits == m1, lane, float(n_e)), axis=-1, keepdims=True)
        rest_l = jnp.where(lane == i1, -jnp.inf, logits)
        m2 = jnp.max(rest_l, axis=-1, keepdims=True)
        i2 = jnp.min(jnp.where(rest_l == m2, lane, float(n_e)), axis=-1, keepdims=True)
        e2 = jnp.exp(m2 - m1)
        gate_ref[...] = (jnp.where(lane == i1, 1.0 / (1.0 + e2), 0.0)
                         + jnp.where(lane == i2, e2 / (1.0 + e2), 0.0))


def _mix_out(x, ya, yb, yc, w_out, g, router_w, tm):
    n, d = x.shape
    route = router_w is not None
    tile = lambda w: pl.BlockSpec((tm, w), lambda i: (i, 0))
    in_specs = [tile(d), tile(D_A), tile(D_POOL), tile(D_R),
                pl.BlockSpec(w_out.shape, lambda i: (0, 0)),
                pl.BlockSpec((1, d), lambda i: (0, 0))]
    out_specs = [tile(d), tile(d)]
    out_shape = [jax.ShapeDtypeStruct((n, d), F32), jax.ShapeDtypeStruct((n, d), BF16)]
    args = [x, ya, yb, yc, w_out, g]
    if route:
        n_e = router_w.shape[1]
        in_specs.append(pl.BlockSpec(router_w.shape, lambda i: (0, 0)))
        out_specs.append(tile(n_e))
        out_shape.append(jax.ShapeDtypeStruct((n, n_e), F32))
        args.append(router_w)
    return pl.pallas_call(
        functools.partial(_mix_out_kernel, route=route),
        grid=(n // tm,),
        in_specs=in_specs, out_specs=out_specs, out_shape=out_shape,
        compiler_params=_cparams("parallel"),
        name="mix_out_proj",
    )(*args)


def _ffn_kernel(h_ref, xm_ref, wg_ref, wu_ref, wd_ref, o_ref, acc_ref):
    j = pl.program_id(1)

    @pl.when(j == 0)
    def _():
        acc_ref[...] = xm_ref[...]

    h = h_ref[...]
    act = (_silu(jnp.dot(h, wg_ref[...], preferred_element_type=F32))
           * jnp.dot(h, wu_ref[...], preferred_element_type=F32))
    acc_ref[...] += jnp.dot(act.astype(BF16), wd_ref[...], preferred_element_type=F32)

    @pl.when(j == pl.num_programs(1) - 1)
    def _():
        o_ref[...] = acc_ref[...]


def _ffn(h, xm, wg, wu, wd, tm, tf):
    n, d = xm.shape
    f = wg.shape[1]
    return pl.pallas_call(
        _ffn_kernel,
        grid=(n // tm, f // tf),
        in_specs=[pl.BlockSpec((tm, d), lambda i, j: (i, 0)),
                  pl.BlockSpec((tm, d), lambda i, j: (i, 0)),
                  pl.BlockSpec((d, tf), lambda i, j: (0, j)),
                  pl.BlockSpec((d, tf), lambda i, j: (0, j)),
                  pl.BlockSpec((tf, d), lambda i, j: (j, 0))],
        out_specs=pl.BlockSpec((tm, d), lambda i, j: (i, 0)),
        out_shape=jax.ShapeDtypeStruct((n, d), F32),
        scratch_shapes=[pltpu.VMEM((tm, d), F32)],
        compiler_params=_cparams("parallel", "arbitrary"),
        name="swiglu_ffn",
    )(h, xm, wg, wu, wd)


def _moe_kernel(h_ref, xm_ref, gate_ref, wg_ref, wu_ref, wd_ref, o_ref, acc_ref):
    e = pl.program_id(1)
    j = pl.program_id(2)

    @pl.when((e == 0) & (j == 0))
    def _():
        acc_ref[...] = xm_ref[...]

    gates = gate_ref[...]
    lane = lax.broadcasted_iota(jnp.int32, gates.shape, 1)
    gate = jnp.sum(jnp.where(lane == e, gates, 0.0), axis=-1, keepdims=True)
    h = h_ref[...]
    act = (_silu(jnp.dot(h, wg_ref[0], preferred_element_type=F32))
           * jnp.dot(h, wu_ref[0], preferred_element_type=F32))
    acc_ref[...] += gate * jnp.dot(act.astype(BF16), wd_ref[0], preferred_element_type=F32)

    @pl.when((e == pl.num_programs(1) - 1) & (j == pl.num_programs(2) - 1))
    def _():
        o_ref[...] = acc_ref[...]


def _moe(h, xm, gates, wg, wu, wd, tm, tf):
    n, d = xm.shape
    n_e, _, f = wg.shape
    return pl.pallas_call(
        _moe_kernel,
        grid=(n // tm, n_e, f // tf),
        in_specs=[pl.BlockSpec((tm, d), lambda i, e, j: (i, 0)),
                  pl.BlockSpec((tm, d), lambda i, e, j: (i, 0)),
                  pl.BlockSpec((tm, n_e), lambda i, e, j: (i, 0)),
                  pl.BlockSpec((1, d, tf), lambda i, e, j: (e, 0, j)),
                  pl.BlockSpec((1, d, tf), lambda i, e, j: (e, 0, j)),
                  pl.BlockSpec((1, tf, d), lambda i, e, j: (e, j, 0))],
        out_specs=pl.BlockSpec((tm, d), lambda i, e, j: (i, 0)),
        out_shape=jax.ShapeDtypeStruct((n, d), F32),
        scratch_shapes=[pltpu.VMEM((tm, d), F32)],
        compiler_params=_cparams("parallel", "arbitrary", "arbitrary"),
        name="moe_ffn",
    )(h, xm, gates, wg, wu, wd)


def _pick(n, candidates):
    for c in candidates:
        if n % c == 0:
            return c
    raise ValueError(f"no tile for {n}")


def _trunk(x, layers, cache_k, cache_v, state_pool, state_shift, state_wkv):
    prompt = cache_k is None
    b, t, d = x.shape
    n = b * t
    tm = _pick(n, (512, 256, 128))
    tm_ffn = _pick(n, (1024, 512, 256, 128))
    cq = CHUNK if prompt else t
    qb = _pick(t // cq, (4, 2, 1))
    pool_tb = _pick(t, (512, 256, 128, 64, 32))
    r_chunk = CHUNK if prompt else t
    r_tb = _pick(t, (256, 128, 64, 32))
    x2 = x.reshape(n, d)
    nk, nv, npool, nshift, nwkv = [], [], [], [], []
    for l, lp in enumerate(layers):
        proj2 = _norm_matmul(x2, lp["norm1_g"], lp["w_in"], tm)
        proj = proj2.reshape(b, t, -1)
        c_pool = D_A + 2 * D_KV
        c_r = c_pool + D_POOL
        v_raw = proj[:, :, D_A + D_KV:c_pool]
        if prompt:
            cache_kv = None
            prefix = jnp.zeros((b, POOL_HALO, D_POOL), F32)
            prev = jnp.zeros((b, 1, D_R_IN), F32)
            st0 = jnp.zeros((b, D_R, D_R), F32)
        else:
            cache_kv = jnp.concatenate([cache_k[l].reshape(b, WINDOW, D_KV),
                                        cache_v[l].reshape(b, WINDOW, D_KV)], axis=-1)
            prefix = jnp.pad(state_pool[l], ((0, 0), (POOL_HALO - POOL_CTX, 0), (0, 0)))
            prev = state_shift[l]
            st0 = _state_to_blockdiag(state_wkv[l])
        ya, k_norm = _attention(proj, cache_kv, lp["q_gain"], lp["k_gain"], lp["sink"], cq, qb)
        yb = _pool_mix(proj, prefix, lp["pool_w"], lp["pool_scale"], 0 if prompt else POOL_CTX, pool_tb)
        yc, st_fin = _rwkv_mix(proj, prev, st0, lp, r_chunk, r_tb)
        keep = WINDOW if prompt else t
        nk.append(k_norm[:, t - keep:].reshape(b, keep, A_KV_HEADS, HEAD_DIM))
        nv.append(v_raw[:, t - keep:].reshape(b, keep, A_KV_HEADS, HEAD_DIM))
        npool.append(proj[:, t - POOL_CTX:, c_pool:c_r])
        nshift.append(proj[:, t - 1:, c_r:])
        nwkv.append(_blockdiag_to_state(st_fin))
        outs = _mix_out(x2, ya.reshape(n, -1), yb.reshape(n, -1), yc.reshape(n, -1),
                        lp["w_out"], lp["norm2_g"], lp.get("router_w"), tm)
        if "router_w" in lp:
            xm, h2, gates = outs
            x2 = _moe(h2, xm, gates, lp["wg"], lp["wu"], lp["wd"], tm_ffn, 512)
        else:
            xm, h2 = outs
            x2 = _ffn(h2, xm, lp["wg"], lp["wu"], lp["wd"], tm_ffn, 512)
    return (x2.reshape(b, t, d), jnp.stack(nk), jnp.stack(nv), jnp.stack(npool),
            jnp.stack(nshift), jnp.stack(nwkv))


def kernel(x_prompt, x_sample, cache_k, cache_v, state_pool, state_shift, state_wkv, norm1_g, w_in, q_gain, k_gain, attn_sink, pool_w, pool_scale, shift_mu, decay_w0, decay_w2, iclr_a0, iclr_a2, gate_g2, k_k, k_a, r_k, lnx_g, lnx_b, w_out, norm2_g, ffn_wg, ffn_wu, ffn_wd, router_w, moe_wg, moe_wu, moe_wd):
    depth = w_in.shape[0]
    r_w = decay_w2.shape[1]
    layers = []
    for l in range(depth):
        lp = dict(
            norm1_g=norm1_g[l][None], w_in=w_in[l].astype(BF16),
            q_gain=q_gain[l][None], k_gain=k_gain[l][None], sink=attn_sink[l],
            pool_w=jax.scipy.linalg.block_diag(*[pool_w[l, gi] for gi in range(len(POOL_WINDOWS))]).astype(BF16),
            pool_scale=pool_scale[l][None],
            mu=shift_mu[l][None], w0=decay_w0[l][None], a0=iclr_a0[l][None],
            w2=jnp.pad(decay_w2[l], ((0, R_WA - r_w), (0, 0))).astype(BF16),
            a2=jnp.pad(iclr_a2[l], ((r_w, 0), (0, 0))).astype(BF16),
            g2=gate_g2[l].astype(BF16),
            k_k=k_k[l][None], k_a=k_a[l][None], r_k=r_k[l].reshape(1, D_R),
            lnx_g=lnx_g[l][None], lnx_b=lnx_b[l][None],
            w_out=w_out[l].astype(BF16), norm2_g=norm2_g[l][None])
        if l % 2 == 0:
            lp.update(wg=ffn_wg[l // 2].astype(BF16), wu=ffn_wu[l // 2].astype(BF16),
                      wd=ffn_wd[l // 2].astype(BF16))
        else:
            lp.update(router_w=router_w[l // 2], wg=moe_wg[l // 2].astype(BF16),
                      wu=moe_wu[l // 2].astype(BF16), wd=moe_wd[l // 2].astype(BF16))
        layers.append(lp)
    y_p, pk, pv, ppool, pshift, pwkv = _trunk(x_prompt, layers, None, None, None, None, None)
    y_s, sk, sv, spool, sshift, swkv = _trunk(x_sample, layers, cache_k, cache_v, state_pool,
                                              state_shift, state_wkv)
    return (y_p, y_s, pk, pv, ppool, pshift, pwkv, sk, sv, spool, sshift, swkv)
```

```python
import functools

import jax
import jax.numpy as jnp
from jax import lax
from jax.experimental import pallas as pl
from jax.experimental.pallas import tpu as pltpu

F32 = jnp.float32
BF16 = jnp.bfloat16

HEAD_DIM = 64
A_HEADS = 8
A_KV_HEADS = 2
A_GROUP = A_HEADS // A_KV_HEADS
D_A = A_HEADS * HEAD_DIM
D_KV = A_KV_HEADS * HEAD_DIM
WINDOW = 128
CHUNK = 64
POOL_WINDOWS = (2, 4, 8, 16)
POOL_CTX = 15
POOL_HALO = 16
D_POOL = 256
POOL_GW = D_POOL // len(POOL_WINDOWS)
D_R = 256
R_HEADS = D_R // HEAD_DIM
R_WA = 128
R_G = 128
D_R_IN = 3 * D_R + R_WA + R_G
TOP_K = 2
NORM_EPS = 1e-6
GN_EPS = 64e-5
NEG_INF = -1e30
VMEM_LIMIT_BYTES = 56 * 1024 * 1024


def _cparams(*sem):
    return pltpu.CompilerParams(dimension_semantics=sem, vmem_limit_bytes=VMEM_LIMIT_BYTES)


def _mm(a, b):
    return jnp.dot(a.astype(BF16), b.astype(BF16), preferred_element_type=F32)


def _mm_nt(a, b):
    return lax.dot_general(a.astype(BF16), b.astype(BF16), (((1,), (1,)), ((), ())),
                           preferred_element_type=F32)


def _split2(x):
    hi = x.astype(BF16)
    lo = (x - hi.astype(F32)).astype(BF16)
    return hi, lo


def _split3(x):
    hi = x.astype(BF16)
    r1 = x - hi.astype(F32)
    mid = r1.astype(BF16)
    lo = (r1 - mid.astype(F32)).astype(BF16)
    return hi, mid, lo


def _mm_exact_rhs(x, b):
    bb = b.astype(BF16)
    hi, mid, lo = _split3(x)
    dot = lambda u: jnp.dot(u, bb, preferred_element_type=F32)
    return dot(hi) + dot(mid) + dot(lo)


def _mm_exact_lhs(a, x):
    ab = a.astype(BF16)
    hi, mid, lo = _split3(x)
    dot = lambda u: jnp.dot(ab, u, preferred_element_type=F32)
    return dot(hi) + dot(mid) + dot(lo)


def _mm_hi(a, b):
    ah, al = _split2(a)
    bh, bl = _split2(b)
    dot = lambda u, v: jnp.dot(u, v, preferred_element_type=F32)
    return dot(ah, bh) + (dot(ah, bl) + dot(al, bh))


def _sigmoid(x):
    return 1.0 / (1.0 + jnp.exp(-x))


def _silu(x):
    return x * _sigmoid(x)


def _norm_matmul_kernel(x_ref, g_ref, w_ref, o_ref):
    x = x_ref[...]
    h = x * lax.rsqrt(jnp.mean(x * x, axis=-1, keepdims=True) + NORM_EPS) * g_ref[...]
    o_ref[...] = jnp.dot(h.astype(BF16), w_ref[...], preferred_element_type=F32)


def _norm_matmul(x, g, w, tm):
    n, d = x.shape
    dout = w.shape[1]
    return pl.pallas_call(
        _norm_matmul_kernel,
        grid=(n // tm,),
        in_specs=[pl.BlockSpec((tm, d), lambda i: (i, 0)),
                  pl.BlockSpec((1, d), lambda i: (0, 0)),
                  pl.BlockSpec((d, dout), lambda i: (0, 0))],
        out_specs=pl.BlockSpec((tm, dout), lambda i: (i, 0)),
        out_shape=jax.ShapeDtypeStruct((n, dout), F32),
        compiler_params=_cparams("parallel"),
        name="norm_in_proj",
    )(x, g, w)


def _block_ones(n):
    r = lax.broadcasted_iota(jnp.int32, (n, n), 0) // HEAD_DIM
    c = lax.broadcasted_iota(jnp.int32, (n, n), 1) // HEAD_DIM
    return (r == c).astype(BF16)


def _head_rms_scale(z, ones):
    hi, lo = _split2(z * z)
    ss = jnp.dot(hi, ones, preferred_element_type=F32) + jnp.dot(lo, ones, preferred_element_type=F32)
    return lax.rsqrt(ss * (1.0 / HEAD_DIM) + NORM_EPS)


def _attn_kernel(sink_ref, q_ref, kv_ref, prev_ref, bias_ref, qg_ref, kg_ref, ya_ref, kn_ref,
                 *, rq, n_part, prev_is_raw):
    i = pl.program_id(1)
    kp = WINDOW + rq
    gw = A_GROUP * HEAD_DIM
    q = q_ref[0]
    kv = kv_ref[0]
    pv = prev_ref[0]
    ones_q = _block_ones(D_A)
    ones_k = _block_ones(D_KV)
    qn = (q * _head_rms_scale(q, ones_q) * qg_ref[...]).astype(BF16)
    k_cur = kv[:, 0:D_KV]
    k_cur = k_cur * _head_rms_scale(k_cur, ones_k) * kg_ref[...]
    kn_ref[0] = k_cur
    k_prev = pv[:, 0:D_KV]
    if prev_is_raw:
        k_prev = k_prev * _head_rms_scale(k_prev, ones_k) * kg_ref[...]
    k_all = jnp.concatenate([k_prev, k_cur], axis=0).astype(BF16)
    v_all = jnp.concatenate([pv[:, D_KV:], kv[:, D_KV:]], axis=0).astype(BF16)
    nq = A_GROUP * rq
    lane_in = lax.broadcasted_iota(jnp.int32, (D_KV, gw), 0)
    lane_out = lax.broadcasted_iota(jnp.int32, (D_KV, gw), 1)
    lane_out_t = lax.broadcasted_iota(jnp.int32, (gw, D_KV), 0)
    lane_in_t = lax.broadcasted_iota(jnp.int32, (gw, D_KV), 1)
    slot_lane = lax.broadcasted_iota(jnp.int32, (1, gw), 1) // HEAD_DIM
    slot_mask_bf = [(slot_lane == hh).astype(BF16) for hh in range(A_GROUP)]
    slot_row = lax.broadcasted_iota(jnp.int32, (gw, rq), 0) // HEAD_DIM
    key_row = lax.broadcasted_iota(jnp.int32, (kp, nq), 0)
    col_head = lax.broadcasted_iota(jnp.int32, (1, nq), 1) // rq
    for g in range(A_KV_HEADS):
        select = (lane_in // HEAD_DIM == g) & (lane_in % HEAD_DIM == lane_out % HEAD_DIM)
        k_wide = jnp.dot(k_all, select.astype(BF16), preferred_element_type=F32).astype(BF16)
        select_t = (lane_in_t // HEAD_DIM == g) & (lane_in_t % HEAD_DIM == lane_out_t % HEAD_DIM)
        v_wide_t = lax.dot_general(select_t.astype(BF16), v_all, (((1,), (1,)), ((), ())),
                                   preferred_element_type=F32).astype(BF16)
        q_g = qn[:, g * gw:(g + 1) * gw]
        sink_row = jnp.zeros((1, nq), F32)
        for hh in range(A_GROUP):
            sink_row = jnp.where(col_head == hh, sink_ref[g * A_GROUP + hh], sink_row)
        bias_t = bias_ref[g]
        for p in range(n_part):
            rows = slice(p * rq, (p + 1) * rq)
            qs = jnp.concatenate([q_g[rows] * m for m in slot_mask_bf], axis=0)
            kb = k_wide[p * rq:p * rq + kp]
            s = lax.dot_general(kb, qs, (((1,), (1,)), ((), ())), preferred_element_type=F32) + bias_t
            if prev_is_raw and p * rq < WINDOW:
                n_pad = WINDOW - (i * n_part + p) * rq
                s = s + jnp.where(key_row < n_pad, NEG_INF, 0.0)
            m = jnp.maximum(jnp.max(s, axis=0, keepdims=True), sink_row)
            e = jnp.exp(s - m)
            den = jnp.sum(e, axis=0, keepdims=True) + jnp.exp(sink_row - m)
            prob = (e * (1.0 / den)).astype(BF16)
            o_t = jnp.dot(v_wide_t[:, p * rq:p * rq + kp], prob, preferred_element_type=F32)
            out_t = o_t[:, (A_GROUP - 1) * rq:]
            for hh in range(A_GROUP - 2, -1, -1):
                out_t = jnp.where(slot_row == hh, o_t[:, hh * rq:(hh + 1) * rq], out_t)
            ya_ref[0, rows, g * gw:(g + 1) * gw] = out_t.T.astype(BF16)


def _attn_bias(rq, cq):
    kp = WINDOW + rq
    i = jnp.arange(rq)[:, None]
    j = jnp.arange(kp)[None, :]
    jb = j - cq * (i // cq)
    valid = (jb >= 0) & (jb < WINDOW + cq)
    dist = jnp.abs(WINDOW + (i % cq) - jb).astype(F32)
    slopes = jnp.exp2(-8.0 * jnp.arange(1, A_HEADS + 1, dtype=F32) / A_HEADS)
    bias = jnp.where(valid[None], -slopes[:, None, None] * dist[None], NEG_INF)
    return jnp.swapaxes(bias.reshape(A_KV_HEADS, A_GROUP * rq, kp), 1, 2)


def _attention(proj, cache_kv, q_gain, k_gain, sink, cq, rq, n_part):
    b, t, _ = proj.shape
    tq = rq * n_part
    prompt = cache_kv is None
    kv_col = D_A // (2 * D_KV)
    if prompt:
        prev_arr = proj
        prev_spec = pl.BlockSpec((1, WINDOW, 2 * D_KV),
                                 lambda bi, i: (bi, jnp.maximum(i * (tq // WINDOW) - 1, 0), kv_col))
    else:
        prev_arr = cache_kv
        prev_spec = pl.BlockSpec((1, WINDOW, 2 * D_KV), lambda bi, i: (bi, 0, 0))
    kp = WINDOW + rq
    q_gain_t = jnp.tile(q_gain * (HEAD_DIM ** -0.5), (1, A_HEADS))
    k_gain_t = jnp.tile(k_gain, (1, A_KV_HEADS))
    kern = functools.partial(_attn_kernel, rq=rq, n_part=n_part, prev_is_raw=prompt)
    return pl.pallas_call(
        kern,
        grid=(b, t // tq),
        in_specs=[pl.BlockSpec(memory_space=pltpu.SMEM),
                  pl.BlockSpec((1, tq, D_A), lambda bi, i: (bi, i, 0)),
                  pl.BlockSpec((1, tq, 2 * D_KV), lambda bi, i: (bi, i, kv_col)),
                  prev_spec,
                  pl.BlockSpec((A_KV_HEADS, kp, A_GROUP * rq), lambda bi, i: (0, 0, 0)),
                  pl.BlockSpec((1, D_A), lambda bi, i: (0, 0)),
                  pl.BlockSpec((1, D_KV), lambda bi, i: (0, 0))],
        out_specs=[pl.BlockSpec((1, tq, D_A), lambda bi, i: (bi, i, 0)),
                   pl.BlockSpec((1, tq, D_KV), lambda bi, i: (bi, i, 0))],
        out_shape=[jax.ShapeDtypeStruct((b, t, D_A), BF16),
                   jax.ShapeDtypeStruct((b, t, D_KV), F32)],
        compiler_params=_cparams("parallel", "arbitrary"),
        name="swa_attention",
    )(sink, proj, proj, prev_arr, _attn_bias(rq, cq), q_gain_t, k_gain_t)


def _pool_kernel(u_ref, halo_ref, prefix_ref, w_ref, scale_ref, yb_ref, *, n_prefix):
    i = pl.program_id(1)
    u = u_ref[0]
    tb = u.shape[0]
    halo = jnp.where(i == 0, prefix_ref[0], halo_ref[0])
    ext = jnp.concatenate([halo, u], axis=0)
    col = lax.broadcasted_iota(jnp.int32, (1, D_POOL), 1)
    pos = i * tb + lax.broadcasted_iota(jnp.int32, (tb, 1), 0)
    total = None
    count = None
    acc = ext
    span = 1
    for gi, w in enumerate(POOL_WINDOWS):
        while span < w:
            acc = acc + pltpu.roll(acc, span, axis=0)
            span *= 2
        in_group = (col >= gi * POOL_GW) & (col < (gi + 1) * POOL_GW)
        tail = acc[POOL_HALO:]
        total = jnp.where(in_group, tail, 0.0 if total is None else total)
        cnt = jnp.minimum(pos + (1 + n_prefix), w).astype(F32)
        count = jnp.where(in_group, cnt, 1.0 if count is None else count)
    d = total / count - u
    yb_ref[0] = (_mm(d, w_ref[...]) * scale_ref[...]).astype(BF16)


def _pool_mix(proj, prefix, w_blockdiag, scale, n_prefix, tb):
    b, t, _ = proj.shape
    col = (D_A + 2 * D_KV) // D_POOL
    kern = functools.partial(_pool_kernel, n_prefix=n_prefix)
    return pl.pallas_call(
        kern,
        grid=(b, t // tb),
        in_specs=[pl.BlockSpec((1, tb, D_POOL), lambda bi, i: (bi, i, col)),
                  pl.BlockSpec((1, POOL_HALO, D_POOL),
                               lambda bi, i: (bi, jnp.maximum(i * (tb // POOL_HALO) - 1, 0), col)),
                  pl.BlockSpec((1, POOL_HALO, D_POOL), lambda bi, i: (bi, 0, 0)),
                  pl.BlockSpec((D_POOL, D_POOL), lambda bi, i: (0, 0)),
                  pl.BlockSpec((1, D_POOL), lambda bi, i: (0, 0))],
        out_specs=pl.BlockSpec((1, tb, D_POOL), lambda bi, i: (bi, i, 0)),
        out_shape=jax.ShapeDtypeStruct((b, t, D_POOL), BF16),
        compiler_params=_cparams("parallel", "arbitrary"),
        name="pool_mix",
    )(proj, proj, prefix, w_blockdiag, scale)


def _rwkv_kernel(p_ref, prev_ref, st0_ref, mu_ref, w0_ref, a0_ref, w2_ref, a2_ref, g2_ref,
                 kk_ref, ka_ref, rk_ref, lng_ref, lnb_ref, yc_ref, st_ref, carry_ref, y_ref,
                 *, chunk):
    j = pl.program_id(1)

    @pl.when(j == 0)
    def _():
        carry_ref[...] = prev_ref[0]
        st_ref[0] = st0_ref[0]

    p = p_ref[0]
    tb = p.shape[0]
    n_chunk = tb // chunk
    hl = R_HEADS * chunk

    row = lax.broadcasted_iota(jnp.int32, (tb, 1), 0)
    p_prev = jnp.where(row == 0, carry_ref[...], pltpu.roll(p, 1, axis=0))
    carry_ref[...] = p[tb - 1:tb]
    xs = p + mu_ref[...] * (p_prev - p)
    r = xs[:, 0:D_R]
    k = xs[:, D_R:2 * D_R]
    v = xs[:, 2 * D_R:3 * D_R]
    wa = xs[:, 3 * D_R:3 * D_R + R_WA]
    gd = xs[:, 3 * D_R + R_WA:]

    z = -(w0_ref[...] + _mm(jnp.tanh(wa), w2_ref[...]))
    softplus = jnp.maximum(z, 0.0) + jnp.log(1.0 + jnp.exp(-jnp.abs(z)))
    lw = -jnp.exp(-softplus - 0.5)
    a = _sigmoid(a0_ref[...] + _mm(wa, a2_ref[...]))
    g = _mm(_sigmoid(gd), g2_ref[...])

    lane_r = lax.broadcasted_iota(jnp.int32, (D_R, D_R), 0) // HEAD_DIM
    lane_c = lax.broadcasted_iota(jnp.int32, (D_R, D_R), 1) // HEAD_DIM
    head_ones = (lane_r == lane_c).astype(F32)
    seg_sum = lambda t: _mm_exact_rhs(t, head_ones)

    kk = k * kk_ref[...]
    kk = kk / jnp.maximum(jnp.sqrt(seg_sum(kk * kk)), 1e-12)
    k2 = k * (1.0 + (a - 1.0) * ka_ref[...])
    bb = kk * a

    ti = lax.broadcasted_iota(jnp.int32, (tb, tb), 0)
    tj = lax.broadcasted_iota(jnp.int32, (tb, tb), 1)
    cum = _mm_exact_lhs(((ti // chunk == tj // chunk) & (tj <= ti)).astype(F32), lw)

    wi = lax.broadcasted_iota(jnp.int32, (chunk, hl), 0)
    wj = lax.broadcasted_iota(jnp.int32, (chunk, hl), 1) % chunk
    strict = wj < wi
    incl = wj <= wi
    eye_w = (wj == wi).astype(F32)
    diag_k = (lax.broadcasted_iota(jnp.int32, (HEAD_DIM, D_R), 0)
              == lax.broadcasted_iota(jnp.int32, (HEAD_DIM, D_R), 1) % HEAD_DIM)
    lane_k = lax.broadcasted_iota(jnp.int32, (1, D_R), 1) // HEAD_DIM
    lane_t = lax.broadcasted_iota(jnp.int32, (1, hl), 1) // chunk
    mask_k = [(lane_k == h).astype(F32) for h in range(R_HEADS)]
    mask_k_bf = [m.astype(BF16) for m in mask_k]
    mask_t_bf = [(lane_t == h).astype(BF16) for h in range(R_HEADS)]

    def blockdiag(t, masks):
        t16 = t.astype(BF16)
        return jnp.concatenate([t16 * m for m in masks], axis=0)

    def wide_transpose(t):
        tt = jnp.concatenate([t * m for m in mask_k], axis=0).T
        out = tt[0:HEAD_DIM]
        for h in range(1, R_HEADS):
            out = out + tt[h * HEAD_DIM:(h + 1) * HEAD_DIM]
        return out.astype(BF16)

    chunks = []
    for c in range(n_chunk):
        sl = slice(c * chunk, (c + 1) * chunk)
        cum_c = cum[sl]
        cum_last = cum_c[chunk - 1:chunk]
        g_in = jnp.exp(cum_c)
        g_prev = jnp.exp(cum_c - lw[sl])
        g_inv = jnp.exp(-cum_c)
        g_out = jnp.exp(cum_last - cum_c)
        a_n = (kk[sl] * g_prev).astype(BF16)
        r_n = r[sl] * g_in
        ch = dict(sl=sl, r_n=r_n, g_last=jnp.exp(cum_last),
                  a_s=blockdiag(a_n, mask_k_bf),
                  v_s=blockdiag(v[sl], mask_k_bf),
                  bo_w=wide_transpose(bb[sl] * g_out),
                  ko_w=wide_transpose(k2[sl] * g_out))
        ar = jnp.concatenate([a_n, r_n.astype(BF16)], axis=0)
        m_b = _mm_nt(ar, blockdiag(bb[sl] * g_inv, mask_k_bf))
        m_k = _mm_nt(ar, blockdiag(k2[sl] * g_inv, mask_k_bf))
        m_ab = jnp.where(strict, m_b[:chunk], 0.0)
        ch.update(m_rb=jnp.where(incl, m_b[chunk:], 0.0).astype(BF16),
                  m_ak=jnp.where(strict, m_k[:chunk], 0.0).astype(BF16),
                  m_rk=jnp.where(incl, m_k[chunk:], 0.0).astype(BF16),
                  t_inv=eye_w - m_ab, pw=m_ab.astype(BF16))
        chunks.append(ch)

    n = 1
    while 2 * n < chunk:
        for ch in chunks:
            ch["pw"] = _mm(ch["pw"], blockdiag(ch["pw"], mask_t_bf)).astype(BF16)
        for ch in chunks:
            ch["t_inv"] = ch["t_inv"] + _mm(ch["t_inv"], blockdiag(ch["pw"], mask_t_bf))
        n *= 2

    for ch in chunks:
        t_inv = ch["t_inv"].astype(BF16)
        ch["a_bar"] = blockdiag(_mm(t_inv, ch["a_s"]), mask_k_bf)
        ch["u0"] = blockdiag(-_mm(t_inv, blockdiag(_mm(ch["m_ak"], ch["v_s"]), mask_k_bf)), mask_k_bf)
    for ch in chunks:
        ch["r_bar"] = ch["r_n"] - _mm(ch["m_rb"], ch["a_bar"])
        ch["y0"] = _mm(ch["m_rk"], ch["v_s"]) + _mm(ch["m_rb"], ch["u0"])
        ch["g_w"] = jnp.where(diag_k, ch["g_last"], 0.0) - _mm(ch["bo_w"], ch["a_bar"])
        ch["h_w"] = _mm(ch["bo_w"], ch["u0"]) + _mm(ch["ko_w"], ch["v_s"])

    st_w = st_ref[0]
    for ch in chunks:
        st = blockdiag(st_w, mask_k_bf)
        y_ref[ch["sl"], :] = ch["y0"] + _mm(ch["r_bar"], st)
        st_w = _mm(ch["g_w"], st) + ch["h_w"]
    st_ref[0] = st_w

    y = y_ref[...]
    mean = seg_sum(y) * (1.0 / HEAD_DIM)
    d = y - mean
    var = seg_sum(d * d) * (1.0 / HEAD_DIM)
    yn = d * lax.rsqrt(var + GN_EPS) * lng_ref[...] + lnb_ref[...]
    bonus = seg_sum(r * k2 * rk_ref[...]) * v
    yc_ref[0] = ((yn + bonus) * g).astype(BF16)


def _rwkv_mix(proj, prev, st0, lp, chunk, tb):
    b, t, _ = proj.shape
    col = (D_A + 2 * D_KV + D_POOL) // D_R_IN
    row = lambda n: pl.BlockSpec((1, n), lambda bi, i: (0, 0))
    full = lambda s: pl.BlockSpec(s, lambda bi, i: (0,) * len(s))
    kern = functools.partial(_rwkv_kernel, chunk=chunk)
    return pl.pallas_call(
        kern,
        grid=(b, t // tb),
        in_specs=[pl.BlockSpec((1, tb, D_R_IN), lambda bi, i: (bi, i, col)),
                  pl.BlockSpec((1, 1, D_R_IN), lambda bi, i: (bi, 0, 0)),
                  pl.BlockSpec((1, HEAD_DIM, D_R), lambda bi, i: (bi, 0, 0)),
                  row(D_R_IN), row(D_R), row(D_R),
                  full((R_WA, D_R)), full((R_WA, D_R)), full((R_G, D_R)),
                  row(D_R), row(D_R), row(D_R), row(D_R), row(D_R)],
        out_specs=[pl.BlockSpec((1, tb, D_R), lambda bi, i: (bi, i, 0)),
                   pl.BlockSpec((1, HEAD_DIM, D_R), lambda bi, i: (bi, 0, 0))],
        out_shape=[jax.ShapeDtypeStruct((b, t, D_R), BF16),
                   jax.ShapeDtypeStruct((b, HEAD_DIM, D_R), F32)],
        scratch_shapes=[pltpu.VMEM((1, D_R_IN), F32), pltpu.VMEM((tb, D_R), F32)],
        compiler_params=_cparams("parallel", "arbitrary"),
        name="rwkv7_mix",
    )(proj, prev, st0, lp["mu"], lp["w0"], lp["a0"], lp["w2"], lp["a2"], lp["g2"],
      lp["k_k"], lp["k_a"], lp["r_k"], lp["lnx_g"], lp["lnx_b"])


def _state_to_wide(s):
    b = s.shape[0]
    return jnp.transpose(s, (0, 3, 1, 2)).reshape(b, HEAD_DIM, D_R)


def _wide_to_state(st):
    b = st.shape[0]
    return jnp.transpose(st.reshape(b, HEAD_DIM, R_HEADS, HEAD_DIM), (0, 2, 3, 1))


def _mix_out_kernel(x_ref, ya_ref, yb_ref, yc_ref, w_ref, g_ref, *rest, route):
    if route:
        rw_ref, xm_ref, h_ref, gate_ref = rest
    else:
        xm_ref, h_ref = rest
    dot = lambda u, lo, hi: jnp.dot(u[...], w_ref[lo:hi, :], preferred_element_type=F32)
    xm = (x_ref[...] + dot(ya_ref, 0, D_A) + dot(yb_ref, D_A, D_A + D_POOL)
          + dot(yc_ref, D_A + D_POOL, D_A + D_POOL + D_R))
    xm_ref[...] = xm
    h = xm * lax.rsqrt(jnp.mean(xm * xm, axis=-1, keepdims=True) + NORM_EPS) * g_ref[...]
    h_ref[...] = h.astype(BF16)
    if route:
        logits = _mm_hi(h, rw_ref[...])
        n_e = logits.shape[-1]
        lane = lax.broadcasted_iota(jnp.int32, logits.shape, 1).astype(F32)
        m1 = jnp.max(logits, axis=-1, keepdims=True)
        i1 = jnp.min(jnp.where(logits == m1, lane, float(n_e)), axis=-1, keepdims=True)
        rest_l = jnp.where(lane == i1, -jnp.inf, logits)
        m2 = jnp.max(rest_l, axis=-1, keepdims=True)
        i2 = jnp.min(jnp.where(rest_l == m2, lane, float(n_e)), axis=-1, keepdims=True)
        e2 = jnp.exp(m2 - m1)
        gate_ref[...] = (jnp.where(lane == i1, 1.0 / (1.0 + e2), 0.0)
                         + jnp.where(lane == i2, e2 / (1.0 + e2), 0.0))


def _mix_out(x, ya, yb, yc, w_out, g, router_w, tm):
    n, d = x.shape
    route = router_w is not None
    tile = lambda w: pl.BlockSpec((tm, w), lambda i: (i, 0))
    in_specs = [tile(d), tile(D_A), tile(D_POOL), tile(D_R),
                pl.BlockSpec(w_out.shape, lambda i: (0, 0)),
                pl.BlockSpec((1, d), lambda i: (0, 0))]
    out_specs = [tile(d), tile(d)]
    out_shape = [jax.ShapeDtypeStruct((n, d), F32), jax.ShapeDtypeStruct((n, d), BF16)]
    args = [x, ya, yb, yc, w_out, g]
    if route:
        n_e = router_w.shape[1]
        in_specs.append(pl.BlockSpec(router_w.shape, lambda i: (0, 0)))
        out_specs.append(tile(n_e))
        out_shape.append(jax.ShapeDtypeStruct((n, n_e), F32))
        args.append(router_w)
    return pl.pallas_call(
        functools.partial(_mix_out_kernel, route=route),
        grid=(n // tm,),
        in_specs=in_specs, out_specs=out_specs, out_shape=out_shape,
        compiler_params=_cparams("parallel"),
        name="mix_out_proj",
    )(*args)


def _ffn_kernel(h_ref, xm_ref, wg_ref, wu_ref, wd_ref, o_ref, acc_ref):
    j = pl.program_id(1)

    @pl.when(j == 0)
    def _():
        acc_ref[...] = xm_ref[...]

    h = h_ref[...]
    act = (_silu(jnp.dot(h, wg_ref[...], preferred_element_type=F32))
           * jnp.dot(h, wu_ref[...], preferred_element_type=F32))
    acc_ref[...] += jnp.dot(act.astype(BF16), wd_ref[...], preferred_element_type=F32)

    @pl.when(j == pl.num_programs(1) - 1)
    def _():
        o_ref[...] = acc_ref[...]


def _ffn(h, xm, wg, wu, wd, tm, tf):
    n, d = xm.shape
    f = wg.shape[1]
    return pl.pallas_call(
        _ffn_kernel,
        grid=(n // tm, f // tf),
        in_specs=[pl.BlockSpec((tm, d), lambda i, j: (i, 0)),
                  pl.BlockSpec((tm, d), lambda i, j: (i, 0)),
                  pl.BlockSpec((d, tf), lambda i, j: (0, j)),
                  pl.BlockSpec((d, tf), lambda i, j: (0, j)),
                  pl.BlockSpec((tf, d), lambda i, j: (j, 0))],
        out_specs=pl.BlockSpec((tm, d), lambda i, j: (i, 0)),
        out_shape=jax.ShapeDtypeStruct((n, d), F32),
        scratch_shapes=[pltpu.VMEM((tm, d), F32)],
        compiler_params=_cparams("parallel", "arbitrary"),
        name="swiglu_ffn",
    )(h, xm, wg, wu, wd)


def _moe_kernel(h_ref, xm_ref, gate_ref, wg_ref, wu_ref, wd_ref, o_ref, acc_ref):
    e = pl.program_id(1)
    j = pl.program_id(2)

    @pl.when((e == 0) & (j == 0))
    def _():
        acc_ref[...] = xm_ref[...]

    gates = gate_ref[...]
    lane = lax.broadcasted_iota(jnp.int32, gates.shape, 1)
    gate = jnp.sum(jnp.where(lane == e, gates, 0.0), axis=-1, keepdims=True)
    h = h_ref[...]
    act = (_silu(jnp.dot(h, wg_ref[0], preferred_element_type=F32))
           * jnp.dot(h, wu_ref[0], preferred_element_type=F32))
    acc_ref[...] += gate * jnp.dot(act.astype(BF16), wd_ref[0], preferred_element_type=F32)

    @pl.when((e == pl.num_programs(1) - 1) & (j == pl.num_programs(2) - 1))
    def _():
        o_ref[...] = acc_ref[...]


def _moe(h, xm, gates, wg, wu, wd, tm, tf):
    n, d = xm.shape
    n_e, _, f = wg.shape
    return pl.pallas_call(
        _moe_kernel,
        grid=(n // tm, n_e, f // tf),
        in_specs=[pl.BlockSpec((tm, d), lambda i, e, j: (i, 0)),
                  pl.BlockSpec((tm, d), lambda i, e, j: (i, 0)),
                  pl.BlockSpec((tm, n_e), lambda i, e, j: (i, 0)),
                  pl.BlockSpec((1, d, tf), lambda i, e, j: (e, 0, j)),
                  pl.BlockSpec((1, d, tf), lambda i, e, j: (e, 0, j)),
                  pl.BlockSpec((1, tf, d), lambda i, e, j: (e, j, 0))],
        out_specs=pl.BlockSpec((tm, d), lambda i, e, j: (i, 0)),
        out_shape=jax.ShapeDtypeStruct((n, d), F32),
        scratch_shapes=[pltpu.VMEM((tm, d), F32)],
        compiler_params=_cparams("parallel", "arbitrary", "arbitrary"),
        name="moe_ffn",
    )(h, xm, gates, wg, wu, wd)


MOE_ROW_ALIGN = 32
MOE_GATE_LANES = 128
MOE_TILE = 1024
MOE_GROUP_ROWS = 1024
MOE_COPY_WINDOW = 64
_UNSELECTED = 1e9


def _tile_routing(gates_sel, rank, axis):
    n_e = rank.shape[axis]
    cnt = jnp.max(rank, axis=1 - axis, keepdims=True)
    padded = jnp.floor((cnt + (MOE_ROW_ALIGN - 1)) * (1.0 / MOE_ROW_ALIGN)) * MOE_ROW_ALIGN
    offs, run = [], jnp.zeros((1, 1), F32)
    for e in range(n_e):
        offs.append(run)
        run = run + (padded[e:e + 1] if axis == 0 else padded[:, e:e + 1])
    off = jnp.concatenate(offs, axis=axis)
    dest = jnp.where(gates_sel, off + rank - 1.0, -1.0)
    d_hi = jnp.max(dest, axis=axis, keepdims=True)
    d_lo = jnp.min(jnp.where(gates_sel, dest, _UNSELECTED), axis=axis, keepdims=True)
    d_lo = jnp.where(d_lo == d_hi, -2.0, d_lo)
    return cnt, dest, d_hi, d_lo


def _moe_dispatch_kernel(h_ref, gate_ref, hs_ref, cnt_ref, *, chunk_rows):
    h = h_ref[...]
    gates = gate_ref[...]
    tm, n_e = gates.shape
    d = h.shape[1]
    mt = hs_ref.shape[0]
    eye = (lax.broadcasted_iota(jnp.int32, (n_e, n_e), 0)
           == lax.broadcasted_iota(jnp.int32, (n_e, n_e), 1)).astype(BF16)
    to_rows = lambda u: lax.dot_general(eye, u, (((1,), (1,)), ((), ())), preferred_element_type=F32)
    g_hi3, g_mid3, g_lo3 = _split3(gates)
    gate_row = to_rows(g_hi3) + to_rows(g_mid3) + to_rows(g_lo3)
    sel_row = gate_row > 0.0
    ti = lax.broadcasted_iota(jnp.int32, (tm, tm), 0)
    tj = lax.broadcasted_iota(jnp.int32, (tm, tm), 1)
    rank = jnp.dot(sel_row.astype(BF16), (ti <= tj).astype(BF16), preferred_element_type=F32)
    cnt, dest, d_hi, d_lo = _tile_routing(sel_row, rank, 0)
    cnt_ref[0] = jnp.broadcast_to(cnt, cnt_ref.shape[1:])
    g_hi = jnp.sum(jnp.where(dest == d_hi, gate_row, 0.0), axis=0, keepdims=True)
    g_lo = jnp.sum(jnp.where(dest == d_lo, gate_row, 0.0), axis=0, keepdims=True)
    lane = lax.broadcasted_iota(jnp.int32, (chunk_rows, MOE_GATE_LANES), 1)
    for c0 in range(0, mt, chunk_rows):
        rho = (c0 + lax.broadcasted_iota(jnp.int32, (chunk_rows, 1), 0)).astype(F32)
        is_hi = d_hi == rho
        is_lo = d_lo == rho
        onehot = jnp.where(is_hi, 1.0, jnp.where(is_lo, 1.0, 0.0)).astype(BF16)
        rows = jnp.dot(onehot, h, preferred_element_type=F32)
        g = jnp.sum(jnp.where(is_hi, g_hi, jnp.where(is_lo, g_lo, 0.0)), axis=1, keepdims=True)
        p0, p1, p2 = [t.astype(F32) for t in _split3(g)]
        gcols = jnp.where(lane == 0, p0, jnp.where(lane == 1, p1, jnp.where(lane == 2, p2, 0.0)))
        hs_ref[c0:c0 + chunk_rows, 0:d] = rows.astype(BF16)
        hs_ref[c0:c0 + chunk_rows, d:d + MOE_GATE_LANES] = gcols.astype(BF16)


def _moe_dispatch(h, gates, tm, mt):
    n, d = h.shape
    n_e = gates.shape[1]
    n_tiles = n // tm
    return pl.pallas_call(
        functools.partial(_moe_dispatch_kernel, chunk_rows=256),
        grid=(n_tiles,),
        in_specs=[pl.BlockSpec((tm, d), lambda i: (i, 0)),
                  pl.BlockSpec((tm, n_e), lambda i: (i, 0))],
        out_specs=[pl.BlockSpec((mt, d + MOE_GATE_LANES), lambda i: (i, 0)),
                   pl.BlockSpec((1, n_e, 128), lambda i: (i, 0, 0))],
        out_shape=[jax.ShapeDtypeStruct((n_tiles * mt, d + MOE_GATE_LANES), BF16),
                   jax.ShapeDtypeStruct((n_tiles, n_e, 128), F32)],
        compiler_params=_cparams("parallel"),
        name="moe_dispatch",
    )(h, gates)


def _moe_tables(cnt, mt, group_rows):
    i32 = jnp.int32
    n_tiles, n_e = cnt.shape
    bpt = mt // MOE_ROW_ALIGN
    g = group_rows // MOE_ROW_ALIGN
    nblk = jnp.ceil(cnt / MOE_ROW_ALIGN).astype(i32)
    cum = jnp.cumsum(nblk, axis=1)
    off_blk = cum - nblk
    src_base = jnp.arange(n_tiles, dtype=i32)[:, None] * bpt + off_blk
    tot = jnp.sum(nblk, axis=0)
    totp = ((tot + g - 1) // g) * g
    es = jnp.cumsum(totp) - totp
    nblk_t = nblk.T
    seg_start = es[:, None] + jnp.cumsum(nblk_t, axis=1) - nblk_t
    seg_start_f, seg_len_f, seg_src_f = seg_start.reshape(-1), nblk_t.reshape(-1), src_base.T.reshape(-1)
    n_src = n_tiles * bpt
    n_dst = -(-(n_src + n_e * (g - 1)) // g) * g
    dblk = jnp.arange(n_dst, dtype=i32)
    seg = jnp.searchsorted(seg_start_f, dblk, side="right").astype(i32) - 1
    within = dblk - seg_start_f[seg]
    src_of_dst = jnp.where(within < seg_len_f[seg], seg_src_f[seg] + within, 0)
    n_rb = n_dst // g
    eid = jnp.clip(jnp.searchsorted(es, jnp.arange(n_rb, dtype=i32) * g, side="right").astype(i32) - 1,
                   0, n_e - 1)
    n_valid = ((es[-1] + totp[-1]) // g).reshape(1)
    sblk = jnp.arange(n_src, dtype=i32)
    t_of, o_of = sblk // bpt, sblk % bpt
    e_of = jnp.sum(o_of[:, None] >= cum[t_of], axis=1).astype(i32)
    e_c = jnp.minimum(e_of, n_e - 1)
    dst_of_src = jnp.where(e_of < n_e, seg_start[e_c, t_of] + o_of - off_blk[t_of, e_c], 0)
    return src_of_dst.astype(i32), dst_of_src.astype(i32), eid, n_valid.astype(i32), n_dst


def _block_copy_kernel(idx_ref, src_ref, dst_ref, sem, *, n_blocks, rows):
    def copy(k):
        s = pl.multiple_of(idx_ref[k] * rows, rows)
        t = pl.multiple_of(k * rows, rows)
        return pltpu.make_async_copy(src_ref.at[pl.ds(s, rows)], dst_ref.at[pl.ds(t, rows)], sem)

    def issue(k, carry):
        @pl.when(k >= MOE_COPY_WINDOW)
        def _():
            copy(k - MOE_COPY_WINDOW).wait()
        copy(k).start()
        return carry

    lax.fori_loop(0, n_blocks, issue, 0)

    def drain(k, carry):
        copy(k).wait()
        return carry

    lax.fori_loop(max(n_blocks - MOE_COPY_WINDOW, 0), n_blocks, drain, 0)


def _block_copy(idx, src, n_blocks, name):
    rows = MOE_ROW_ALIGN
    return pl.pallas_call(
        functools.partial(_block_copy_kernel, n_blocks=n_blocks, rows=rows),
        grid_spec=pltpu.PrefetchScalarGridSpec(
            num_scalar_prefetch=1, grid=(1,),
            in_specs=[pl.BlockSpec(memory_space=pl.ANY)],
            out_specs=pl.BlockSpec(memory_space=pl.ANY),
            scratch_shapes=[pltpu.SemaphoreType.DMA(())]),
        out_shape=jax.ShapeDtypeStruct((n_blocks * rows, src.shape[1]), src.dtype),
        compiler_params=pltpu.CompilerParams(dimension_semantics=("arbitrary",)),
        name=name,
    )(idx, src)


def _moe_group_kernel(eid_ref, nv_ref, hs_ref, wg_ref, wu_ref, wd_ref, ys_ref, acc_ref):
    i = pl.program_id(0)
    j = pl.program_id(1)
    d = ys_ref.shape[1]

    @pl.when(i < nv_ref[0])
    def _():
        h = hs_ref[:, 0:d]
        act = (_silu(jnp.dot(h, wg_ref[0], preferred_element_type=F32))
               * jnp.dot(h, wu_ref[0], preferred_element_type=F32))
        part = jnp.dot(act.astype(BF16), wd_ref[0], preferred_element_type=F32)

        @pl.when(j == 0)
        def _():
            acc_ref[...] = part

        @pl.when(j > 0)
        def _():
            acc_ref[...] += part

        @pl.when(j == pl.num_programs(1) - 1)
        def _():
            gp = hs_ref[:, d:d + MOE_GATE_LANES].astype(F32)
            gate = gp[:, 0:1] + gp[:, 1:2] + gp[:, 2:3]
            ys_ref[...] = (acc_ref[...] * gate).astype(BF16)

    @pl.when((i >= nv_ref[0]) & (j == 0))
    def _():
        ys_ref[...] = jnp.zeros(ys_ref.shape, BF16)


def _moe_group_ffn(hs, eid, n_valid, wg, wu, wd, rb, tf):
    rows, dw = hs.shape
    d = dw - MOE_GATE_LANES
    f = wg.shape[2]
    nj = f // tf
    live_j = lambda i, j, nv: jnp.where(i < nv[0], j, nj - 1)
    return pl.pallas_call(
        _moe_group_kernel,
        grid_spec=pltpu.PrefetchScalarGridSpec(
            num_scalar_prefetch=2, grid=(rows // rb, nj),
            in_specs=[pl.BlockSpec((rb, dw), lambda i, j, eid, nv: (jnp.minimum(i, nv[0] - 1), 0)),
                      pl.BlockSpec((1, d, tf), lambda i, j, eid, nv: (eid[i], 0, live_j(i, j, nv))),
                      pl.BlockSpec((1, d, tf), lambda i, j, eid, nv: (eid[i], 0, live_j(i, j, nv))),
                      pl.BlockSpec((1, tf, d), lambda i, j, eid, nv: (eid[i], live_j(i, j, nv), 0))],
            out_specs=pl.BlockSpec((rb, d), lambda i, j, eid, nv: (i, 0)),
            scratch_shapes=[pltpu.VMEM((rb, d), F32)]),
        out_shape=jax.ShapeDtypeStruct((rows, d), BF16),
        compiler_params=_cparams("arbitrary", "arbitrary"),
        name="moe_group_ffn",
    )(eid, n_valid, hs, wg, wu, wd)


def _moe_combine_kernel(xm_ref, gate_ref, ys_ref, o_ref):
    gates = gate_ref[...]
    tm, n_e = gates.shape
    mt = ys_ref.shape[0]
    sel = gates > 0.0
    ti = lax.broadcasted_iota(jnp.int32, (tm, tm), 0)
    tj = lax.broadcasted_iota(jnp.int32, (tm, tm), 1)
    rank = jnp.dot((tj <= ti).astype(BF16), sel.astype(BF16), preferred_element_type=F32)
    _, _, d_hi, d_lo = _tile_routing(sel, rank, 1)
    rho = lax.broadcasted_iota(jnp.int32, (1, mt), 1).astype(F32)
    onehot = jnp.where(d_hi == rho, 1.0, jnp.where(d_lo == rho, 1.0, 0.0)).astype(BF16)
    o_ref[...] = xm_ref[...] + jnp.dot(onehot, ys_ref[...], preferred_element_type=F32)


def _moe_combine(xm, gates, ys, tm, mt):
    n, d = xm.shape
    n_e = gates.shape[1]
    return pl.pallas_call(
        _moe_combine_kernel,
        grid=(n // tm,),
        in_specs=[pl.BlockSpec((tm, d), lambda i: (i, 0)),
                  pl.BlockSpec((tm, n_e), lambda i: (i, 0)),
                  pl.BlockSpec((mt, d), lambda i: (i, 0))],
        out_specs=pl.BlockSpec((tm, d), lambda i: (i, 0)),
        out_shape=jax.ShapeDtypeStruct((n, d), F32),
        compiler_params=_cparams("parallel"),
        name="moe_combine",
    )(xm, gates, ys)


def _moe_sparse(h, xm, gates, wg, wu, wd, tf):
    n_e = gates.shape[1]
    tm = MOE_TILE
    mt = TOP_K * tm + n_e * MOE_ROW_ALIGN
    hs_tile, cnt = _moe_dispatch(h, gates, tm, mt)
    src_of_dst, dst_of_src, eid, n_valid, n_dst = _moe_tables(cnt[:, :, 0], mt, MOE_GROUP_ROWS)
    hs_exp = _block_copy(src_of_dst, hs_tile, n_dst, "moe_gather_blocks")
    ys_exp = _moe_group_ffn(hs_exp, eid, n_valid, wg, wu, wd, MOE_GROUP_ROWS, tf)
    ys_tile = _block_copy(dst_of_src, ys_exp, hs_tile.shape[0] // MOE_ROW_ALIGN, "moe_scatter_blocks")
    return _moe_combine(xm, gates, ys_tile, tm, mt)


def _pick(n, candidates):
    for c in candidates:
        if n % c == 0:
            return c
    raise ValueError(f"no tile for {n}")


def _trunk(x, layers, cache_k, cache_v, state_pool, state_shift, state_wkv):
    prompt = cache_k is None
    b, t, d = x.shape
    n = b * t
    tm = _pick(n, (512, 256, 128))
    tm_ffn = _pick(n, (1024, 512, 256, 128))
    cq = CHUNK if prompt else t
    rq = _pick(t, (2 * CHUNK,)) if prompt else t
    n_part = _pick(t // rq, (4, 2, 1))
    pool_tb = _pick(t, (512, 256, 128, 64, 32))
    r_chunk = CHUNK if prompt else t
    r_tb = _pick(t, (512, 256, 128, 64, 32))
    x2 = x.reshape(n, d)
    nk, nv, npool, nshift, nwkv = [], [], [], [], []
    for l, lp in enumerate(layers):
        proj2 = _norm_matmul(x2, lp["norm1_g"], lp["w_in"], tm)
        proj = proj2.reshape(b, t, -1)
        c_pool = D_A + 2 * D_KV
        c_r = c_pool + D_POOL
        v_raw = proj[:, :, D_A + D_KV:c_pool]
        if prompt:
            cache_kv = None
            prefix = jnp.zeros((b, POOL_HALO, D_POOL), F32)
            prev = jnp.zeros((b, 1, D_R_IN), F32)
            st0 = jnp.zeros((b, HEAD_DIM, D_R), F32)
        else:
            cache_kv = jnp.concatenate([cache_k[l].reshape(b, WINDOW, D_KV),
                                        cache_v[l].reshape(b, WINDOW, D_KV)], axis=-1)
            prefix = jnp.pad(state_pool[l], ((0, 0), (POOL_HALO - POOL_CTX, 0), (0, 0)))
            prev = state_shift[l]
            st0 = _state_to_wide(state_wkv[l])
        ya, k_norm = _attention(proj, cache_kv, lp["q_gain"], lp["k_gain"], lp["sink"], cq, rq, n_part)
        yb = _pool_mix(proj, prefix, lp["pool_w"], lp["pool_scale"], 0 if prompt else POOL_CTX, pool_tb)
        yc, st_fin = _rwkv_mix(proj, prev, st0, lp, r_chunk, r_tb)
        keep = WINDOW if prompt else t
        nk.append(k_norm[:, t - keep:].reshape(b, keep, A_KV_HEADS, HEAD_DIM))
        nv.append(v_raw[:, t - keep:].reshape(b, keep, A_KV_HEADS, HEAD_DIM))
        npool.append(proj[:, t - POOL_CTX:, c_pool:c_r])
        nshift.append(proj[:, t - 1:, c_r:])
        nwkv.append(_wide_to_state(st_fin))
        outs = _mix_out(x2, ya.reshape(n, -1), yb.reshape(n, -1), yc.reshape(n, -1),
                        lp["w_out"], lp["norm2_g"], lp.get("router_w"), tm)
        if "router_w" in lp:
            xm, h2, gates = outs
            if n % MOE_TILE == 0 and n >= 2 * MOE_TILE:
                x2 = _moe_sparse(h2, xm, gates, lp["wg"], lp["wu"], lp["wd"], 512)
            else:
                x2 = _moe(h2, xm, gates, lp["wg"], lp["wu"], lp["wd"], tm_ffn, 512)
        else:
            xm, h2 = outs
            x2 = _ffn(h2, xm, lp["wg"], lp["wu"], lp["wd"], tm_ffn, 512)
    return (x2.reshape(b, t, d), jnp.stack(nk), jnp.stack(nv), jnp.stack(npool),
            jnp.stack(nshift), jnp.stack(nwkv))


def kernel(x_prompt, x_sample, cache_k, cache_v, state_pool, state_shift, state_wkv, norm1_g, w_in, q_gain, k_gain, attn_sink, pool_w, pool_scale, shift_mu, decay_w0, decay_w2, iclr_a0, iclr_a2, gate_g2, k_k, k_a, r_k, lnx_g, lnx_b, w_out, norm2_g, ffn_wg, ffn_wu, ffn_wd, router_w, moe_wg, moe_wu, moe_wd):
    depth = w_in.shape[0]
    r_w = decay_w2.shape[1]
    layers = []
    for l in range(depth):
        lp = dict(
            norm1_g=norm1_g[l][None], w_in=w_in[l].astype(BF16),
            q_gain=q_gain[l][None], k_gain=k_gain[l][None], sink=attn_sink[l],
            pool_w=jax.scipy.linalg.block_diag(*[pool_w[l, gi] for gi in range(len(POOL_WINDOWS))]).astype(BF16),
            pool_scale=pool_scale[l][None],
            mu=shift_mu[l][None], w0=decay_w0[l][None], a0=iclr_a0[l][None],
            w2=jnp.pad(decay_w2[l], ((0, R_WA - r_w), (0, 0))).astype(BF16),
            a2=jnp.pad(iclr_a2[l], ((r_w, 0), (0, 0))).astype(BF16),
            g2=gate_g2[l].astype(BF16),
            k_k=k_k[l][None], k_a=k_a[l][None], r_k=r_k[l].reshape(1, D_R),
            lnx_g=lnx_g[l][None], lnx_b=lnx_b[l][None],
            w_out=w_out[l].astype(BF16), norm2_g=norm2_g[l][None])
        if l % 2 == 0:
            lp.update(wg=ffn_wg[l // 2].astype(BF16), wu=ffn_wu[l // 2].astype(BF16),
                      wd=ffn_wd[l // 2].astype(BF16))
        else:
            lp.update(router_w=router_w[l // 2], wg=moe_wg[l // 2].astype(BF16),
                      wu=moe_wu[l // 2].astype(BF16), wd=moe_wd[l // 2].astype(BF16))
        layers.append(lp)
    y_p, pk, pv, ppool, pshift, pwkv = _trunk(x_prompt, layers, None, None, None, None, None)
    y_s, sk, sv, spool, sshift, swkv = _trunk(x_sample, layers, cache_k, cache_v, state_pool,
                                              state_shift, state_wkv)
    return (y_p, y_s, pk, pv, ppool, pshift, pwkv, sk, sv, spool, sshift, swkv)
```

```python
import functools

import jax
import jax.numpy as jnp
from jax import lax
from jax.experimental import pallas as pl
from jax.experimental.pallas import tpu as pltpu

F32 = jnp.float32
BF16 = jnp.bfloat16

HEAD_DIM = 64
A_HEADS = 8
A_KV_HEADS = 2
A_GROUP = A_HEADS // A_KV_HEADS
D_A = A_HEADS * HEAD_DIM
D_KV = A_KV_HEADS * HEAD_DIM
WINDOW = 128
CHUNK = 64
POOL_WINDOWS = (2, 4, 8, 16)
POOL_CTX = 15
POOL_HALO = 16
D_POOL = 256
POOL_GW = D_POOL // len(POOL_WINDOWS)
D_R = 256
R_HEADS = D_R // HEAD_DIM
R_WA = 128
R_G = 128
D_R_IN = 3 * D_R + R_WA + R_G
TOP_K = 2
NORM_EPS = 1e-6
GN_EPS = 64e-5
NEG_INF = -1e30
VMEM_LIMIT_BYTES = 56 * 1024 * 1024


def _cparams(*sem):
    return pltpu.CompilerParams(dimension_semantics=sem, vmem_limit_bytes=VMEM_LIMIT_BYTES)


def _mm(a, b):
    return jnp.dot(a.astype(BF16), b.astype(BF16), preferred_element_type=F32)


def _mm_nt(a, b):
    return lax.dot_general(a.astype(BF16), b.astype(BF16), (((1,), (1,)), ((), ())),
                           preferred_element_type=F32)


def _split2(x):
    hi = x.astype(BF16)
    lo = (x - hi.astype(F32)).astype(BF16)
    return hi, lo


def _split3(x):
    hi = x.astype(BF16)
    r1 = x - hi.astype(F32)
    mid = r1.astype(BF16)
    lo = (r1 - mid.astype(F32)).astype(BF16)
    return hi, mid, lo


def _mm_exact_rhs(x, b):
    bb = b.astype(BF16)
    hi, mid, lo = _split3(x)
    dot = lambda u: jnp.dot(u, bb, preferred_element_type=F32)
    return dot(hi) + dot(mid) + dot(lo)


def _mm_exact_lhs(a, x):
    ab = a.astype(BF16)
    hi, mid, lo = _split3(x)
    dot = lambda u: jnp.dot(ab, u, preferred_element_type=F32)
    return dot(hi) + dot(mid) + dot(lo)


def _mm_hi(a, b):
    ah, al = _split2(a)
    bh, bl = _split2(b)
    dot = lambda u, v: jnp.dot(u, v, preferred_element_type=F32)
    return dot(ah, bh) + (dot(ah, bl) + dot(al, bh))


def _sigmoid(x):
    return 1.0 / (1.0 + jnp.exp(-x))


def _silu(x):
    return x * _sigmoid(x)


def _norm_matmul_kernel(x_ref, g_ref, w_ref, o_ref):
    x = x_ref[...]
    h = x * lax.rsqrt(jnp.mean(x * x, axis=-1, keepdims=True) + NORM_EPS) * g_ref[...]
    o_ref[...] = jnp.dot(h.astype(BF16), w_ref[...], preferred_element_type=F32)


def _norm_matmul(x, g, w, tm):
    n, d = x.shape
    dout = w.shape[1]
    return pl.pallas_call(
        _norm_matmul_kernel,
        grid=(n // tm,),
        in_specs=[pl.BlockSpec((tm, d), lambda i: (i, 0)),
                  pl.BlockSpec((1, d), lambda i: (0, 0)),
                  pl.BlockSpec((d, dout), lambda i: (0, 0))],
        out_specs=pl.BlockSpec((tm, dout), lambda i: (i, 0)),
        out_shape=jax.ShapeDtypeStruct((n, dout), F32),
        compiler_params=_cparams("parallel"),
        name="norm_in_proj",
    )(x, g, w)


def _block_ones(n):
    r = lax.broadcasted_iota(jnp.int32, (n, n), 0) // HEAD_DIM
    c = lax.broadcasted_iota(jnp.int32, (n, n), 1) // HEAD_DIM
    return (r == c).astype(BF16)


def _head_rms_scale(z, ones):
    hi, lo = _split2(z * z)
    ss = jnp.dot(hi, ones, preferred_element_type=F32) + jnp.dot(lo, ones, preferred_element_type=F32)
    return lax.rsqrt(ss * (1.0 / HEAD_DIM) + NORM_EPS)


def _attn_kernel(sink_ref, q_ref, kv_ref, prev_ref, bias_ref, qg_ref, kg_ref, ya_ref, kn_ref,
                 *, rq, n_part, prev_is_raw):
    i = pl.program_id(1)
    kp = WINDOW + rq
    gw = A_GROUP * HEAD_DIM
    q = q_ref[0]
    kv = kv_ref[0]
    pv = prev_ref[0]
    ones_q = _block_ones(D_A)
    ones_k = _block_ones(D_KV)
    qn = (q * _head_rms_scale(q, ones_q) * qg_ref[...]).astype(BF16)
    k_cur = kv[:, 0:D_KV]
    k_cur = k_cur * _head_rms_scale(k_cur, ones_k) * kg_ref[...]
    kn_ref[0] = k_cur
    k_prev = pv[:, 0:D_KV]
    if prev_is_raw:
        k_prev = k_prev * _head_rms_scale(k_prev, ones_k) * kg_ref[...]
    k_all = jnp.concatenate([k_prev, k_cur], axis=0).astype(BF16)
    v_all = jnp.concatenate([pv[:, D_KV:], kv[:, D_KV:]], axis=0).astype(BF16)
    nq = A_GROUP * rq
    lane_in = lax.broadcasted_iota(jnp.int32, (D_KV, gw), 0)
    lane_out = lax.broadcasted_iota(jnp.int32, (D_KV, gw), 1)
    lane_out_t = lax.broadcasted_iota(jnp.int32, (gw, D_KV), 0)
    lane_in_t = lax.broadcasted_iota(jnp.int32, (gw, D_KV), 1)
    slot_lane = lax.broadcasted_iota(jnp.int32, (1, gw), 1) // HEAD_DIM
    slot_mask_bf = [(slot_lane == hh).astype(BF16) for hh in range(A_GROUP)]
    slot_row = lax.broadcasted_iota(jnp.int32, (gw, rq), 0) // HEAD_DIM
    key_row = lax.broadcasted_iota(jnp.int32, (kp, nq), 0)
    col_head = lax.broadcasted_iota(jnp.int32, (1, nq), 1) // rq
    for g in range(A_KV_HEADS):
        select = (lane_in // HEAD_DIM == g) & (lane_in % HEAD_DIM == lane_out % HEAD_DIM)
        k_wide = jnp.dot(k_all, select.astype(BF16), preferred_element_type=F32).astype(BF16)
        select_t = (lane_in_t // HEAD_DIM == g) & (lane_in_t % HEAD_DIM == lane_out_t % HEAD_DIM)
        v_wide_t = lax.dot_general(select_t.astype(BF16), v_all, (((1,), (1,)), ((), ())),
                                   preferred_element_type=F32).astype(BF16)
        q_g = qn[:, g * gw:(g + 1) * gw]
        sink_row = jnp.zeros((1, nq), F32)
        for hh in range(A_GROUP):
            sink_row = jnp.where(col_head == hh, sink_ref[g * A_GROUP + hh], sink_row)
        bias_t = bias_ref[g]
        for p in range(n_part):
            rows = slice(p * rq, (p + 1) * rq)
            qs = jnp.concatenate([q_g[rows] * m for m in slot_mask_bf], axis=0)
            kb = k_wide[p * rq:p * rq + kp]
            s = lax.dot_general(kb, qs, (((1,), (1,)), ((), ())), preferred_element_type=F32) + bias_t
            if prev_is_raw and p * rq < WINDOW:
                n_pad = WINDOW - (i * n_part + p) * rq
                s = s + jnp.where(key_row < n_pad, NEG_INF, 0.0)
            m = jnp.maximum(jnp.max(s, axis=0, keepdims=True), sink_row)
            e = jnp.exp(s - m)
            den = jnp.sum(e, axis=0, keepdims=True) + jnp.exp(sink_row - m)
            prob = (e * (1.0 / den)).astype(BF16)
            o_t = jnp.dot(v_wide_t[:, p * rq:p * rq + kp], prob, preferred_element_type=F32)
            out_t = o_t[:, (A_GROUP - 1) * rq:]
            for hh in range(A_GROUP - 2, -1, -1):
                out_t = jnp.where(slot_row == hh, o_t[:, hh * rq:(hh + 1) * rq], out_t)
            ya_ref[0, rows, g * gw:(g + 1) * gw] = out_t.T.astype(BF16)


def _attn_bias(rq, cq):
    kp = WINDOW + rq
    i = jnp.arange(rq)[:, None]
    j = jnp.arange(kp)[None, :]
    jb = j - cq * (i // cq)
    valid = (jb >= 0) & (jb < WINDOW + cq)
    dist = jnp.abs(WINDOW + (i % cq) - jb).astype(F32)
    slopes = jnp.exp2(-8.0 * jnp.arange(1, A_HEADS + 1, dtype=F32) / A_HEADS)
    bias = jnp.where(valid[None], -slopes[:, None, None] * dist[None], NEG_INF)
    return jnp.swapaxes(bias.reshape(A_KV_HEADS, A_GROUP * rq, kp), 1, 2)


def _attention(proj, cache_kv, q_gain, k_gain, sink, cq, rq, n_part):
    b, t, _ = proj.shape
    tq = rq * n_part
    prompt = cache_kv is None
    kv_col = D_A // (2 * D_KV)
    if prompt:
        prev_arr = proj
        prev_spec = pl.BlockSpec((1, WINDOW, 2 * D_KV),
                                 lambda bi, i: (bi, jnp.maximum(i * (tq // WINDOW) - 1, 0), kv_col))
    else:
        prev_arr = cache_kv
        prev_spec = pl.BlockSpec((1, WINDOW, 2 * D_KV), lambda bi, i: (bi, 0, 0))
    kp = WINDOW + rq
    q_gain_t = jnp.tile(q_gain * (HEAD_DIM ** -0.5), (1, A_HEADS))
    k_gain_t = jnp.tile(k_gain, (1, A_KV_HEADS))
    kern = functools.partial(_attn_kernel, rq=rq, n_part=n_part, prev_is_raw=prompt)
    return pl.pallas_call(
        kern,
        grid=(b, t // tq),
        in_specs=[pl.BlockSpec(memory_space=pltpu.SMEM),
                  pl.BlockSpec((1, tq, D_A), lambda bi, i: (bi, i, 0)),
                  pl.BlockSpec((1, tq, 2 * D_KV), lambda bi, i: (bi, i, kv_col)),
                  prev_spec,
                  pl.BlockSpec((A_KV_HEADS, kp, A_GROUP * rq), lambda bi, i: (0, 0, 0)),
                  pl.BlockSpec((1, D_A), lambda bi, i: (0, 0)),
                  pl.BlockSpec((1, D_KV), lambda bi, i: (0, 0))],
        out_specs=[pl.BlockSpec((1, tq, D_A), lambda bi, i: (bi, i, 0)),
                   pl.BlockSpec((1, tq, D_KV), lambda bi, i: (bi, i, 0))],
        out_shape=[jax.ShapeDtypeStruct((b, t, D_A), BF16),
                   jax.ShapeDtypeStruct((b, t, D_KV), F32)],
        compiler_params=_cparams("parallel", "arbitrary"),
        name="swa_attention",
    )(sink, proj, proj, prev_arr, _attn_bias(rq, cq), q_gain_t, k_gain_t)


def _pool_kernel(u_ref, halo_ref, prefix_ref, w_ref, scale_ref, yb_ref, *, n_prefix):
    i = pl.program_id(1)
    u = u_ref[0]
    tb = u.shape[0]
    halo = jnp.where(i == 0, prefix_ref[0], halo_ref[0])
    ext = jnp.concatenate([halo, u], axis=0)
    col = lax.broadcasted_iota(jnp.int32, (1, D_POOL), 1)
    pos = i * tb + lax.broadcasted_iota(jnp.int32, (tb, 1), 0)
    total = None
    count = None
    acc = ext
    span = 1
    for gi, w in enumerate(POOL_WINDOWS):
        while span < w:
            acc = acc + pltpu.roll(acc, span, axis=0)
            span *= 2
        in_group = (col >= gi * POOL_GW) & (col < (gi + 1) * POOL_GW)
        tail = acc[POOL_HALO:]
        total = jnp.where(in_group, tail, 0.0 if total is None else total)
        cnt = jnp.minimum(pos + (1 + n_prefix), w).astype(F32)
        count = jnp.where(in_group, cnt, 1.0 if count is None else count)
    d = total / count - u
    yb_ref[0] = (_mm(d, w_ref[...]) * scale_ref[...]).astype(BF16)


def _pool_mix(proj, prefix, w_blockdiag, scale, n_prefix, tb):
    b, t, _ = proj.shape
    col = (D_A + 2 * D_KV) // D_POOL
    kern = functools.partial(_pool_kernel, n_prefix=n_prefix)
    return pl.pallas_call(
        kern,
        grid=(b, t // tb),
        in_specs=[pl.BlockSpec((1, tb, D_POOL), lambda bi, i: (bi, i, col)),
                  pl.BlockSpec((1, POOL_HALO, D_POOL),
                               lambda bi, i: (bi, jnp.maximum(i * (tb // POOL_HALO) - 1, 0), col)),
                  pl.BlockSpec((1, POOL_HALO, D_POOL), lambda bi, i: (bi, 0, 0)),
                  pl.BlockSpec((D_POOL, D_POOL), lambda bi, i: (0, 0)),
                  pl.BlockSpec((1, D_POOL), lambda bi, i: (0, 0))],
        out_specs=pl.BlockSpec((1, tb, D_POOL), lambda bi, i: (bi, i, 0)),
        out_shape=jax.ShapeDtypeStruct((b, t, D_POOL), BF16),
        compiler_params=_cparams("parallel", "arbitrary"),
        name="pool_mix",
    )(proj, proj, prefix, w_blockdiag, scale)


def _rwkv_kernel(p_ref, prev_ref, st0_ref, mu_ref, w0_ref, a0_ref, w2_ref, a2_ref, g2_ref,
                 kk_ref, ka_ref, rk_ref, lng_ref, lnb_ref, yc_ref, st_ref, carry_ref, y_ref,
                 *, chunk):
    j = pl.program_id(1)

    @pl.when(j == 0)
    def _():
        carry_ref[...] = prev_ref[0]
        st_ref[0] = st0_ref[0]

    p = p_ref[0]
    tb = p.shape[0]
    n_chunk = tb // chunk
    hl = R_HEADS * chunk

    row = lax.broadcasted_iota(jnp.int32, (tb, 1), 0)
    p_prev = jnp.where(row == 0, carry_ref[...], pltpu.roll(p, 1, axis=0))
    carry_ref[...] = p[tb - 1:tb]
    xs = p + mu_ref[...] * (p_prev - p)
    r = xs[:, 0:D_R]
    k = xs[:, D_R:2 * D_R]
    v = xs[:, 2 * D_R:3 * D_R]
    wa = xs[:, 3 * D_R:3 * D_R + R_WA]
    gd = xs[:, 3 * D_R + R_WA:]

    z = -(w0_ref[...] + _mm(jnp.tanh(wa), w2_ref[...]))
    softplus = jnp.maximum(z, 0.0) + jnp.log(1.0 + jnp.exp(-jnp.abs(z)))
    lw = -jnp.exp(-softplus - 0.5)
    a = _sigmoid(a0_ref[...] + _mm(wa, a2_ref[...]))
    g = _mm(_sigmoid(gd), g2_ref[...])

    lane_r = lax.broadcasted_iota(jnp.int32, (D_R, D_R), 0) // HEAD_DIM
    lane_c = lax.broadcasted_iota(jnp.int32, (D_R, D_R), 1) // HEAD_DIM
    head_ones = (lane_r == lane_c).astype(F32)
    seg_sum = lambda t: _mm_exact_rhs(t, head_ones)

    kk = k * kk_ref[...]
    kk = kk / jnp.maximum(jnp.sqrt(seg_sum(kk * kk)), 1e-12)
    k2 = k * (1.0 + (a - 1.0) * ka_ref[...])
    bb = kk * a

    ti = lax.broadcasted_iota(jnp.int32, (tb, tb), 0)
    tj = lax.broadcasted_iota(jnp.int32, (tb, tb), 1)
    cum = _mm_exact_lhs(((ti // chunk == tj // chunk) & (tj <= ti)).astype(F32), lw)

    wi = lax.broadcasted_iota(jnp.int32, (chunk, hl), 0)
    wj = lax.broadcasted_iota(jnp.int32, (chunk, hl), 1) % chunk
    strict = wj < wi
    incl = wj <= wi
    eye_w = (wj == wi).astype(F32)
    diag_k = (lax.broadcasted_iota(jnp.int32, (HEAD_DIM, D_R), 0)
              == lax.broadcasted_iota(jnp.int32, (HEAD_DIM, D_R), 1) % HEAD_DIM)
    lane_k = lax.broadcasted_iota(jnp.int32, (1, D_R), 1) // HEAD_DIM
    lane_t = lax.broadcasted_iota(jnp.int32, (1, hl), 1) // chunk
    mask_k = [(lane_k == h).astype(F32) for h in range(R_HEADS)]
    mask_k_bf = [m.astype(BF16) for m in mask_k]
    mask_t_bf = [(lane_t == h).astype(BF16) for h in range(R_HEADS)]

    def blockdiag(t, masks):
        t16 = t.astype(BF16)
        return jnp.concatenate([t16 * m for m in masks], axis=0)

    def wide_transpose(t):
        tt = jnp.concatenate([t * m for m in mask_k], axis=0).T
        out = tt[0:HEAD_DIM]
        for h in range(1, R_HEADS):
            out = out + tt[h * HEAD_DIM:(h + 1) * HEAD_DIM]
        return out.astype(BF16)

    chunks = []
    for c in range(n_chunk):
        sl = slice(c * chunk, (c + 1) * chunk)
        cum_c = cum[sl]
        cum_last = cum_c[chunk - 1:chunk]
        g_in = jnp.exp(cum_c)
        g_prev = jnp.exp(cum_c - lw[sl])
        g_inv = jnp.exp(-cum_c)
        g_out = jnp.exp(cum_last - cum_c)
        a_n = (kk[sl] * g_prev).astype(BF16)
        r_n = r[sl] * g_in
        ch = dict(sl=sl, r_n=r_n, g_last=jnp.exp(cum_last),
                  a_s=blockdiag(a_n, mask_k_bf),
                  v_s=blockdiag(v[sl], mask_k_bf),
                  bo_w=wide_transpose(bb[sl] * g_out),
                  ko_w=wide_transpose(k2[sl] * g_out))
        ar = jnp.concatenate([a_n, r_n.astype(BF16)], axis=0)
        m_b = _mm_nt(ar, blockdiag(bb[sl] * g_inv, mask_k_bf))
        m_k = _mm_nt(ar, blockdiag(k2[sl] * g_inv, mask_k_bf))
        m_ab = jnp.where(strict, m_b[:chunk], 0.0)
        ch.update(m_rb=jnp.where(incl, m_b[chunk:], 0.0).astype(BF16),
                  m_ak=jnp.where(strict, m_k[:chunk], 0.0).astype(BF16),
                  m_rk=jnp.where(incl, m_k[chunk:], 0.0).astype(BF16),
                  t_inv=eye_w - m_ab, pw=m_ab.astype(BF16))
        chunks.append(ch)

    n = 1
    while 2 * n < chunk:
        for ch in chunks:
            ch["pw"] = _mm(ch["pw"], blockdiag(ch["pw"], mask_t_bf)).astype(BF16)
        for ch in chunks:
            ch["t_inv"] = ch["t_inv"] + _mm(ch["t_inv"], blockdiag(ch["pw"], mask_t_bf))
        n *= 2

    for ch in chunks:
        t_inv = ch["t_inv"].astype(BF16)
        ch["a_bar"] = blockdiag(_mm(t_inv, ch["a_s"]), mask_k_bf)
        ch["u0"] = blockdiag(-_mm(t_inv, blockdiag(_mm(ch["m_ak"], ch["v_s"]), mask_k_bf)), mask_k_bf)
    for ch in chunks:
        ch["r_bar"] = ch["r_n"] - _mm(ch["m_rb"], ch["a_bar"])
        ch["y0"] = _mm(ch["m_rk"], ch["v_s"]) + _mm(ch["m_rb"], ch["u0"])
        ch["g_w"] = jnp.where(diag_k, ch["g_last"], 0.0) - _mm(ch["bo_w"], ch["a_bar"])
        ch["h_w"] = _mm(ch["bo_w"], ch["u0"]) + _mm(ch["ko_w"], ch["v_s"])

    st_w = st_ref[0]
    for ch in chunks:
        st = blockdiag(st_w, mask_k_bf)
        y_ref[ch["sl"], :] = ch["y0"] + _mm(ch["r_bar"], st)
        st_w = _mm(ch["g_w"], st) + ch["h_w"]
    st_ref[0] = st_w

    y = y_ref[...]
    mean = seg_sum(y) * (1.0 / HEAD_DIM)
    d = y - mean
    var = seg_sum(d * d) * (1.0 / HEAD_DIM)
    yn = d * lax.rsqrt(var + GN_EPS) * lng_ref[...] + lnb_ref[...]
    bonus = seg_sum(r * k2 * rk_ref[...]) * v
    yc_ref[0] = ((yn + bonus) * g).astype(BF16)


def _rwkv_mix(proj, prev, st0, lp, chunk, tb):
    b, t, _ = proj.shape
    col = (D_A + 2 * D_KV + D_POOL) // D_R_IN
    row = lambda n: pl.BlockSpec((1, n), lambda bi, i: (0, 0))
    full = lambda s: pl.BlockSpec(s, lambda bi, i: (0,) * len(s))
    kern = functools.partial(_rwkv_kernel, chunk=chunk)
    return pl.pallas_call(
        kern,
        grid=(b, t // tb),
        in_specs=[pl.BlockSpec((1, tb, D_R_IN), lambda bi, i: (bi, i, col)),
                  pl.BlockSpec((1, 1, D_R_IN), lambda bi, i: (bi, 0, 0)),
                  pl.BlockSpec((1, HEAD_DIM, D_R), lambda bi, i: (bi, 0, 0)),
                  row(D_R_IN), row(D_R), row(D_R),
                  full((R_WA, D_R)), full((R_WA, D_R)), full((R_G, D_R)),
                  row(D_R), row(D_R), row(D_R), row(D_R), row(D_R)],
        out_specs=[pl.BlockSpec((1, tb, D_R), lambda bi, i: (bi, i, 0)),
                   pl.BlockSpec((1, HEAD_DIM, D_R), lambda bi, i: (bi, 0, 0))],
        out_shape=[jax.ShapeDtypeStruct((b, t, D_R), BF16),
                   jax.ShapeDtypeStruct((b, HEAD_DIM, D_R), F32)],
        scratch_shapes=[pltpu.VMEM((1, D_R_IN), F32), pltpu.VMEM((tb, D_R), F32)],
        compiler_params=_cparams("parallel", "arbitrary"),
        name="rwkv7_mix",
    )(proj, prev, st0, lp["mu"], lp["w0"], lp["a0"], lp["w2"], lp["a2"], lp["g2"],
      lp["k_k"], lp["k_a"], lp["r_k"], lp["lnx_g"], lp["lnx_b"])


def _state_to_wide(s):
    b = s.shape[0]
    return jnp.transpose(s, (0, 3, 1, 2)).reshape(b, HEAD_DIM, D_R)


def _wide_to_state(st):
    b = st.shape[0]
    return jnp.transpose(st.reshape(b, HEAD_DIM, R_HEADS, HEAD_DIM), (0, 2, 3, 1))


def _mix_out_kernel(x_ref, ya_ref, yb_ref, yc_ref, w_ref, g_ref, *rest, route):
    if route:
        rw_ref, xm_ref, h_ref, gate_ref = rest
    else:
        xm_ref, h_ref = rest
    dot = lambda u, lo, hi: jnp.dot(u[...], w_ref[lo:hi, :], preferred_element_type=F32)
    xm = (x_ref[...] + dot(ya_ref, 0, D_A) + dot(yb_ref, D_A, D_A + D_POOL)
          + dot(yc_ref, D_A + D_POOL, D_A + D_POOL + D_R))
    xm_ref[...] = xm
    h = xm * lax.rsqrt(jnp.mean(xm * xm, axis=-1, keepdims=True) + NORM_EPS) * g_ref[...]
    h_ref[...] = h.astype(BF16)
    if route:
        logits = _mm_hi(h, rw_ref[...])
        n_e = logits.shape[-1]
        lane = lax.broadcasted_iota(jnp.int32, logits.shape, 1).astype(F32)
        m1 = jnp.max(logits, axis=-1, keepdims=True)
        i1 = jnp.min(jnp.where(logits == m1, lane, float(n_e)), axis=-1, keepdims=True)
        rest_l = jnp.where(lane == i1, -jnp.inf, logits)
        m2 = jnp.max(rest_l, axis=-1, keepdims=True)
        i2 = jnp.min(jnp.where(rest_l == m2, lane, float(n_e)), axis=-1, keepdims=True)
        e2 = jnp.exp(m2 - m1)
        gate_ref[...] = (jnp.where(lane == i1, 1.0 / (1.0 + e2), 0.0)
                         + jnp.where(lane == i2, e2 / (1.0 + e2), 0.0))


def _mix_out(x, ya, yb, yc, w_out, g, router_w, tm):
    n, d = x.shape
    route = router_w is not None
    tile = lambda w: pl.BlockSpec((tm, w), lambda i: (i, 0))
    in_specs = [tile(d), tile(D_A), tile(D_POOL), tile(D_R),
                pl.BlockSpec(w_out.shape, lambda i: (0, 0)),
                pl.BlockSpec((1, d), lambda i: (0, 0))]
    out_specs = [tile(d), tile(d)]
    out_shape = [jax.ShapeDtypeStruct((n, d), F32), jax.ShapeDtypeStruct((n, d), BF16)]
    args = [x, ya, yb, yc, w_out, g]
    if route:
        n_e = router_w.shape[1]
        in_specs.append(pl.BlockSpec(router_w.shape, lambda i: (0, 0)))
        out_specs.append(tile(n_e))
        out_shape.append(jax.ShapeDtypeStruct((n, n_e), F32))
        args.append(router_w)
    return pl.pallas_call(
        functools.partial(_mix_out_kernel, route=route),
        grid=(n // tm,),
        in_specs=in_specs, out_specs=out_specs, out_shape=out_shape,
        compiler_params=_cparams("parallel"),
        name="mix_out_proj",
    )(*args)


def _ffn_kernel(h_ref, xm_ref, wg_ref, wu_ref, wd_ref, o_ref, acc_ref):
    j = pl.program_id(1)

    @pl.when(j == 0)
    def _():
        acc_ref[...] = xm_ref[...]

    h = h_ref[...]
    act = (_silu(jnp.dot(h, wg_ref[...], preferred_element_type=F32))
           * jnp.dot(h, wu_ref[...], preferred_element_type=F32))
    acc_ref[...] += jnp.dot(act.astype(BF16), wd_ref[...], preferred_element_type=F32)

    @pl.when(j == pl.num_programs(1) - 1)
    def _():
        o_ref[...] = acc_ref[...]


def _ffn(h, xm, wg, wu, wd, tm, tf):
    n, d = xm.shape
    f = wg.shape[1]
    return pl.pallas_call(
        _ffn_kernel,
        grid=(n // tm, f // tf),
        in_specs=[pl.BlockSpec((tm, d), lambda i, j: (i, 0)),
                  pl.BlockSpec((tm, d), lambda i, j: (i, 0)),
                  pl.BlockSpec((d, tf), lambda i, j: (0, j)),
                  pl.BlockSpec((d, tf), lambda i, j: (0, j)),
                  pl.BlockSpec((tf, d), lambda i, j: (j, 0))],
        out_specs=pl.BlockSpec((tm, d), lambda i, j: (i, 0)),
        out_shape=jax.ShapeDtypeStruct((n, d), F32),
        scratch_shapes=[pltpu.VMEM((tm, d), F32)],
        compiler_params=_cparams("parallel", "arbitrary"),
        name="swiglu_ffn",
    )(h, xm, wg, wu, wd)


def _moe_kernel(h_ref, xm_ref, gate_ref, wg_ref, wu_ref, wd_ref, o_ref, acc_ref):
    e = pl.program_id(1)
    j = pl.program_id(2)

    @pl.when((e == 0) & (j == 0))
    def _():
        acc_ref[...] = xm_ref[...]

    gates = gate_ref[...]
    lane = lax.broadcasted_iota(jnp.int32, gates.shape, 1)
    gate = jnp.sum(jnp.where(lane == e, gates, 0.0), axis=-1, keepdims=True)
    h = h_ref[...]
    act = (_silu(jnp.dot(h, wg_ref[0], preferred_element_type=F32))
           * jnp.dot(h, wu_ref[0], preferred_element_type=F32))
    acc_ref[...] += gate * jnp.dot(act.astype(BF16), wd_ref[0], preferred_element_type=F32)

    @pl.when((e == pl.num_programs(1) - 1) & (j == pl.num_programs(2) - 1))
    def _():
        o_ref[...] = acc_ref[...]


def _moe(h, xm, gates, wg, wu, wd, tm, tf):
    n, d = xm.shape
    n_e, _, f = wg.shape
    return pl.pallas_call(
        _moe_kernel,
        grid=(n // tm, n_e, f // tf),
        in_specs=[pl.BlockSpec((tm, d), lambda i, e, j: (i, 0)),
                  pl.BlockSpec((tm, d), lambda i, e, j: (i, 0)),
                  pl.BlockSpec((tm, n_e), lambda i, e, j: (i, 0)),
                  pl.BlockSpec((1, d, tf), lambda i, e, j: (e, 0, j)),
                  pl.BlockSpec((1, d, tf), lambda i, e, j: (e, 0, j)),
                  pl.BlockSpec((1, tf, d), lambda i, e, j: (e, j, 0))],
        out_specs=pl.BlockSpec((tm, d), lambda i, e, j: (i, 0)),
        out_shape=jax.ShapeDtypeStruct((n, d), F32),
        scratch_shapes=[pltpu.VMEM((tm, d), F32)],
        compiler_params=_cparams("parallel", "arbitrary", "arbitrary"),
        name="moe_ffn",
    )(h, xm, gates, wg, wu, wd)


MOE_ROW_ALIGN = 32
MOE_GATE_LANES = 128
MOE_TILE = 1024
MOE_GROUP_ROWS = 1024
_UNSELECTED = 1e9


def _tile_routing(gates_sel, rank, axis):
    n_e = rank.shape[axis]
    cnt = jnp.max(rank, axis=1 - axis, keepdims=True)
    padded = jnp.floor((cnt + (MOE_ROW_ALIGN - 1)) * (1.0 / MOE_ROW_ALIGN)) * MOE_ROW_ALIGN
    offs, run = [], jnp.zeros((1, 1), F32)
    for e in range(n_e):
        offs.append(run)
        run = run + (padded[e:e + 1] if axis == 0 else padded[:, e:e + 1])
    off = jnp.concatenate(offs, axis=axis)
    dest = jnp.where(gates_sel, off + rank - 1.0, -1.0)
    d_hi = jnp.max(dest, axis=axis, keepdims=True)
    d_lo = jnp.min(jnp.where(gates_sel, dest, _UNSELECTED), axis=axis, keepdims=True)
    d_lo = jnp.where(d_lo == d_hi, -2.0, d_lo)
    return cnt, dest, d_hi, d_lo


def _moe_dispatch_kernel(h_ref, gate_ref, hs_ref, cnt_ref, *, chunk_rows):
    h = h_ref[...]
    gates = gate_ref[...]
    tm, n_e = gates.shape
    d = h.shape[1]
    mt = hs_ref.shape[0]
    eye = (lax.broadcasted_iota(jnp.int32, (n_e, n_e), 0)
           == lax.broadcasted_iota(jnp.int32, (n_e, n_e), 1)).astype(BF16)
    to_rows = lambda u: lax.dot_general(eye, u, (((1,), (1,)), ((), ())), preferred_element_type=F32)
    g_hi3, g_mid3, g_lo3 = _split3(gates)
    gate_row = to_rows(g_hi3) + to_rows(g_mid3) + to_rows(g_lo3)
    sel_row = gate_row > 0.0
    ti = lax.broadcasted_iota(jnp.int32, (tm, tm), 0)
    tj = lax.broadcasted_iota(jnp.int32, (tm, tm), 1)
    rank = jnp.dot(sel_row.astype(BF16), (ti <= tj).astype(BF16), preferred_element_type=F32)
    cnt, dest, d_hi, d_lo = _tile_routing(sel_row, rank, 0)
    cnt_ref[0] = jnp.broadcast_to(cnt, cnt_ref.shape[1:])
    g_hi = jnp.sum(jnp.where(dest == d_hi, gate_row, 0.0), axis=0, keepdims=True)
    g_lo = jnp.sum(jnp.where(dest == d_lo, gate_row, 0.0), axis=0, keepdims=True)
    lane = lax.broadcasted_iota(jnp.int32, (chunk_rows, MOE_GATE_LANES), 1)
    for c0 in range(0, mt, chunk_rows):
        rho = (c0 + lax.broadcasted_iota(jnp.int32, (chunk_rows, 1), 0)).astype(F32)
        is_hi = d_hi == rho
        is_lo = d_lo == rho
        onehot = jnp.where(is_hi, 1.0, jnp.where(is_lo, 1.0, 0.0)).astype(BF16)
        rows = jnp.dot(onehot, h, preferred_element_type=F32)
        g = jnp.sum(jnp.where(is_hi, g_hi, jnp.where(is_lo, g_lo, 0.0)), axis=1, keepdims=True)
        p0, p1, p2 = [t.astype(F32) for t in _split3(g)]
        gcols = jnp.where(lane == 0, p0, jnp.where(lane == 1, p1, jnp.where(lane == 2, p2, 0.0)))
        hs_ref[c0:c0 + chunk_rows, 0:d] = rows.astype(BF16)
        hs_ref[c0:c0 + chunk_rows, d:d + MOE_GATE_LANES] = gcols.astype(BF16)


def _moe_dispatch(h, gates, tm, mt):
    n, d = h.shape
    n_e = gates.shape[1]
    n_tiles = n // tm
    return pl.pallas_call(
        functools.partial(_moe_dispatch_kernel, chunk_rows=256),
        grid=(n_tiles,),
        in_specs=[pl.BlockSpec((tm, d), lambda i: (i, 0)),
                  pl.BlockSpec((tm, n_e), lambda i: (i, 0))],
        out_specs=[pl.BlockSpec((mt, d + MOE_GATE_LANES), lambda i: (i, 0)),
                   pl.BlockSpec((1, n_e, 128), lambda i: (i, 0, 0))],
        out_shape=[jax.ShapeDtypeStruct((n_tiles * mt, d + MOE_GATE_LANES), BF16),
                   jax.ShapeDtypeStruct((n_tiles, n_e, 128), F32)],
        compiler_params=_cparams("parallel"),
        name="moe_dispatch",
    )(h, gates)


def _moe_tables(cnt, mt, group_rows):
    i32 = jnp.int32
    n_tiles, n_e = cnt.shape
    bpt = mt // MOE_ROW_ALIGN
    g = group_rows // MOE_ROW_ALIGN
    nblk = jnp.ceil(cnt / MOE_ROW_ALIGN).astype(i32)
    cum = jnp.cumsum(nblk, axis=1)
    off_blk = cum - nblk
    src_base = jnp.arange(n_tiles, dtype=i32)[:, None] * bpt + off_blk
    tot = jnp.sum(nblk, axis=0)
    totp = ((tot + g - 1) // g) * g
    es = jnp.cumsum(totp) - totp
    nblk_t = nblk.T
    seg_start = es[:, None] + jnp.cumsum(nblk_t, axis=1) - nblk_t
    n_src = n_tiles * bpt
    n_dst = -(-(n_src + n_e * (g - 1)) // g) * g
    dblk = jnp.arange(n_dst, dtype=i32)[:, None]
    s0, sl, ss = seg_start.reshape(1, -1), nblk_t.reshape(1, -1), src_base.T.reshape(1, -1)
    hit = (dblk >= s0) & (dblk < s0 + sl)
    src_of_dst = jnp.sum(jnp.where(hit, ss + dblk - s0, 0), axis=1)
    n_rb = n_dst // g
    eid = jnp.sum(jnp.arange(n_rb, dtype=i32)[:, None] * g >= es[None, :], axis=1).astype(i32) - 1
    n_valid = ((es[-1] + totp[-1]) // g).reshape(1)
    o_of = jnp.arange(bpt, dtype=i32)[None, :, None]
    off3, len3 = off_blk[:, None, :], nblk[:, None, :]
    hit3 = (o_of >= off3) & (o_of < off3 + len3)
    dst_of_src = jnp.sum(jnp.where(hit3, seg_start.T[:, None, :] + o_of - off3, 0), axis=2).reshape(-1)
    return src_of_dst.astype(i32), dst_of_src.astype(i32), eid, n_valid.astype(i32), n_dst


def _block_gather_kernel(idx_ref, *refs):
    del idx_ref
    out_ref = refs[-1]
    rows = refs[0].shape[0]
    for q, src_ref in enumerate(refs[:-1]):
        out_ref[q * rows:(q + 1) * rows, :] = src_ref[...]


def _block_gather(idx, src, n_blocks, per_step, name):
    rows = MOE_ROW_ALIGN
    width = src.shape[1]
    pick = lambda q: pl.BlockSpec((rows, width), lambda i, idx: (idx[i * per_step + q], 0))
    return pl.pallas_call(
        _block_gather_kernel,
        grid_spec=pltpu.PrefetchScalarGridSpec(
            num_scalar_prefetch=1, grid=(n_blocks // per_step,),
            in_specs=[pick(q) for q in range(per_step)],
            out_specs=pl.BlockSpec((per_step * rows, width), lambda i, idx: (i, 0))),
        out_shape=jax.ShapeDtypeStruct((n_blocks * rows, width), src.dtype),
        compiler_params=_cparams("arbitrary"),
        name=name,
    )(idx, *([src] * per_step))


def _moe_group_kernel(eid_ref, nv_ref, hs_ref, wg_ref, wu_ref, wd_ref, ys_ref, acc_ref):
    i = pl.program_id(0)
    j = pl.program_id(1)
    d = ys_ref.shape[1]

    @pl.when(i < nv_ref[0])
    def _():
        h = hs_ref[:, 0:d]
        act = (_silu(jnp.dot(h, wg_ref[0], preferred_element_type=F32))
               * jnp.dot(h, wu_ref[0], preferred_element_type=F32))
        part = jnp.dot(act.astype(BF16), wd_ref[0], preferred_element_type=F32)

        @pl.when(j == 0)
        def _():
            acc_ref[...] = part

        @pl.when(j > 0)
        def _():
            acc_ref[...] += part

        @pl.when(j == pl.num_programs(1) - 1)
        def _():
            gp = hs_ref[:, d:d + MOE_GATE_LANES].astype(F32)
            gate = gp[:, 0:1] + gp[:, 1:2] + gp[:, 2:3]
            ys_ref[...] = (acc_ref[...] * gate).astype(BF16)

    @pl.when((i >= nv_ref[0]) & (j == 0))
    def _():
        ys_ref[...] = jnp.zeros(ys_ref.shape, BF16)


def _moe_group_ffn(hs, eid, n_valid, wg, wu, wd, rb, tf):
    rows, dw = hs.shape
    d = dw - MOE_GATE_LANES
    f = wg.shape[2]
    nj = f // tf
    live_j = lambda i, j, nv: jnp.where(i < nv[0], j, nj - 1)
    return pl.pallas_call(
        _moe_group_kernel,
        grid_spec=pltpu.PrefetchScalarGridSpec(
            num_scalar_prefetch=2, grid=(rows // rb, nj),
            in_specs=[pl.BlockSpec((rb, dw), lambda i, j, eid, nv: (jnp.minimum(i, nv[0] - 1), 0)),
                      pl.BlockSpec((1, d, tf), lambda i, j, eid, nv: (eid[i], 0, live_j(i, j, nv))),
                      pl.BlockSpec((1, d, tf), lambda i, j, eid, nv: (eid[i], 0, live_j(i, j, nv))),
                      pl.BlockSpec((1, tf, d), lambda i, j, eid, nv: (eid[i], live_j(i, j, nv), 0))],
            out_specs=pl.BlockSpec((rb, d), lambda i, j, eid, nv: (i, 0)),
            scratch_shapes=[pltpu.VMEM((rb, d), F32)]),
        out_shape=jax.ShapeDtypeStruct((rows, d), BF16),
        compiler_params=_cparams("arbitrary", "arbitrary"),
        name="moe_group_ffn",
    )(eid, n_valid, hs, wg, wu, wd)


def _moe_combine_kernel(xm_ref, gate_ref, ys_ref, o_ref):
    gates = gate_ref[...]
    tm, n_e = gates.shape
    mt = ys_ref.shape[0]
    sel = gates > 0.0
    ti = lax.broadcasted_iota(jnp.int32, (tm, tm), 0)
    tj = lax.broadcasted_iota(jnp.int32, (tm, tm), 1)
    rank = jnp.dot((tj <= ti).astype(BF16), sel.astype(BF16), preferred_element_type=F32)
    _, _, d_hi, d_lo = _tile_routing(sel, rank, 1)
    rho = lax.broadcasted_iota(jnp.int32, (1, mt), 1).astype(F32)
    onehot = jnp.where(d_hi == rho, 1.0, jnp.where(d_lo == rho, 1.0, 0.0)).astype(BF16)
    o_ref[...] = xm_ref[...] + jnp.dot(onehot, ys_ref[...], preferred_element_type=F32)


def _moe_combine(xm, gates, ys, tm, mt):
    n, d = xm.shape
    n_e = gates.shape[1]
    return pl.pallas_call(
        _moe_combine_kernel,
        grid=(n // tm,),
        in_specs=[pl.BlockSpec((tm, d), lambda i: (i, 0)),
                  pl.BlockSpec((tm, n_e), lambda i: (i, 0)),
                  pl.BlockSpec((mt, d), lambda i: (i, 0))],
        out_specs=pl.BlockSpec((tm, d), lambda i: (i, 0)),
        out_shape=jax.ShapeDtypeStruct((n, d), F32),
        compiler_params=_cparams("parallel"),
        name="moe_combine",
    )(xm, gates, ys)


def _moe_sparse(h, xm, gates, wg, wu, wd, tf):
    n_e = gates.shape[1]
    tm = MOE_TILE
    mt = TOP_K * tm + n_e * MOE_ROW_ALIGN
    hs_tile, cnt = _moe_dispatch(h, gates, tm, mt)
    src_of_dst, dst_of_src, eid, n_valid, n_dst = _moe_tables(cnt[:, :, 0], mt, MOE_GROUP_ROWS)
    per_step = MOE_GROUP_ROWS // MOE_ROW_ALIGN
    hs_exp = _block_gather(src_of_dst, hs_tile, n_dst, per_step, "moe_gather_blocks")
    ys_exp = _moe_group_ffn(hs_exp, eid, n_valid, wg, wu, wd, MOE_GROUP_ROWS, tf)
    bpt = mt // MOE_ROW_ALIGN
    per_step_back = max(c for c in range(1, per_step + 8) if bpt % c == 0)
    ys_tile = _block_gather(dst_of_src, ys_exp, hs_tile.shape[0] // MOE_ROW_ALIGN, per_step_back,
                            "moe_scatter_blocks")
    return _moe_combine(xm, gates, ys_tile, tm, mt)


def _pick(n, candidates):
    for c in candidates:
        if n % c == 0:
            return c
    raise ValueError(f"no tile for {n}")


def _trunk(x, layers, cache_k, cache_v, state_pool, state_shift, state_wkv):
    prompt = cache_k is None
    b, t, d = x.shape
    n = b * t
    tm = _pick(n, (512, 256, 128))
    tm_ffn = _pick(n, (1024, 512, 256, 128))
    cq = CHUNK if prompt else t
    rq = _pick(t, (2 * CHUNK,)) if prompt else t
    n_part = _pick(t // rq, (4, 2, 1))
    pool_tb = _pick(t, (512, 256, 128, 64, 32))
    r_chunk = CHUNK if prompt else t
    r_tb = _pick(t, (512, 256, 128, 64, 32))
    x2 = x.reshape(n, d)
    nk, nv, npool, nshift, nwkv = [], [], [], [], []
    for l, lp in enumerate(layers):
        proj2 = _norm_matmul(x2, lp["norm1_g"], lp["w_in"], tm)
        proj = proj2.reshape(b, t, -1)
        c_pool = D_A + 2 * D_KV
        c_r = c_pool + D_POOL
        v_raw = proj[:, :, D_A + D_KV:c_pool]
        if prompt:
            cache_kv = None
            prefix = jnp.zeros((b, POOL_HALO, D_POOL), F32)
            prev = jnp.zeros((b, 1, D_R_IN), F32)
            st0 = jnp.zeros((b, HEAD_DIM, D_R), F32)
        else:
            cache_kv = jnp.concatenate([cache_k[l].reshape(b, WINDOW, D_KV),
                                        cache_v[l].reshape(b, WINDOW, D_KV)], axis=-1)
            prefix = jnp.pad(state_pool[l], ((0, 0), (POOL_HALO - POOL_CTX, 0), (0, 0)))
            prev = state_shift[l]
            st0 = _state_to_wide(state_wkv[l])
        ya, k_norm = _attention(proj, cache_kv, lp["q_gain"], lp["k_gain"], lp["sink"], cq, rq, n_part)
        yb = _pool_mix(proj, prefix, lp["pool_w"], lp["pool_scale"], 0 if prompt else POOL_CTX, pool_tb)
        yc, st_fin = _rwkv_mix(proj, prev, st0, lp, r_chunk, r_tb)
        keep = WINDOW if prompt else t
        nk.append(k_norm[:, t - keep:].reshape(b, keep, A_KV_HEADS, HEAD_DIM))
        nv.append(v_raw[:, t - keep:].reshape(b, keep, A_KV_HEADS, HEAD_DIM))
        npool.append(proj[:, t - POOL_CTX:, c_pool:c_r])
        nshift.append(proj[:, t - 1:, c_r:])
        nwkv.append(_wide_to_state(st_fin))
        outs = _mix_out(x2, ya.reshape(n, -1), yb.reshape(n, -1), yc.reshape(n, -1),
                        lp["w_out"], lp["norm2_g"], lp.get("router_w"), tm)
        if "router_w" in lp:
            xm, h2, gates = outs
            if n % MOE_TILE == 0 and n >= 2 * MOE_TILE:
                x2 = _moe_sparse(h2, xm, gates, lp["wg"], lp["wu"], lp["wd"], 512)
            else:
                x2 = _moe(h2, xm, gates, lp["wg"], lp["wu"], lp["wd"], tm_ffn, 512)
        else:
            xm, h2 = outs
            x2 = _ffn(h2, xm, lp["wg"], lp["wu"], lp["wd"], tm_ffn, 512)
    return (x2.reshape(b, t, d), jnp.stack(nk), jnp.stack(nv), jnp.stack(npool),
            jnp.stack(nshift), jnp.stack(nwkv))


def kernel(x_prompt, x_sample, cache_k, cache_v, state_pool, state_shift, state_wkv, norm1_g, w_in, q_gain, k_gain, attn_sink, pool_w, pool_scale, shift_mu, decay_w0, decay_w2, iclr_a0, iclr_a2, gate_g2, k_k, k_a, r_k, lnx_g, lnx_b, w_out, norm2_g, ffn_wg, ffn_wu, ffn_wd, router_w, moe_wg, moe_wu, moe_wd):
    depth = w_in.shape[0]
    r_w = decay_w2.shape[1]
    layers = []
    for l in range(depth):
        lp = dict(
            norm1_g=norm1_g[l][None], w_in=w_in[l].astype(BF16),
            q_gain=q_gain[l][None], k_gain=k_gain[l][None], sink=attn_sink[l],
            pool_w=jax.scipy.linalg.block_diag(*[pool_w[l, gi] for gi in range(len(POOL_WINDOWS))]).astype(BF16),
            pool_scale=pool_scale[l][None],
            mu=shift_mu[l][None], w0=decay_w0[l][None], a0=iclr_a0[l][None],
            w2=jnp.pad(decay_w2[l], ((0, R_WA - r_w), (0, 0))).astype(BF16),
            a2=jnp.pad(iclr_a2[l], ((r_w, 0), (0, 0))).astype(BF16),
            g2=gate_g2[l].astype(BF16),
            k_k=k_k[l][None], k_a=k_a[l][None], r_k=r_k[l].reshape(1, D_R),
            lnx_g=lnx_g[l][None], lnx_b=lnx_b[l][None],
            w_out=w_out[l].astype(BF16), norm2_g=norm2_g[l][None])
        if l % 2 == 0:
            lp.update(wg=ffn_wg[l // 2].astype(BF16), wu=ffn_wu[l // 2].astype(BF16),
                      wd=ffn_wd[l // 2].astype(BF16))
        else:
            lp.update(router_w=router_w[l // 2], wg=moe_wg[l // 2].astype(BF16),
                      wu=moe_wu[l // 2].astype(BF16), wd=moe_wd[l // 2].astype(BF16))
        layers.append(lp)
    y_p, pk, pv, ppool, pshift, pwkv = _trunk(x_prompt, layers, None, None, None, None, None)
    y_s, sk, sv, spool, sshift, swkv = _trunk(x_sample, layers, cache_k, cache_v, state_pool,
                                              state_shift, state_wkv)
    return (y_p, y_s, pk, pv, ppool, pshift, pwkv, sk, sv, spool, sshift, swkv)
```

```python
import functools

import jax
import jax.numpy as jnp
from jax import lax
from jax.experimental import pallas as pl
from jax.experimental.pallas import tpu as pltpu

F32 = jnp.float32
BF16 = jnp.bfloat16

HEAD_DIM = 64
A_HEADS = 8
A_KV_HEADS = 2
A_GROUP = A_HEADS // A_KV_HEADS
D_A = A_HEADS * HEAD_DIM
D_KV = A_KV_HEADS * HEAD_DIM
WINDOW = 128
CHUNK = 64
POOL_WINDOWS = (2, 4, 8, 16)
POOL_CTX = 15
POOL_HALO = 16
D_POOL = 256
POOL_GW = D_POOL // len(POOL_WINDOWS)
D_R = 256
R_HEADS = D_R // HEAD_DIM
R_WA = 128
R_G = 128
D_R_IN = 3 * D_R + R_WA + R_G
TOP_K = 2
NORM_EPS = 1e-6
GN_EPS = 64e-5
NEG_INF = -1e30
VMEM_LIMIT_BYTES = 56 * 1024 * 1024


def _cparams(*sem):
    return pltpu.CompilerParams(dimension_semantics=sem, vmem_limit_bytes=VMEM_LIMIT_BYTES)


def _mm(a, b):
    return jnp.dot(a.astype(BF16), b.astype(BF16), preferred_element_type=F32)


def _mm_nt(a, b):
    return lax.dot_general(a.astype(BF16), b.astype(BF16), (((1,), (1,)), ((), ())),
                           preferred_element_type=F32)


def _split2(x):
    hi = x.astype(BF16)
    lo = (x - hi.astype(F32)).astype(BF16)
    return hi, lo


def _split3(x):
    hi = x.astype(BF16)
    r1 = x - hi.astype(F32)
    mid = r1.astype(BF16)
    lo = (r1 - mid.astype(F32)).astype(BF16)
    return hi, mid, lo


def _mm_exact_rhs(x, b):
    bb = b.astype(BF16)
    hi, mid, lo = _split3(x)
    dot = lambda u: jnp.dot(u, bb, preferred_element_type=F32)
    return dot(hi) + dot(mid) + dot(lo)


def _mm_exact_lhs(a, x):
    ab = a.astype(BF16)
    hi, mid, lo = _split3(x)
    dot = lambda u: jnp.dot(ab, u, preferred_element_type=F32)
    return dot(hi) + dot(mid) + dot(lo)


def _mm_hi(a, b):
    ah, al = _split2(a)
    bh, bl = _split2(b)
    dot = lambda u, v: jnp.dot(u, v, preferred_element_type=F32)
    return dot(ah, bh) + (dot(ah, bl) + dot(al, bh))


def _sigmoid(x):
    return 1.0 / (1.0 + jnp.exp(-x))


def _silu(x):
    return x * _sigmoid(x)


def _norm_matmul_kernel(x_ref, g_ref, w_ref, o_ref):
    x = x_ref[...]
    h = x * lax.rsqrt(jnp.mean(x * x, axis=-1, keepdims=True) + NORM_EPS) * g_ref[...]
    o_ref[...] = jnp.dot(h.astype(BF16), w_ref[...], preferred_element_type=F32)


def _norm_matmul(x, g, w, tm):
    n, d = x.shape
    dout = w.shape[1]
    return pl.pallas_call(
        _norm_matmul_kernel,
        grid=(n // tm,),
        in_specs=[pl.BlockSpec((tm, d), lambda i: (i, 0)),
                  pl.BlockSpec((1, d), lambda i: (0, 0)),
                  pl.BlockSpec((d, dout), lambda i: (0, 0))],
        out_specs=pl.BlockSpec((tm, dout), lambda i: (i, 0)),
        out_shape=jax.ShapeDtypeStruct((n, dout), F32),
        compiler_params=_cparams("parallel"),
        name="norm_in_proj",
    )(x, g, w)


def _block_ones(n):
    r = lax.broadcasted_iota(jnp.int32, (n, n), 0) // HEAD_DIM
    c = lax.broadcasted_iota(jnp.int32, (n, n), 1) // HEAD_DIM
    return (r == c).astype(BF16)


def _head_rms_scale(z, ones):
    hi, lo = _split2(z * z)
    ss = jnp.dot(hi, ones, preferred_element_type=F32) + jnp.dot(lo, ones, preferred_element_type=F32)
    return lax.rsqrt(ss * (1.0 / HEAD_DIM) + NORM_EPS)


def _attn_kernel(sink_ref, q_ref, kv_ref, prev_ref, bias_ref, qg_ref, kg_ref, ya_ref, kn_ref,
                 *, rq, n_part, prev_is_raw):
    i = pl.program_id(1)
    kp = WINDOW + rq
    gw = A_GROUP * HEAD_DIM
    q = q_ref[0]
    kv = kv_ref[0]
    pv = prev_ref[0]
    ones_q = _block_ones(D_A)
    ones_k = _block_ones(D_KV)
    qn = (q * _head_rms_scale(q, ones_q) * qg_ref[...]).astype(BF16)
    k_cur = kv[:, 0:D_KV]
    k_cur = k_cur * _head_rms_scale(k_cur, ones_k) * kg_ref[...]
    kn_ref[0] = k_cur
    k_prev = pv[:, 0:D_KV]
    if prev_is_raw:
        k_prev = k_prev * _head_rms_scale(k_prev, ones_k) * kg_ref[...]
    k_all = jnp.concatenate([k_prev, k_cur], axis=0).astype(BF16)
    v_all = jnp.concatenate([pv[:, D_KV:], kv[:, D_KV:]], axis=0).astype(BF16)
    nq = A_GROUP * rq
    lane_in = lax.broadcasted_iota(jnp.int32, (D_KV, gw), 0)
    lane_out = lax.broadcasted_iota(jnp.int32, (D_KV, gw), 1)
    lane_out_t = lax.broadcasted_iota(jnp.int32, (gw, D_KV), 0)
    lane_in_t = lax.broadcasted_iota(jnp.int32, (gw, D_KV), 1)
    slot_lane = lax.broadcasted_iota(jnp.int32, (1, gw), 1) // HEAD_DIM
    slot_mask_bf = [(slot_lane == hh).astype(BF16) for hh in range(A_GROUP)]
    slot_row = lax.broadcasted_iota(jnp.int32, (gw, rq), 0) // HEAD_DIM
    key_row = lax.broadcasted_iota(jnp.int32, (kp, nq), 0)
    col_head = lax.broadcasted_iota(jnp.int32, (1, nq), 1) // rq
    for g in range(A_KV_HEADS):
        select = (lane_in // HEAD_DIM == g) & (lane_in % HEAD_DIM == lane_out % HEAD_DIM)
        k_wide = jnp.dot(k_all, select.astype(BF16), preferred_element_type=F32).astype(BF16)
        select_t = (lane_in_t // HEAD_DIM == g) & (lane_in_t % HEAD_DIM == lane_out_t % HEAD_DIM)
        v_wide_t = lax.dot_general(select_t.astype(BF16), v_all, (((1,), (1,)), ((), ())),
                                   preferred_element_type=F32).astype(BF16)
        q_g = qn[:, g * gw:(g + 1) * gw]
        sink_row = jnp.zeros((1, nq), F32)
        for hh in range(A_GROUP):
            sink_row = jnp.where(col_head == hh, sink_ref[g * A_GROUP + hh], sink_row)
        bias_t = bias_ref[g]
        for p in range(n_part):
            rows = slice(p * rq, (p + 1) * rq)
            qs = jnp.concatenate([q_g[rows] * m for m in slot_mask_bf], axis=0)
            kb = k_wide[p * rq:p * rq + kp]
            s = lax.dot_general(kb, qs, (((1,), (1,)), ((), ())), preferred_element_type=F32) + bias_t
            if prev_is_raw and p * rq < WINDOW:
                n_pad = WINDOW - (i * n_part + p) * rq
                s = s + jnp.where(key_row < n_pad, NEG_INF, 0.0)
            m = jnp.maximum(jnp.max(s, axis=0, keepdims=True), sink_row)
            e = jnp.exp(s - m)
            den = jnp.sum(e, axis=0, keepdims=True) + jnp.exp(sink_row - m)
            prob = (e * (1.0 / den)).astype(BF16)
            o_t = jnp.dot(v_wide_t[:, p * rq:p * rq + kp], prob, preferred_element_type=F32)
            out_t = o_t[:, (A_GROUP - 1) * rq:]
            for hh in range(A_GROUP - 2, -1, -1):
                out_t = jnp.where(slot_row == hh, o_t[:, hh * rq:(hh + 1) * rq], out_t)
            ya_ref[0, rows, g * gw:(g + 1) * gw] = out_t.T.astype(BF16)


def _attn_bias(rq, cq):
    kp = WINDOW + rq
    i = jnp.arange(rq)[:, None]
    j = jnp.arange(kp)[None, :]
    jb = j - cq * (i // cq)
    valid = (jb >= 0) & (jb < WINDOW + cq)
    dist = jnp.abs(WINDOW + (i % cq) - jb).astype(F32)
    slopes = jnp.exp2(-8.0 * jnp.arange(1, A_HEADS + 1, dtype=F32) / A_HEADS)
    bias = jnp.where(valid[None], -slopes[:, None, None] * dist[None], NEG_INF)
    return jnp.swapaxes(bias.reshape(A_KV_HEADS, A_GROUP * rq, kp), 1, 2)


def _attention(proj, cache_kv, q_gain, k_gain, sink, cq, rq, n_part):
    b, t, _ = proj.shape
    tq = rq * n_part
    prompt = cache_kv is None
    kv_col = D_A // (2 * D_KV)
    if prompt:
        prev_arr = proj
        prev_spec = pl.BlockSpec((1, WINDOW, 2 * D_KV),
                                 lambda bi, i: (bi, jnp.maximum(i * (tq // WINDOW) - 1, 0), kv_col))
    else:
        prev_arr = cache_kv
        prev_spec = pl.BlockSpec((1, WINDOW, 2 * D_KV), lambda bi, i: (bi, 0, 0))
    kp = WINDOW + rq
    q_gain_t = jnp.tile(q_gain * (HEAD_DIM ** -0.5), (1, A_HEADS))
    k_gain_t = jnp.tile(k_gain, (1, A_KV_HEADS))
    kern = functools.partial(_attn_kernel, rq=rq, n_part=n_part, prev_is_raw=prompt)
    return pl.pallas_call(
        kern,
        grid=(b, t // tq),
        in_specs=[pl.BlockSpec(memory_space=pltpu.SMEM),
                  pl.BlockSpec((1, tq, D_A), lambda bi, i: (bi, i, 0)),
                  pl.BlockSpec((1, tq, 2 * D_KV), lambda bi, i: (bi, i, kv_col)),
                  prev_spec,
                  pl.BlockSpec((A_KV_HEADS, kp, A_GROUP * rq), lambda bi, i: (0, 0, 0)),
                  pl.BlockSpec((1, D_A), lambda bi, i: (0, 0)),
                  pl.BlockSpec((1, D_KV), lambda bi, i: (0, 0))],
        out_specs=[pl.BlockSpec((1, tq, D_A), lambda bi, i: (bi, i, 0)),
                   pl.BlockSpec((1, tq, D_KV), lambda bi, i: (bi, i, 0))],
        out_shape=[jax.ShapeDtypeStruct((b, t, D_A), BF16),
                   jax.ShapeDtypeStruct((b, t, D_KV), F32)],
        compiler_params=_cparams("parallel", "arbitrary"),
        name="swa_attention",
    )(sink, proj, proj, prev_arr, _attn_bias(rq, cq), q_gain_t, k_gain_t)


def _pool_kernel(u_ref, halo_ref, prefix_ref, w_ref, scale_ref, yb_ref, *, n_prefix):
    i = pl.program_id(1)
    u = u_ref[0]
    tb = u.shape[0]
    halo = jnp.where(i == 0, prefix_ref[0], halo_ref[0])
    ext = jnp.concatenate([halo, u], axis=0)
    col = lax.broadcasted_iota(jnp.int32, (1, D_POOL), 1)
    pos = i * tb + lax.broadcasted_iota(jnp.int32, (tb, 1), 0)
    total = None
    count = None
    acc = ext
    span = 1
    for gi, w in enumerate(POOL_WINDOWS):
        while span < w:
            acc = acc + pltpu.roll(acc, span, axis=0)
            span *= 2
        in_group = (col >= gi * POOL_GW) & (col < (gi + 1) * POOL_GW)
        tail = acc[POOL_HALO:]
        total = jnp.where(in_group, tail, 0.0 if total is None else total)
        cnt = jnp.minimum(pos + (1 + n_prefix), w).astype(F32)
        count = jnp.where(in_group, cnt, 1.0 if count is None else count)
    d = total / count - u
    yb_ref[0] = (_mm(d, w_ref[...]) * scale_ref[...]).astype(BF16)


def _pool_mix(proj, prefix, w_blockdiag, scale, n_prefix, tb):
    b, t, _ = proj.shape
    col = (D_A + 2 * D_KV) // D_POOL
    kern = functools.partial(_pool_kernel, n_prefix=n_prefix)
    return pl.pallas_call(
        kern,
        grid=(b, t // tb),
        in_specs=[pl.BlockSpec((1, tb, D_POOL), lambda bi, i: (bi, i, col)),
                  pl.BlockSpec((1, POOL_HALO, D_POOL),
                               lambda bi, i: (bi, jnp.maximum(i * (tb // POOL_HALO) - 1, 0), col)),
                  pl.BlockSpec((1, POOL_HALO, D_POOL), lambda bi, i: (bi, 0, 0)),
                  pl.BlockSpec((D_POOL, D_POOL), lambda bi, i: (0, 0)),
                  pl.BlockSpec((1, D_POOL), lambda bi, i: (0, 0))],
        out_specs=pl.BlockSpec((1, tb, D_POOL), lambda bi, i: (bi, i, 0)),
        out_shape=jax.ShapeDtypeStruct((b, t, D_POOL), BF16),
        compiler_params=_cparams("parallel", "arbitrary"),
        name="pool_mix",
    )(proj, proj, prefix, w_blockdiag, scale)


def _rwkv_kernel(p_ref, prev_ref, st0_ref, mu_ref, w0_ref, a0_ref, w2_ref, a2_ref, g2_ref,
                 kk_ref, ka_ref, rk_ref, lng_ref, lnb_ref, yc_ref, st_ref, carry_ref, y_ref,
                 *, chunk):
    j = pl.program_id(1)

    @pl.when(j == 0)
    def _():
        carry_ref[...] = prev_ref[0]
        st_ref[0] = st0_ref[0]

    p = p_ref[0]
    tb = p.shape[0]
    n_chunk = tb // chunk
    hl = R_HEADS * chunk

    row = lax.broadcasted_iota(jnp.int32, (tb, 1), 0)
    p_prev = jnp.where(row == 0, carry_ref[...], pltpu.roll(p, 1, axis=0))
    carry_ref[...] = p[tb - 1:tb]
    xs = p + mu_ref[...] * (p_prev - p)
    r = xs[:, 0:D_R]
    k = xs[:, D_R:2 * D_R]
    v = xs[:, 2 * D_R:3 * D_R]
    wa = xs[:, 3 * D_R:3 * D_R + R_WA]
    gd = xs[:, 3 * D_R + R_WA:]

    z = -(w0_ref[...] + _mm(jnp.tanh(wa), w2_ref[...]))
    softplus = jnp.maximum(z, 0.0) + jnp.log(1.0 + jnp.exp(-jnp.abs(z)))
    lw = -jnp.exp(-softplus - 0.5)
    a = _sigmoid(a0_ref[...] + _mm(wa, a2_ref[...]))
    g = _mm(_sigmoid(gd), g2_ref[...])

    lane_r = lax.broadcasted_iota(jnp.int32, (D_R, D_R), 0) // HEAD_DIM
    lane_c = lax.broadcasted_iota(jnp.int32, (D_R, D_R), 1) // HEAD_DIM
    head_ones = (lane_r == lane_c).astype(F32)
    seg_sum = lambda t: _mm_exact_rhs(t, head_ones)

    kk = k * kk_ref[...]
    kk = kk / jnp.maximum(jnp.sqrt(seg_sum(kk * kk)), 1e-12)
    k2 = k * (1.0 + (a - 1.0) * ka_ref[...])
    bb = kk * a

    ti = lax.broadcasted_iota(jnp.int32, (tb, tb), 0)
    tj = lax.broadcasted_iota(jnp.int32, (tb, tb), 1)
    cum = _mm_exact_lhs(((ti // chunk == tj // chunk) & (tj <= ti)).astype(F32), lw)

    wi = lax.broadcasted_iota(jnp.int32, (chunk, hl), 0)
    wj = lax.broadcasted_iota(jnp.int32, (chunk, hl), 1) % chunk
    strict = wj < wi
    incl = wj <= wi
    eye_w = (wj == wi).astype(F32)
    diag_k = (lax.broadcasted_iota(jnp.int32, (HEAD_DIM, D_R), 0)
              == lax.broadcasted_iota(jnp.int32, (HEAD_DIM, D_R), 1) % HEAD_DIM)
    lane_k = lax.broadcasted_iota(jnp.int32, (1, D_R), 1) // HEAD_DIM
    lane_t = lax.broadcasted_iota(jnp.int32, (1, hl), 1) // chunk
    mask_k = [(lane_k == h).astype(F32) for h in range(R_HEADS)]
    mask_k_bf = [m.astype(BF16) for m in mask_k]
    mask_t_bf = [(lane_t == h).astype(BF16) for h in range(R_HEADS)]

    def blockdiag(t, masks):
        t16 = t.astype(BF16)
        return jnp.concatenate([t16 * m for m in masks], axis=0)

    def wide_transpose(t):
        tt = jnp.concatenate([t * m for m in mask_k], axis=0).T
        out = tt[0:HEAD_DIM]
        for h in range(1, R_HEADS):
            out = out + tt[h * HEAD_DIM:(h + 1) * HEAD_DIM]
        return out.astype(BF16)

    chunks = []
    for c in range(n_chunk):
        sl = slice(c * chunk, (c + 1) * chunk)
        cum_c = cum[sl]
        cum_last = cum_c[chunk - 1:chunk]
        g_in = jnp.exp(cum_c)
        g_prev = jnp.exp(cum_c - lw[sl])
        g_inv = jnp.exp(-cum_c)
        g_out = jnp.exp(cum_last - cum_c)
        a_n = (kk[sl] * g_prev).astype(BF16)
        r_n = r[sl] * g_in
        ch = dict(sl=sl, r_n=r_n, g_last=jnp.exp(cum_last),
                  a_s=blockdiag(a_n, mask_k_bf),
                  v_s=blockdiag(v[sl], mask_k_bf),
                  bo_w=wide_transpose(bb[sl] * g_out),
                  ko_w=wide_transpose(k2[sl] * g_out))
        ar = jnp.concatenate([a_n, r_n.astype(BF16)], axis=0)
        m_b = _mm_nt(ar, blockdiag(bb[sl] * g_inv, mask_k_bf))
        m_k = _mm_nt(ar, blockdiag(k2[sl] * g_inv, mask_k_bf))
        m_ab = jnp.where(strict, m_b[:chunk], 0.0)
        ch.update(m_rb=jnp.where(incl, m_b[chunk:], 0.0).astype(BF16),
                  m_ak=jnp.where(strict, m_k[:chunk], 0.0).astype(BF16),
                  m_rk=jnp.where(incl, m_k[chunk:], 0.0).astype(BF16),
                  t_inv=eye_w - m_ab, pw=m_ab.astype(BF16))
        chunks.append(ch)

    n = 1
    while 2 * n < chunk:
        for ch in chunks:
            ch["pw"] = _mm(ch["pw"], blockdiag(ch["pw"], mask_t_bf)).astype(BF16)
        for ch in chunks:
            ch["t_inv"] = ch["t_inv"] + _mm(ch["t_inv"], blockdiag(ch["pw"], mask_t_bf))
        n *= 2

    for ch in chunks:
        t_inv = ch["t_inv"].astype(BF16)
        ch["a_bar"] = blockdiag(_mm(t_inv, ch["a_s"]), mask_k_bf)
        ch["u0"] = blockdiag(-_mm(t_inv, blockdiag(_mm(ch["m_ak"], ch["v_s"]), mask_k_bf)), mask_k_bf)
    for ch in chunks:
        ch["r_bar"] = ch["r_n"] - _mm(ch["m_rb"], ch["a_bar"])
        ch["y0"] = _mm(ch["m_rk"], ch["v_s"]) + _mm(ch["m_rb"], ch["u0"])
        ch["g_w"] = jnp.where(diag_k, ch["g_last"], 0.0) - _mm(ch["bo_w"], ch["a_bar"])
        ch["h_w"] = _mm(ch["bo_w"], ch["u0"]) + _mm(ch["ko_w"], ch["v_s"])

    st_w = st_ref[0]
    for ch in chunks:
        st = blockdiag(st_w, mask_k_bf)
        y_ref[ch["sl"], :] = ch["y0"] + _mm(ch["r_bar"], st)
        st_w = _mm(ch["g_w"], st) + ch["h_w"]
    st_ref[0] = st_w

    y = y_ref[...]
    mean = seg_sum(y) * (1.0 / HEAD_DIM)
    d = y - mean
    var = seg_sum(d * d) * (1.0 / HEAD_DIM)
    yn = d * lax.rsqrt(var + GN_EPS) * lng_ref[...] + lnb_ref[...]
    bonus = seg_sum(r * k2 * rk_ref[...]) * v
    yc_ref[0] = ((yn + bonus) * g).astype(BF16)


def _rwkv_mix(proj, prev, st0, lp, chunk, tb):
    b, t, _ = proj.shape
    col = (D_A + 2 * D_KV + D_POOL) // D_R_IN
    row = lambda n: pl.BlockSpec((1, n), lambda bi, i: (0, 0))
    full = lambda s: pl.BlockSpec(s, lambda bi, i: (0,) * len(s))
    kern = functools.partial(_rwkv_kernel, chunk=chunk)
    return pl.pallas_call(
        kern,
        grid=(b, t // tb),
        in_specs=[pl.BlockSpec((1, tb, D_R_IN), lambda bi, i: (bi, i, col)),
                  pl.BlockSpec((1, 1, D_R_IN), lambda bi, i: (bi, 0, 0)),
                  pl.BlockSpec((1, HEAD_DIM, D_R), lambda bi, i: (bi, 0, 0)),
                  row(D_R_IN), row(D_R), row(D_R),
                  full((R_WA, D_R)), full((R_WA, D_R)), full((R_G, D_R)),
                  row(D_R), row(D_R), row(D_R), row(D_R), row(D_R)],
        out_specs=[pl.BlockSpec((1, tb, D_R), lambda bi, i: (bi, i, 0)),
                   pl.BlockSpec((1, HEAD_DIM, D_R), lambda bi, i: (bi, 0, 0))],
        out_shape=[jax.ShapeDtypeStruct((b, t, D_R), BF16),
                   jax.ShapeDtypeStruct((b, HEAD_DIM, D_R), F32)],
        scratch_shapes=[pltpu.VMEM((1, D_R_IN), F32), pltpu.VMEM((tb, D_R), F32)],
        compiler_params=_cparams("parallel", "arbitrary"),
        name="rwkv7_mix",
    )(proj, prev, st0, lp["mu"], lp["w0"], lp["a0"], lp["w2"], lp["a2"], lp["g2"],
      lp["k_k"], lp["k_a"], lp["r_k"], lp["lnx_g"], lp["lnx_b"])


def _state_to_wide(s):
    b = s.shape[0]
    return jnp.transpose(s, (0, 3, 1, 2)).reshape(b, HEAD_DIM, D_R)


def _wide_to_state(st):
    b = st.shape[0]
    return jnp.transpose(st.reshape(b, HEAD_DIM, R_HEADS, HEAD_DIM), (0, 2, 3, 1))


def _mix_out_kernel(x_ref, ya_ref, yb_ref, yc_ref, w_ref, g_ref, *rest, route):
    if route:
        rw_ref, xm_ref, h_ref, gate_ref = rest
    else:
        xm_ref, h_ref = rest
    dot = lambda u, lo, hi: jnp.dot(u[...], w_ref[lo:hi, :], preferred_element_type=F32)
    xm = (x_ref[...] + dot(ya_ref, 0, D_A) + dot(yb_ref, D_A, D_A + D_POOL)
          + dot(yc_ref, D_A + D_POOL, D_A + D_POOL + D_R))
    xm_ref[...] = xm
    h = xm * lax.rsqrt(jnp.mean(xm * xm, axis=-1, keepdims=True) + NORM_EPS) * g_ref[...]
    h_ref[...] = h.astype(BF16)
    if route:
        logits = _mm_hi(h, rw_ref[...])
        n_e = logits.shape[-1]
        lane = lax.broadcasted_iota(jnp.int32, logits.shape, 1).astype(F32)
        m1 = jnp.max(logits, axis=-1, keepdims=True)
        i1 = jnp.min(jnp.where(logits == m1, lane, float(n_e)), axis=-1, keepdims=True)
        rest_l = jnp.where(lane == i1, -jnp.inf, logits)
        m2 = jnp.max(rest_l, axis=-1, keepdims=True)
        i2 = jnp.min(jnp.where(rest_l == m2, lane, float(n_e)), axis=-1, keepdims=True)
        e2 = jnp.exp(m2 - m1)
        gate_ref[...] = (jnp.where(lane == i1, 1.0 / (1.0 + e2), 0.0)
                         + jnp.where(lane == i2, e2 / (1.0 + e2), 0.0))


def _mix_out(x, ya, yb, yc, w_out, g, router_w, tm):
    n, d = x.shape
    route = router_w is not None
    tile = lambda w: pl.BlockSpec((tm, w), lambda i: (i, 0))
    in_specs = [tile(d), tile(D_A), tile(D_POOL), tile(D_R),
                pl.BlockSpec(w_out.shape, lambda i: (0, 0)),
                pl.BlockSpec((1, d), lambda i: (0, 0))]
    out_specs = [tile(d), tile(d)]
    out_shape = [jax.ShapeDtypeStruct((n, d), F32), jax.ShapeDtypeStruct((n, d), BF16)]
    args = [x, ya, yb, yc, w_out, g]
    if route:
        n_e = router_w.shape[1]
        in_specs.append(pl.BlockSpec(router_w.shape, lambda i: (0, 0)))
        out_specs.append(tile(n_e))
        out_shape.append(jax.ShapeDtypeStruct((n, n_e), F32))
        args.append(router_w)
    return pl.pallas_call(
        functools.partial(_mix_out_kernel, route=route),
        grid=(n // tm,),
        in_specs=in_specs, out_specs=out_specs, out_shape=out_shape,
        compiler_params=_cparams("parallel"),
        name="mix_out_proj",
    )(*args)


def _swiglu_act(h, wg, wu):
    return (_silu(jnp.dot(h, wg, preferred_element_type=F32))
            * jnp.dot(h, wu, preferred_element_type=F32)).astype(BF16)


def _down_proj(act_ref, wd, tf):
    out = None
    for jj in range(act_ref.shape[0]):
        part = jnp.dot(act_ref[jj], wd[jj * tf:(jj + 1) * tf, :], preferred_element_type=F32)
        out = part if out is None else out + part
    return out


def _ffn_kernel(h_ref, xm_ref, wg_ref, wu_ref, wd_ref, o_ref, act_ref):
    j = pl.program_id(1)
    tf = wg_ref.shape[1]
    act_ref[j] = _swiglu_act(h_ref[...], wg_ref[...], wu_ref[...])

    @pl.when(j == pl.num_programs(1) - 1)
    def _():
        o_ref[...] = xm_ref[...] + _down_proj(act_ref, wd_ref, tf)


def _ffn(h, xm, wg, wu, wd, tm, tf):
    n, d = xm.shape
    f = wg.shape[1]
    return pl.pallas_call(
        _ffn_kernel,
        grid=(n // tm, f // tf),
        in_specs=[pl.BlockSpec((tm, d), lambda i, j: (i, 0)),
                  pl.BlockSpec((tm, d), lambda i, j: (i, 0)),
                  pl.BlockSpec((d, tf), lambda i, j: (0, j)),
                  pl.BlockSpec((d, tf), lambda i, j: (0, j)),
                  pl.BlockSpec((f, d), lambda i, j: (0, 0))],
        out_specs=pl.BlockSpec((tm, d), lambda i, j: (i, 0)),
        out_shape=jax.ShapeDtypeStruct((n, d), F32),
        scratch_shapes=[pltpu.VMEM((f // tf, tm, tf), BF16)],
        compiler_params=_cparams("parallel", "arbitrary"),
        name="swiglu_ffn",
    )(h, xm, wg, wu, wd)


def _moe_kernel(h_ref, xm_ref, gate_ref, wg_ref, wu_ref, wd_ref, o_ref, acc_ref):
    e = pl.program_id(1)
    j = pl.program_id(2)

    @pl.when((e == 0) & (j == 0))
    def _():
        acc_ref[...] = xm_ref[...]

    gates = gate_ref[...]
    lane = lax.broadcasted_iota(jnp.int32, gates.shape, 1)
    gate = jnp.sum(jnp.where(lane == e, gates, 0.0), axis=-1, keepdims=True)
    h = h_ref[...]
    act = (_silu(jnp.dot(h, wg_ref[0], preferred_element_type=F32))
           * jnp.dot(h, wu_ref[0], preferred_element_type=F32))
    acc_ref[...] += gate * jnp.dot(act.astype(BF16), wd_ref[0], preferred_element_type=F32)

    @pl.when((e == pl.num_programs(1) - 1) & (j == pl.num_programs(2) - 1))
    def _():
        o_ref[...] = acc_ref[...]


def _moe(h, xm, gates, wg, wu, wd, tm, tf):
    n, d = xm.shape
    n_e, _, f = wg.shape
    return pl.pallas_call(
        _moe_kernel,
        grid=(n // tm, n_e, f // tf),
        in_specs=[pl.BlockSpec((tm, d), lambda i, e, j: (i, 0)),
                  pl.BlockSpec((tm, d), lambda i, e, j: (i, 0)),
                  pl.BlockSpec((tm, n_e), lambda i, e, j: (i, 0)),
                  pl.BlockSpec((1, d, tf), lambda i, e, j: (e, 0, j)),
                  pl.BlockSpec((1, d, tf), lambda i, e, j: (e, 0, j)),
                  pl.BlockSpec((1, tf, d), lambda i, e, j: (e, j, 0))],
        out_specs=pl.BlockSpec((tm, d), lambda i, e, j: (i, 0)),
        out_shape=jax.ShapeDtypeStruct((n, d), F32),
        scratch_shapes=[pltpu.VMEM((tm, d), F32)],
        compiler_params=_cparams("parallel", "arbitrary", "arbitrary"),
        name="moe_ffn",
    )(h, xm, gates, wg, wu, wd)


MOE_ROW_ALIGN = 32
MOE_GATE_LANES = 128
MOE_TILE = 1024
MOE_GROUP_ROWS = 1024
_UNSELECTED = 1e9


def _tile_routing(gates_sel, rank, axis):
    n_e = rank.shape[axis]
    cnt = jnp.max(rank, axis=1 - axis, keepdims=True)
    padded = jnp.floor((cnt + (MOE_ROW_ALIGN - 1)) * (1.0 / MOE_ROW_ALIGN)) * MOE_ROW_ALIGN
    offs, run = [], jnp.zeros((1, 1), F32)
    for e in range(n_e):
        offs.append(run)
        run = run + (padded[e:e + 1] if axis == 0 else padded[:, e:e + 1])
    off = jnp.concatenate(offs, axis=axis)
    dest = jnp.where(gates_sel, off + rank - 1.0, -1.0)
    d_hi = jnp.max(dest, axis=axis, keepdims=True)
    d_lo = jnp.min(jnp.where(gates_sel, dest, _UNSELECTED), axis=axis, keepdims=True)
    d_lo = jnp.where(d_lo == d_hi, -2.0, d_lo)
    return cnt, dest, d_hi, d_lo


def _moe_dispatch_kernel(h_ref, gate_ref, hs_ref, cnt_ref, *, chunk_rows):
    h = h_ref[...]
    gates = gate_ref[...]
    tm, n_e = gates.shape
    d = h.shape[1]
    mt = hs_ref.shape[0]
    eye = (lax.broadcasted_iota(jnp.int32, (n_e, n_e), 0)
           == lax.broadcasted_iota(jnp.int32, (n_e, n_e), 1)).astype(BF16)
    to_rows = lambda u: lax.dot_general(eye, u, (((1,), (1,)), ((), ())), preferred_element_type=F32)
    g_hi3, g_mid3, g_lo3 = _split3(gates)
    gate_row = to_rows(g_hi3) + to_rows(g_mid3) + to_rows(g_lo3)
    sel_row = gate_row > 0.0
    ti = lax.broadcasted_iota(jnp.int32, (tm, tm), 0)
    tj = lax.broadcasted_iota(jnp.int32, (tm, tm), 1)
    rank = jnp.dot(sel_row.astype(BF16), (ti <= tj).astype(BF16), preferred_element_type=F32)
    cnt, dest, d_hi, d_lo = _tile_routing(sel_row, rank, 0)
    cnt_ref[0] = jnp.broadcast_to(cnt, cnt_ref.shape[1:])
    g_hi = jnp.sum(jnp.where(dest == d_hi, gate_row, 0.0), axis=0, keepdims=True)
    g_lo = jnp.sum(jnp.where(dest == d_lo, gate_row, 0.0), axis=0, keepdims=True)
    lane = lax.broadcasted_iota(jnp.int32, (chunk_rows, MOE_GATE_LANES), 1)
    for c0 in range(0, mt, chunk_rows):
        rho = (c0 + lax.broadcasted_iota(jnp.int32, (chunk_rows, 1), 0)).astype(F32)
        is_hi = d_hi == rho
        is_lo = d_lo == rho
        onehot = jnp.where(is_hi, 1.0, jnp.where(is_lo, 1.0, 0.0)).astype(BF16)
        rows = jnp.dot(onehot, h, preferred_element_type=F32)
        g = jnp.sum(jnp.where(is_hi, g_hi, jnp.where(is_lo, g_lo, 0.0)), axis=1, keepdims=True)
        p0, p1, p2 = [t.astype(F32) for t in _split3(g)]
        gcols = jnp.where(lane == 0, p0, jnp.where(lane == 1, p1, jnp.where(lane == 2, p2, 0.0)))
        hs_ref[c0:c0 + chunk_rows, 0:d] = rows.astype(BF16)
        hs_ref[c0:c0 + chunk_rows, d:d + MOE_GATE_LANES] = gcols.astype(BF16)


def _moe_dispatch(h, gates, tm, mt):
    n, d = h.shape
    n_e = gates.shape[1]
    n_tiles = n // tm
    return pl.pallas_call(
        functools.partial(_moe_dispatch_kernel, chunk_rows=256),
        grid=(n_tiles,),
        in_specs=[pl.BlockSpec((tm, d), lambda i: (i, 0)),
                  pl.BlockSpec((tm, n_e), lambda i: (i, 0))],
        out_specs=[pl.BlockSpec((mt, d + MOE_GATE_LANES), lambda i: (i, 0)),
                   pl.BlockSpec((1, n_e, 128), lambda i: (i, 0, 0))],
        out_shape=[jax.ShapeDtypeStruct((n_tiles * mt, d + MOE_GATE_LANES), BF16),
                   jax.ShapeDtypeStruct((n_tiles, n_e, 128), F32)],
        compiler_params=_cparams("parallel"),
        name="moe_dispatch",
    )(h, gates)


def _moe_tables(cnt, mt, group_rows):
    i32 = jnp.int32
    n_tiles, n_e = cnt.shape
    bpt = mt // MOE_ROW_ALIGN
    g = group_rows // MOE_ROW_ALIGN
    nblk = jnp.ceil(cnt / MOE_ROW_ALIGN).astype(i32)
    cum = jnp.cumsum(nblk, axis=1)
    off_blk = cum - nblk
    src_base = jnp.arange(n_tiles, dtype=i32)[:, None] * bpt + off_blk
    tot = jnp.sum(nblk, axis=0)
    totp = ((tot + g - 1) // g) * g
    es = jnp.cumsum(totp) - totp
    nblk_t = nblk.T
    seg_start = es[:, None] + jnp.cumsum(nblk_t, axis=1) - nblk_t
    n_src = n_tiles * bpt
    n_dst = -(-(n_src + n_e * (g - 1)) // g) * g
    dblk = jnp.arange(n_dst, dtype=i32)[:, None]
    s0, sl, ss = seg_start.reshape(1, -1), nblk_t.reshape(1, -1), src_base.T.reshape(1, -1)
    hit = (dblk >= s0) & (dblk < s0 + sl)
    src_of_dst = jnp.sum(jnp.where(hit, ss + dblk - s0, 0), axis=1)
    n_rb = n_dst // g
    eid = jnp.sum(jnp.arange(n_rb, dtype=i32)[:, None] * g >= es[None, :], axis=1).astype(i32) - 1
    n_valid = ((es[-1] + totp[-1]) // g).reshape(1)
    o_of = jnp.arange(bpt, dtype=i32)[None, :, None]
    off3, len3 = off_blk[:, None, :], nblk[:, None, :]
    hit3 = (o_of >= off3) & (o_of < off3 + len3)
    dst_of_src = jnp.sum(jnp.where(hit3, seg_start.T[:, None, :] + o_of - off3, 0), axis=2).reshape(-1)
    return src_of_dst.astype(i32), dst_of_src.astype(i32), eid, n_valid.astype(i32), n_dst


def _moe_group_kernel(idx_ref, eid_ref, nv_ref, *refs, n_src):
    del idx_ref, eid_ref
    src_refs = refs[:n_src]
    wg_ref, wu_ref, wd_ref, ys_ref, hs_ref, act_ref = refs[n_src:]
    i = pl.program_id(0)
    j = pl.program_id(1)
    d = ys_ref.shape[1]
    tf = wg_ref.shape[2]
    rows = src_refs[0].shape[0]

    @pl.when(i < nv_ref[0])
    def _():
        @pl.when(j == 0)
        def _():
            for q, src_ref in enumerate(src_refs):
                hs_ref[q * rows:(q + 1) * rows, :] = src_ref[...]

        act_ref[j] = _swiglu_act(hs_ref[:, 0:d], wg_ref[0], wu_ref[0])

        @pl.when(j == pl.num_programs(1) - 1)
        def _():
            gp = hs_ref[:, d:d + MOE_GATE_LANES].astype(F32)
            gate = gp[:, 0:1] + gp[:, 1:2] + gp[:, 2:3]
            ys_ref[...] = (_down_proj(act_ref, wd_ref[0], tf) * gate).astype(BF16)

    @pl.when((i >= nv_ref[0]) & (j == 0))
    def _():
        ys_ref[...] = jnp.zeros(ys_ref.shape, BF16)


def _moe_group_ffn(hs, src_of_dst, eid, n_valid, wg, wu, wd, n_rb, rb, tf):
    dw = hs.shape[1]
    d = dw - MOE_GATE_LANES
    f = wg.shape[2]
    nj = f // tf
    n_src = rb // MOE_ROW_ALIGN
    live_i = lambda i, nv: jnp.minimum(i, nv[0] - 1)
    live_j = lambda i, j, nv: jnp.where(i < nv[0], j, nj - 1)
    piece = lambda q: pl.BlockSpec((MOE_ROW_ALIGN, dw),
                                   lambda i, j, idx, eid, nv: (idx[live_i(i, nv) * n_src + q], 0))
    return pl.pallas_call(
        functools.partial(_moe_group_kernel, n_src=n_src),
        grid_spec=pltpu.PrefetchScalarGridSpec(
            num_scalar_prefetch=3, grid=(n_rb, nj),
            in_specs=[piece(q) for q in range(n_src)] + [
                pl.BlockSpec((1, d, tf), lambda i, j, idx, eid, nv: (eid[i], 0, live_j(i, j, nv))),
                pl.BlockSpec((1, d, tf), lambda i, j, idx, eid, nv: (eid[i], 0, live_j(i, j, nv))),
                pl.BlockSpec((1, f, d), lambda i, j, idx, eid, nv: (eid[i], 0, 0))],
            out_specs=pl.BlockSpec((rb, d), lambda i, j, idx, eid, nv: (i, 0)),
            scratch_shapes=[pltpu.VMEM((rb, dw), BF16), pltpu.VMEM((nj, rb, tf), BF16)]),
        out_shape=jax.ShapeDtypeStruct((n_rb * rb, d), BF16),
        compiler_params=_cparams("arbitrary", "arbitrary"),
        name="moe_group_ffn",
    )(src_of_dst, eid, n_valid, *([hs] * n_src), wg, wu, wd)


def _moe_combine_kernel(idx_ref, xm_ref, gate_ref, *refs):
    del idx_ref
    o_ref, ys_ref = refs[-2:]
    rows = refs[0].shape[0]
    for q, src_ref in enumerate(refs[:-2]):
        ys_ref[q * rows:(q + 1) * rows, :] = src_ref[...]
    gates = gate_ref[...]
    tm, n_e = gates.shape
    mt = ys_ref.shape[0]
    sel = gates > 0.0
    ti = lax.broadcasted_iota(jnp.int32, (tm, tm), 0)
    tj = lax.broadcasted_iota(jnp.int32, (tm, tm), 1)
    rank = jnp.dot((tj <= ti).astype(BF16), sel.astype(BF16), preferred_element_type=F32)
    _, _, d_hi, d_lo = _tile_routing(sel, rank, 1)
    rho = lax.broadcasted_iota(jnp.int32, (1, mt), 1).astype(F32)
    onehot = jnp.where(d_hi == rho, 1.0, jnp.where(d_lo == rho, 1.0, 0.0)).astype(BF16)
    o_ref[...] = xm_ref[...] + jnp.dot(onehot, ys_ref[...], preferred_element_type=F32)


def _moe_combine(xm, gates, ys_exp, dst_of_src, tm, mt):
    n, d = xm.shape
    n_e = gates.shape[1]
    bpt = mt // MOE_ROW_ALIGN
    piece = lambda q: pl.BlockSpec((MOE_ROW_ALIGN, d), lambda i, idx: (idx[i * bpt + q], 0))
    return pl.pallas_call(
        _moe_combine_kernel,
        grid_spec=pltpu.PrefetchScalarGridSpec(
            num_scalar_prefetch=1, grid=(n // tm,),
            in_specs=[pl.BlockSpec((tm, d), lambda i, idx: (i, 0)),
                      pl.BlockSpec((tm, n_e), lambda i, idx: (i, 0))] + [piece(q) for q in range(bpt)],
            out_specs=pl.BlockSpec((tm, d), lambda i, idx: (i, 0)),
            scratch_shapes=[pltpu.VMEM((mt, d), BF16)]),
        out_shape=jax.ShapeDtypeStruct((n, d), F32),
        compiler_params=_cparams("arbitrary"),
        name="moe_combine",
    )(dst_of_src, xm, gates, *([ys_exp] * bpt))


def _moe_sparse(h, xm, gates, wg, wu, wd, tf):
    n_e = gates.shape[1]
    tm = MOE_TILE
    mt = TOP_K * tm + n_e * MOE_ROW_ALIGN
    hs_tile, cnt = _moe_dispatch(h, gates, tm, mt)
    src_of_dst, dst_of_src, eid, n_valid, n_dst = _moe_tables(cnt[:, :, 0], mt, MOE_GROUP_ROWS)
    per_step = MOE_GROUP_ROWS // MOE_ROW_ALIGN
    ys_exp = _moe_group_ffn(hs_tile, src_of_dst, eid, n_valid, wg, wu, wd, n_dst // per_step,
                            MOE_GROUP_ROWS, tf)
    return _moe_combine(xm, gates, ys_exp, dst_of_src, tm, mt)


def _pick(n, candidates):
    for c in candidates:
        if n % c == 0:
            return c
    raise ValueError(f"no tile for {n}")


def _trunk(x, layers, cache_k, cache_v, state_pool, state_shift, state_wkv):
    prompt = cache_k is None
    b, t, d = x.shape
    n = b * t
    tm = _pick(n, (512, 256, 128))
    tm_ffn = _pick(n, (1024, 512, 256, 128))
    cq = CHUNK if prompt else t
    rq = _pick(t, (2 * CHUNK,)) if prompt else t
    n_part = _pick(t // rq, (4, 2, 1))
    pool_tb = _pick(t, (512, 256, 128, 64, 32))
    r_chunk = CHUNK if prompt else t
    r_tb = _pick(t, (512, 256, 128, 64, 32))
    x2 = x.reshape(n, d)
    nk, nv, npool, nshift, nwkv = [], [], [], [], []
    for l, lp in enumerate(layers):
        proj2 = _norm_matmul(x2, lp["norm1_g"], lp["w_in"], tm)
        proj = proj2.reshape(b, t, -1)
        c_pool = D_A + 2 * D_KV
        c_r = c_pool + D_POOL
        v_raw = proj[:, :, D_A + D_KV:c_pool]
        if prompt:
            cache_kv = None
            prefix = jnp.zeros((b, POOL_HALO, D_POOL), F32)
            prev = jnp.zeros((b, 1, D_R_IN), F32)
            st0 = jnp.zeros((b, HEAD_DIM, D_R), F32)
        else:
            cache_kv = jnp.concatenate([cache_k[l].reshape(b, WINDOW, D_KV),
                                        cache_v[l].reshape(b, WINDOW, D_KV)], axis=-1)
            prefix = jnp.pad(state_pool[l], ((0, 0), (POOL_HALO - POOL_CTX, 0), (0, 0)))
            prev = state_shift[l]
            st0 = _state_to_wide(state_wkv[l])
        ya, k_norm = _attention(proj, cache_kv, lp["q_gain"], lp["k_gain"], lp["sink"], cq, rq, n_part)
        yb = _pool_mix(proj, prefix, lp["pool_w"], lp["pool_scale"], 0 if prompt else POOL_CTX, pool_tb)
        yc, st_fin = _rwkv_mix(proj, prev, st0, lp, r_chunk, r_tb)
        keep = WINDOW if prompt else t
        nk.append(k_norm[:, t - keep:].reshape(b, keep, A_KV_HEADS, HEAD_DIM))
        nv.append(v_raw[:, t - keep:].reshape(b, keep, A_KV_HEADS, HEAD_DIM))
        npool.append(proj[:, t - POOL_CTX:, c_pool:c_r])
        nshift.append(proj[:, t - 1:, c_r:])
        nwkv.append(_wide_to_state(st_fin))
        outs = _mix_out(x2, ya.reshape(n, -1), yb.reshape(n, -1), yc.reshape(n, -1),
                        lp["w_out"], lp["norm2_g"], lp.get("router_w"), tm)
        if "router_w" in lp:
            xm, h2, gates = outs
            if n % MOE_TILE == 0 and n >= 2 * MOE_TILE:
                x2 = _moe_sparse(h2, xm, gates, lp["wg"], lp["wu"], lp["wd"], 512)
            else:
                x2 = _moe(h2, xm, gates, lp["wg"], lp["wu"], lp["wd"], tm_ffn, 512)
        else:
            xm, h2 = outs
            x2 = _ffn(h2, xm, lp["wg"], lp["wu"], lp["wd"], tm_ffn, 512)
    return (x2.reshape(b, t, d), jnp.stack(nk), jnp.stack(nv), jnp.stack(npool),
            jnp.stack(nshift), jnp.stack(nwkv))


def kernel(x_prompt, x_sample, cache_k, cache_v, state_pool, state_shift, state_wkv, norm1_g, w_in, q_gain, k_gain, attn_sink, pool_w, pool_scale, shift_mu, decay_w0, decay_w2, iclr_a0, iclr_a2, gate_g2, k_k, k_a, r_k, lnx_g, lnx_b, w_out, norm2_g, ffn_wg, ffn_wu, ffn_wd, router_w, moe_wg, moe_wu, moe_wd):
    depth = w_in.shape[0]
    r_w = decay_w2.shape[1]
    layers = []
    for l in range(depth):
        lp = dict(
            norm1_g=norm1_g[l][None], w_in=w_in[l].astype(BF16),
            q_gain=q_gain[l][None], k_gain=k_gain[l][None], sink=attn_sink[l],
            pool_w=jax.scipy.linalg.block_diag(*[pool_w[l, gi] for gi in range(len(POOL_WINDOWS))]).astype(BF16),
            pool_scale=pool_scale[l][None],
            mu=shift_mu[l][None], w0=decay_w0[l][None], a0=iclr_a0[l][None],
            w2=jnp.pad(decay_w2[l], ((0, R_WA - r_w), (0, 0))).astype(BF16),
            a2=jnp.pad(iclr_a2[l], ((r_w, 0), (0, 0))).astype(BF16),
            g2=gate_g2[l].astype(BF16),
            k_k=k_k[l][None], k_a=k_a[l][None], r_k=r_k[l].reshape(1, D_R),
            lnx_g=lnx_g[l][None], lnx_b=lnx_b[l][None],
            w_out=w_out[l].astype(BF16), norm2_g=norm2_g[l][None])
        if l % 2 == 0:
            lp.update(wg=ffn_wg[l // 2].astype(BF16), wu=ffn_wu[l // 2].astype(BF16),
                      wd=ffn_wd[l // 2].astype(BF16))
        else:
            lp.update(router_w=router_w[l // 2], wg=moe_wg[l // 2].astype(BF16),
                      wu=moe_wu[l // 2].astype(BF16), wd=moe_wd[l // 2].astype(BF16))
        layers.append(lp)
    y_p, pk, pv, ppool, pshift, pwkv = _trunk(x_prompt, layers, None, None, None, None, None)
    y_s, sk, sv, spool, sshift, swkv = _trunk(x_sample, layers, cache_k, cache_v, state_pool,
                                              state_shift, state_wkv)
    return (y_p, y_s, pk, pv, ppool, pshift, pwkv, sk, sv, spool, sshift, swkv)
```

```python
import functools

import jax
import jax.numpy as jnp
from jax import lax
from jax.experimental import pallas as pl
from jax.experimental.pallas import tpu as pltpu

F32 = jnp.float32
BF16 = jnp.bfloat16

HEAD_DIM = 64
A_HEADS = 8
A_KV_HEADS = 2
A_GROUP = A_HEADS // A_KV_HEADS
D_A = A_HEADS * HEAD_DIM
D_KV = A_KV_HEADS * HEAD_DIM
WINDOW = 128
CHUNK = 64
POOL_WINDOWS = (2, 4, 8, 16)
POOL_CTX = 15
POOL_HALO = 16
D_POOL = 256
POOL_GW = D_POOL // len(POOL_WINDOWS)
D_R = 256
R_HEADS = D_R // HEAD_DIM
R_WA = 128
R_G = 128
D_R_IN = 3 * D_R + R_WA + R_G
TOP_K = 2
NORM_EPS = 1e-6
GN_EPS = 64e-5
NEG_INF = -1e30
VMEM_LIMIT_BYTES = 56 * 1024 * 1024


def _cparams(*sem):
    return pltpu.CompilerParams(dimension_semantics=sem, vmem_limit_bytes=VMEM_LIMIT_BYTES)


def _mm(a, b):
    return jnp.dot(a.astype(BF16), b.astype(BF16), preferred_element_type=F32)


def _mm_nt(a, b):
    return lax.dot_general(a.astype(BF16), b.astype(BF16), (((1,), (1,)), ((), ())),
                           preferred_element_type=F32)


def _split2(x):
    hi = x.astype(BF16)
    lo = (x - hi.astype(F32)).astype(BF16)
    return hi, lo


def _split3(x):
    hi = x.astype(BF16)
    r1 = x - hi.astype(F32)
    mid = r1.astype(BF16)
    lo = (r1 - mid.astype(F32)).astype(BF16)
    return hi, mid, lo


def _mm_exact_rhs(x, b):
    bb = b.astype(BF16)
    hi, lo = _split2(x)
    dot = lambda u: jnp.dot(u, bb, preferred_element_type=F32)
    return dot(hi) + dot(lo)


def _mm_exact_lhs(a, x):
    ab = a.astype(BF16)
    hi, lo = _split2(x)
    dot = lambda u: jnp.dot(ab, u, preferred_element_type=F32)
    return dot(hi) + dot(lo)


def _mm_hi(a, b):
    ah, al = _split2(a)
    bh, bl = _split2(b)
    dot = lambda u, v: jnp.dot(u, v, preferred_element_type=F32)
    return dot(ah, bh) + (dot(ah, bl) + dot(al, bh))


def _sigmoid(x):
    return 1.0 / (1.0 + jnp.exp(-x))


def _silu(x):
    return x * _sigmoid(x)


def _norm_matmul_kernel(x_ref, g_ref, w_ref, o_ref):
    x = x_ref[...]
    h = x * lax.rsqrt(jnp.mean(x * x, axis=-1, keepdims=True) + NORM_EPS) * g_ref[...]
    o_ref[...] = jnp.dot(h.astype(BF16), w_ref[...], preferred_element_type=F32)


def _norm_matmul(x, g, w, tm):
    n, d = x.shape
    dout = w.shape[1]
    return pl.pallas_call(
        _norm_matmul_kernel,
        grid=(n // tm,),
        in_specs=[pl.BlockSpec((tm, d), lambda i: (i, 0)),
                  pl.BlockSpec((1, d), lambda i: (0, 0)),
                  pl.BlockSpec((d, dout), lambda i: (0, 0))],
        out_specs=pl.BlockSpec((tm, dout), lambda i: (i, 0)),
        out_shape=jax.ShapeDtypeStruct((n, dout), F32),
        compiler_params=_cparams("parallel"),
        name="norm_in_proj",
    )(x, g, w)


def _block_ones(n):
    r = lax.broadcasted_iota(jnp.int32, (n, n), 0) // HEAD_DIM
    c = lax.broadcasted_iota(jnp.int32, (n, n), 1) // HEAD_DIM
    return (r == c).astype(BF16)


def _head_rms_scale(z, ones):
    hi, lo = _split2(z * z)
    ss = jnp.dot(hi, ones, preferred_element_type=F32) + jnp.dot(lo, ones, preferred_element_type=F32)
    return lax.rsqrt(ss * (1.0 / HEAD_DIM) + NORM_EPS)


def _attn_kernel(sink_ref, q_ref, kv_ref, prev_ref, bias_ref, qg_ref, kg_ref, ya_ref, kn_ref,
                 *, rq, n_part, prev_is_raw):
    i = pl.program_id(1)
    kp = WINDOW + rq
    gw = A_GROUP * HEAD_DIM
    q = q_ref[0]
    kv = kv_ref[0]
    pv = prev_ref[0]
    ones_q = _block_ones(D_A)
    ones_k = _block_ones(D_KV)
    qn = (q * _head_rms_scale(q, ones_q) * qg_ref[...]).astype(BF16)
    k_cur = kv[:, 0:D_KV]
    k_cur = k_cur * _head_rms_scale(k_cur, ones_k) * kg_ref[...]
    kn_ref[0] = k_cur
    k_prev = pv[:, 0:D_KV]
    if prev_is_raw:
        k_prev = k_prev * _head_rms_scale(k_prev, ones_k) * kg_ref[...]
    k_all = jnp.concatenate([k_prev, k_cur], axis=0).astype(BF16)
    v_all = jnp.concatenate([pv[:, D_KV:], kv[:, D_KV:]], axis=0).astype(BF16)
    nq = A_GROUP * rq
    lane_in = lax.broadcasted_iota(jnp.int32, (D_KV, gw), 0)
    lane_out = lax.broadcasted_iota(jnp.int32, (D_KV, gw), 1)
    lane_out_t = lax.broadcasted_iota(jnp.int32, (gw, D_KV), 0)
    lane_in_t = lax.broadcasted_iota(jnp.int32, (gw, D_KV), 1)
    slot_lane = lax.broadcasted_iota(jnp.int32, (1, gw), 1) // HEAD_DIM
    slot_mask_bf = [(slot_lane == hh).astype(BF16) for hh in range(A_GROUP)]
    slot_row = lax.broadcasted_iota(jnp.int32, (gw, rq), 0) // HEAD_DIM
    key_row = lax.broadcasted_iota(jnp.int32, (kp, nq), 0)
    col_head = lax.broadcasted_iota(jnp.int32, (1, nq), 1) // rq
    for g in range(A_KV_HEADS):
        select = (lane_in // HEAD_DIM == g) & (lane_in % HEAD_DIM == lane_out % HEAD_DIM)
        k_wide = jnp.dot(k_all, select.astype(BF16), preferred_element_type=F32).astype(BF16)
        select_t = (lane_in_t // HEAD_DIM == g) & (lane_in_t % HEAD_DIM == lane_out_t % HEAD_DIM)
        v_wide_t = lax.dot_general(select_t.astype(BF16), v_all, (((1,), (1,)), ((), ())),
                                   preferred_element_type=F32).astype(BF16)
        q_g = qn[:, g * gw:(g + 1) * gw]
        sink_row = jnp.zeros((1, nq), F32)
        for hh in range(A_GROUP):
            sink_row = jnp.where(col_head == hh, sink_ref[g * A_GROUP + hh], sink_row)
        bias_t = bias_ref[g]
        for p in range(n_part):
            rows = slice(p * rq, (p + 1) * rq)
            qs = jnp.concatenate([q_g[rows] * m for m in slot_mask_bf], axis=0)
            kb = k_wide[p * rq:p * rq + kp]
            s = lax.dot_general(kb, qs, (((1,), (1,)), ((), ())), preferred_element_type=F32) + bias_t
            if prev_is_raw and p * rq < WINDOW:
                n_pad = WINDOW - (i * n_part + p) * rq
                s = s + jnp.where(key_row < n_pad, NEG_INF, 0.0)
            m = jnp.maximum(jnp.max(s, axis=0, keepdims=True), sink_row)
            e = jnp.exp(s - m)
            den = jnp.sum(e, axis=0, keepdims=True) + jnp.exp(sink_row - m)
            prob = (e * (1.0 / den)).astype(BF16)
            o_t = jnp.dot(v_wide_t[:, p * rq:p * rq + kp], prob, preferred_element_type=F32)
            out_t = o_t[:, (A_GROUP - 1) * rq:]
            for hh in range(A_GROUP - 2, -1, -1):
                out_t = jnp.where(slot_row == hh, o_t[:, hh * rq:(hh + 1) * rq], out_t)
            ya_ref[0, rows, g * gw:(g + 1) * gw] = out_t.T.astype(BF16)


def _attn_bias(rq, cq):
    kp = WINDOW + rq
    i = jnp.arange(rq)[:, None]
    j = jnp.arange(kp)[None, :]
    jb = j - cq * (i // cq)
    valid = (jb >= 0) & (jb < WINDOW + cq)
    dist = jnp.abs(WINDOW + (i % cq) - jb).astype(F32)
    slopes = jnp.exp2(-8.0 * jnp.arange(1, A_HEADS + 1, dtype=F32) / A_HEADS)
    bias = jnp.where(valid[None], -slopes[:, None, None] * dist[None], NEG_INF)
    return jnp.swapaxes(bias.reshape(A_KV_HEADS, A_GROUP * rq, kp), 1, 2)


def _attention(proj, cache_kv, q_gain, k_gain, sink, cq, rq, n_part):
    b, t, _ = proj.shape
    tq = rq * n_part
    prompt = cache_kv is None
    kv_col = D_A // (2 * D_KV)
    if prompt:
        prev_arr = proj
        prev_spec = pl.BlockSpec((1, WINDOW, 2 * D_KV),
                                 lambda bi, i: (bi, jnp.maximum(i * (tq // WINDOW) - 1, 0), kv_col))
    else:
        prev_arr = cache_kv
        prev_spec = pl.BlockSpec((1, WINDOW, 2 * D_KV), lambda bi, i: (bi, 0, 0))
    kp = WINDOW + rq
    q_gain_t = jnp.tile(q_gain * (HEAD_DIM ** -0.5), (1, A_HEADS))
    k_gain_t = jnp.tile(k_gain, (1, A_KV_HEADS))
    kern = functools.partial(_attn_kernel, rq=rq, n_part=n_part, prev_is_raw=prompt)
    return pl.pallas_call(
        kern,
        grid=(b, t // tq),
        in_specs=[pl.BlockSpec(memory_space=pltpu.SMEM),
                  pl.BlockSpec((1, tq, D_A), lambda bi, i: (bi, i, 0)),
                  pl.BlockSpec((1, tq, 2 * D_KV), lambda bi, i: (bi, i, kv_col)),
                  prev_spec,
                  pl.BlockSpec((A_KV_HEADS, kp, A_GROUP * rq), lambda bi, i: (0, 0, 0)),
                  pl.BlockSpec((1, D_A), lambda bi, i: (0, 0)),
                  pl.BlockSpec((1, D_KV), lambda bi, i: (0, 0))],
        out_specs=[pl.BlockSpec((1, tq, D_A), lambda bi, i: (bi, i, 0)),
                   pl.BlockSpec((1, tq, D_KV), lambda bi, i: (bi, i, 0))],
        out_shape=[jax.ShapeDtypeStruct((b, t, D_A), BF16),
                   jax.ShapeDtypeStruct((b, t, D_KV), F32)],
        compiler_params=_cparams("parallel", "arbitrary"),
        name="swa_attention",
    )(sink, proj, proj, prev_arr, _attn_bias(rq, cq), q_gain_t, k_gain_t)


def _pool_kernel(u_ref, halo_ref, prefix_ref, w_ref, scale_ref, yb_ref, *, n_prefix):
    i = pl.program_id(1)
    u = u_ref[0]
    tb = u.shape[0]
    halo = jnp.where(i == 0, prefix_ref[0], halo_ref[0])
    ext = jnp.concatenate([halo, u], axis=0)
    col = lax.broadcasted_iota(jnp.int32, (1, D_POOL), 1)
    pos = i * tb + lax.broadcasted_iota(jnp.int32, (tb, 1), 0)
    total = None
    count = None
    acc = ext
    span = 1
    for gi, w in enumerate(POOL_WINDOWS):
        while span < w:
            acc = acc + pltpu.roll(acc, span, axis=0)
            span *= 2
        in_group = (col >= gi * POOL_GW) & (col < (gi + 1) * POOL_GW)
        tail = acc[POOL_HALO:]
        total = jnp.where(in_group, tail, 0.0 if total is None else total)
        cnt = jnp.minimum(pos + (1 + n_prefix), w).astype(F32)
        count = jnp.where(in_group, cnt, 1.0 if count is None else count)
    d = total / count - u
    yb_ref[0] = (_mm(d, w_ref[...]) * scale_ref[...]).astype(BF16)


def _pool_mix(proj, prefix, w_blockdiag, scale, n_prefix, tb):
    b, t, _ = proj.shape
    col = (D_A + 2 * D_KV) // D_POOL
    kern = functools.partial(_pool_kernel, n_prefix=n_prefix)
    return pl.pallas_call(
        kern,
        grid=(b, t // tb),
        in_specs=[pl.BlockSpec((1, tb, D_POOL), lambda bi, i: (bi, i, col)),
                  pl.BlockSpec((1, POOL_HALO, D_POOL),
                               lambda bi, i: (bi, jnp.maximum(i * (tb // POOL_HALO) - 1, 0), col)),
                  pl.BlockSpec((1, POOL_HALO, D_POOL), lambda bi, i: (bi, 0, 0)),
                  pl.BlockSpec((D_POOL, D_POOL), lambda bi, i: (0, 0)),
                  pl.BlockSpec((1, D_POOL), lambda bi, i: (0, 0))],
        out_specs=pl.BlockSpec((1, tb, D_POOL), lambda bi, i: (bi, i, 0)),
        out_shape=jax.ShapeDtypeStruct((b, t, D_POOL), BF16),
        compiler_params=_cparams("parallel", "arbitrary"),
        name="pool_mix",
    )(proj, proj, prefix, w_blockdiag, scale)


def _rwkv_kernel(p_ref, prev_ref, st0_ref, mu_ref, w0_ref, a0_ref, w2_ref, a2_ref, g2_ref,
                 kk_ref, ka_ref, rk_ref, lng_ref, lnb_ref, yc_ref, st_ref, carry_ref, y_ref,
                 *, chunk):
    j = pl.program_id(1)

    @pl.when(j == 0)
    def _():
        carry_ref[...] = prev_ref[0]
        st_ref[0] = st0_ref[0]

    p = p_ref[0]
    tb = p.shape[0]
    n_chunk = tb // chunk
    hl = R_HEADS * chunk

    row = lax.broadcasted_iota(jnp.int32, (tb, 1), 0)
    p_prev = jnp.where(row == 0, carry_ref[...], pltpu.roll(p, 1, axis=0))
    carry_ref[...] = p[tb - 1:tb]
    xs = p + mu_ref[...] * (p_prev - p)
    r = xs[:, 0:D_R]
    k = xs[:, D_R:2 * D_R]
    v = xs[:, 2 * D_R:3 * D_R]
    wa = xs[:, 3 * D_R:3 * D_R + R_WA]
    gd = xs[:, 3 * D_R + R_WA:]

    z = -(w0_ref[...] + _mm(jnp.tanh(wa), w2_ref[...]))
    softplus = jnp.maximum(z, 0.0) + jnp.log(1.0 + jnp.exp(-jnp.abs(z)))
    lw = -jnp.exp(-softplus - 0.5)
    a = _sigmoid(a0_ref[...] + _mm(wa, a2_ref[...]))
    g = _mm(_sigmoid(gd), g2_ref[...])

    lane_r = lax.broadcasted_iota(jnp.int32, (D_R, D_R), 0) // HEAD_DIM
    lane_c = lax.broadcasted_iota(jnp.int32, (D_R, D_R), 1) // HEAD_DIM
    head_ones = (lane_r == lane_c).astype(F32)
    seg_sum = lambda t: _mm_exact_rhs(t, head_ones)

    kk = k * kk_ref[...]
    kk = kk / jnp.maximum(jnp.sqrt(seg_sum(kk * kk)), 1e-12)
    k2 = k * (1.0 + (a - 1.0) * ka_ref[...])
    bb = kk * a

    ti = lax.broadcasted_iota(jnp.int32, (tb, tb), 0)
    tj = lax.broadcasted_iota(jnp.int32, (tb, tb), 1)
    cum = _mm_exact_lhs(((ti // chunk == tj // chunk) & (tj <= ti)).astype(F32), lw)

    wi = lax.broadcasted_iota(jnp.int32, (chunk, hl), 0)
    wj = lax.broadcasted_iota(jnp.int32, (chunk, hl), 1) % chunk
    strict = wj < wi
    incl = wj <= wi
    eye_w = (wj == wi).astype(F32)
    diag_k = (lax.broadcasted_iota(jnp.int32, (HEAD_DIM, D_R), 0)
              == lax.broadcasted_iota(jnp.int32, (HEAD_DIM, D_R), 1) % HEAD_DIM)
    lane_k = lax.broadcasted_iota(jnp.int32, (1, D_R), 1) // HEAD_DIM
    lane_t = lax.broadcasted_iota(jnp.int32, (1, hl), 1) // chunk
    mask_k = [(lane_k == h).astype(F32) for h in range(R_HEADS)]
    mask_k_bf = [m.astype(BF16) for m in mask_k]
    mask_t_bf = [(lane_t == h).astype(BF16) for h in range(R_HEADS)]

    def blockdiag(t, masks):
        t16 = t.astype(BF16)
        return jnp.concatenate([t16 * m for m in masks], axis=0)

    def wide_transpose(t):
        tt = jnp.concatenate([t * m for m in mask_k], axis=0).T
        out = tt[0:HEAD_DIM]
        for h in range(1, R_HEADS):
            out = out + tt[h * HEAD_DIM:(h + 1) * HEAD_DIM]
        return out.astype(BF16)

    chunks = []
    for c in range(n_chunk):
        sl = slice(c * chunk, (c + 1) * chunk)
        cum_c = cum[sl]
        cum_last = cum_c[chunk - 1:chunk]
        g_in = jnp.exp(cum_c)
        g_prev = jnp.exp(cum_c - lw[sl])
        g_inv = jnp.exp(-cum_c)
        g_out = jnp.exp(cum_last - cum_c)
        a_n = (kk[sl] * g_prev).astype(BF16)
        r_n = r[sl] * g_in
        ch = dict(sl=sl, r_n=r_n, g_last=jnp.exp(cum_last),
                  a_s=blockdiag(a_n, mask_k_bf),
                  v_s=blockdiag(v[sl], mask_k_bf),
                  bo_w=wide_transpose(bb[sl] * g_out),
                  ko_w=wide_transpose(k2[sl] * g_out))
        ar = jnp.concatenate([a_n, r_n.astype(BF16)], axis=0)
        m_b = _mm_nt(ar, blockdiag(bb[sl] * g_inv, mask_k_bf))
        m_k = _mm_nt(ar, blockdiag(k2[sl] * g_inv, mask_k_bf))
        m_ab = jnp.where(strict, m_b[:chunk], 0.0)
        ch.update(m_rb=jnp.where(incl, m_b[chunk:], 0.0).astype(BF16),
                  m_ak=jnp.where(strict, m_k[:chunk], 0.0).astype(BF16),
                  m_rk=jnp.where(incl, m_k[chunk:], 0.0).astype(BF16),
                  t_inv=eye_w - m_ab, pw=m_ab.astype(BF16))
        chunks.append(ch)

    for ch in chunks:
        ch["pw"] = _mm(ch["pw"], blockdiag(ch["pw"], mask_t_bf)).astype(BF16)
    n = 2
    while n < chunk:
        for ch in chunks:
            rhs = blockdiag(ch["pw"], mask_t_bf)
            if 2 * n < chunk:
                both = _mm(jnp.concatenate([ch["pw"], ch["t_inv"].astype(BF16)], axis=0), rhs)
                ch["pw"] = both[:chunk].astype(BF16)
                ch["t_inv"] = ch["t_inv"] + both[chunk:]
            else:
                ch["t_inv"] = ch["t_inv"] + _mm(ch["t_inv"], rhs)
        n *= 2

    for ch in chunks:
        t_inv = ch["t_inv"].astype(BF16)
        ch["a_bar"] = blockdiag(_mm(t_inv, ch["a_s"]), mask_k_bf)
        on_v = _mm(jnp.concatenate([ch["m_ak"], ch["m_rk"], ch["ko_w"]], axis=0), ch["v_s"])
        ch["mrk_v"], ch["ko_v"] = on_v[chunk:2 * chunk], on_v[2 * chunk:]
        ch["u0"] = blockdiag(-_mm(t_inv, blockdiag(on_v[:chunk], mask_k_bf)), mask_k_bf)
    for ch in chunks:
        lhs = jnp.concatenate([ch["m_rb"], ch["bo_w"]], axis=0)
        on_a = _mm(lhs, ch["a_bar"])
        on_u = _mm(lhs, ch["u0"])
        ch["r_bar"] = (ch["r_n"] - on_a[:chunk]).astype(BF16)
        ch["y0"] = ch["mrk_v"] + on_u[:chunk]
        ch["g_w"] = (jnp.where(diag_k, ch["g_last"], 0.0) - on_a[chunk:]).astype(BF16)
        ch["h_w"] = on_u[chunk:] + ch["ko_v"]

    st_w = st_ref[0]
    for ch in chunks:
        on_st = _mm(jnp.concatenate([ch["r_bar"], ch["g_w"]], axis=0), blockdiag(st_w, mask_k_bf))
        y_ref[ch["sl"], :] = ch["y0"] + on_st[:chunk]
        st_w = on_st[chunk:] + ch["h_w"]
    st_ref[0] = st_w

    y = y_ref[...]
    mean = seg_sum(y) * (1.0 / HEAD_DIM)
    d = y - mean
    var = seg_sum(d * d) * (1.0 / HEAD_DIM)
    yn = d * lax.rsqrt(var + GN_EPS) * lng_ref[...] + lnb_ref[...]
    bonus = seg_sum(r * k2 * rk_ref[...]) * v
    yc_ref[0] = ((yn + bonus) * g).astype(BF16)


def _rwkv_mix(proj, prev, st0, lp, chunk, tb):
    b, t, _ = proj.shape
    col = (D_A + 2 * D_KV + D_POOL) // D_R_IN
    row = lambda n: pl.BlockSpec((1, n), lambda bi, i: (0, 0))
    full = lambda s: pl.BlockSpec(s, lambda bi, i: (0,) * len(s))
    kern = functools.partial(_rwkv_kernel, chunk=chunk)
    return pl.pallas_call(
        kern,
        grid=(b, t // tb),
        in_specs=[pl.BlockSpec((1, tb, D_R_IN), lambda bi, i: (bi, i, col)),
                  pl.BlockSpec((1, 1, D_R_IN), lambda bi, i: (bi, 0, 0)),
                  pl.BlockSpec((1, HEAD_DIM, D_R), lambda bi, i: (bi, 0, 0)),
                  row(D_R_IN), row(D_R), row(D_R),
                  full((R_WA, D_R)), full((R_WA, D_R)), full((R_G, D_R)),
                  row(D_R), row(D_R), row(D_R), row(D_R), row(D_R)],
        out_specs=[pl.BlockSpec((1, tb, D_R), lambda bi, i: (bi, i, 0)),
                   pl.BlockSpec((1, HEAD_DIM, D_R), lambda bi, i: (bi, 0, 0))],
        out_shape=[jax.ShapeDtypeStruct((b, t, D_R), BF16),
                   jax.ShapeDtypeStruct((b, HEAD_DIM, D_R), F32)],
        scratch_shapes=[pltpu.VMEM((1, D_R_IN), F32), pltpu.VMEM((tb, D_R), F32)],
        compiler_params=_cparams("parallel", "arbitrary"),
        name="rwkv7_mix",
    )(proj, prev, st0, lp["mu"], lp["w0"], lp["a0"], lp["w2"], lp["a2"], lp["g2"],
      lp["k_k"], lp["k_a"], lp["r_k"], lp["lnx_g"], lp["lnx_b"])


def _state_to_wide(s):
    b = s.shape[0]
    return jnp.transpose(s, (0, 3, 1, 2)).reshape(b, HEAD_DIM, D_R)


def _wide_to_state(st):
    b = st.shape[0]
    return jnp.transpose(st.reshape(b, HEAD_DIM, R_HEADS, HEAD_DIM), (0, 2, 3, 1))


def _mix_out_kernel(x_ref, ya_ref, yb_ref, yc_ref, w_ref, g_ref, *rest, route):
    if route:
        rw_ref, xm_ref, h_ref, gate_ref = rest
    else:
        xm_ref, h_ref = rest
    dot = lambda u, lo, hi: jnp.dot(u[...], w_ref[lo:hi, :], preferred_element_type=F32)
    xm = (x_ref[...] + dot(ya_ref, 0, D_A) + dot(yb_ref, D_A, D_A + D_POOL)
          + dot(yc_ref, D_A + D_POOL, D_A + D_POOL + D_R))
    xm_ref[...] = xm
    h = xm * lax.rsqrt(jnp.mean(xm * xm, axis=-1, keepdims=True) + NORM_EPS) * g_ref[...]
    h_hi = h.astype(BF16)
    h_ref[...] = h_hi
    if route:
        rw_t = rw_ref[...]
        n_e = rw_t.shape[0]
        rw_hi = rw_t.astype(BF16).astype(F32)
        rw2 = jnp.concatenate([rw_hi, rw_t - rw_hi], axis=0).astype(BF16)
        h_lo = (h - h_hi.astype(F32)).astype(BF16)
        nt = lambda a, b: lax.dot_general(a, b, (((1,), (1,)), ((), ())), preferred_element_type=F32)
        on_hi = nt(rw2, h_hi)
        logits = on_hi[:n_e] + (on_hi[n_e:] + nt(rw2, h_lo)[:n_e])
        row = lax.broadcasted_iota(jnp.int32, logits.shape, 0).astype(F32)
        m1 = jnp.max(logits, axis=0, keepdims=True)
        i1 = jnp.min(jnp.where(logits == m1, row, float(n_e)), axis=0, keepdims=True)
        rest_l = jnp.where(row == i1, -jnp.inf, logits)
        m2 = jnp.max(rest_l, axis=0, keepdims=True)
        i2 = jnp.min(jnp.where(rest_l == m2, row, float(n_e)), axis=0, keepdims=True)
        e2 = jnp.exp(m2 - m1)
        gates_t = (jnp.where(row == i1, 1.0 / (1.0 + e2), 0.0)
                   + jnp.where(row == i2, e2 / (1.0 + e2), 0.0))
        gate_ref[...] = gates_t.T


def _mix_out(x, ya, yb, yc, w_out, g, router_w, tm):
    n, d = x.shape
    route = router_w is not None
    tile = lambda w: pl.BlockSpec((tm, w), lambda i: (i, 0))
    in_specs = [tile(d), tile(D_A), tile(D_POOL), tile(D_R),
                pl.BlockSpec(w_out.shape, lambda i: (0, 0)),
                pl.BlockSpec((1, d), lambda i: (0, 0))]
    out_specs = [tile(d), tile(d)]
    out_shape = [jax.ShapeDtypeStruct((n, d), F32), jax.ShapeDtypeStruct((n, d), BF16)]
    args = [x, ya, yb, yc, w_out, g]
    if route:
        n_e = router_w.shape[1]
        in_specs.append(pl.BlockSpec((n_e, d), lambda i: (0, 0)))
        out_specs.append(tile(n_e))
        out_shape.append(jax.ShapeDtypeStruct((n, n_e), F32))
        args.append(router_w.T)
    return pl.pallas_call(
        functools.partial(_mix_out_kernel, route=route),
        grid=(n // tm,),
        in_specs=in_specs, out_specs=out_specs, out_shape=out_shape,
        compiler_params=_cparams("parallel"),
        name="mix_out_proj",
    )(*args)


def _swiglu_act(h, wg, wu):
    return (_silu(jnp.dot(h, wg, preferred_element_type=F32))
            * jnp.dot(h, wu, preferred_element_type=F32)).astype(BF16)


def _down_proj(act_ref, wd, tf):
    out = None
    for jj in range(act_ref.shape[0]):
        part = jnp.dot(act_ref[jj], wd[jj * tf:(jj + 1) * tf, :], preferred_element_type=F32)
        out = part if out is None else out + part
    return out


def _ffn_kernel(h_ref, xm_ref, wg_ref, wu_ref, wd_ref, o_ref, act_ref):
    j = pl.program_id(1)
    tf = wg_ref.shape[1]
    act_ref[j] = _swiglu_act(h_ref[...], wg_ref[...], wu_ref[...])

    @pl.when(j == pl.num_programs(1) - 1)
    def _():
        o_ref[...] = xm_ref[...] + _down_proj(act_ref, wd_ref, tf)


def _ffn(h, xm, wg, wu, wd, tm, tf):
    n, d = xm.shape
    f = wg.shape[1]
    return pl.pallas_call(
        _ffn_kernel,
        grid=(n // tm, f // tf),
        in_specs=[pl.BlockSpec((tm, d), lambda i, j: (i, 0)),
                  pl.BlockSpec((tm, d), lambda i, j: (i, 0)),
                  pl.BlockSpec((d, tf), lambda i, j: (0, j)),
                  pl.BlockSpec((d, tf), lambda i, j: (0, j)),
                  pl.BlockSpec((f, d), lambda i, j: (0, 0))],
        out_specs=pl.BlockSpec((tm, d), lambda i, j: (i, 0)),
        out_shape=jax.ShapeDtypeStruct((n, d), F32),
        scratch_shapes=[pltpu.VMEM((f // tf, tm, tf), BF16)],
        compiler_params=_cparams("parallel", "arbitrary"),
        name="swiglu_ffn",
    )(h, xm, wg, wu, wd)


def _moe_kernel(h_ref, xm_ref, gate_ref, wg_ref, wu_ref, wd_ref, o_ref, acc_ref):
    e = pl.program_id(1)
    j = pl.program_id(2)

    @pl.when((e == 0) & (j == 0))
    def _():
        acc_ref[...] = xm_ref[...]

    gates = gate_ref[...]
    lane = lax.broadcasted_iota(jnp.int32, gates.shape, 1)
    gate = jnp.sum(jnp.where(lane == e, gates, 0.0), axis=-1, keepdims=True)
    h = h_ref[...]
    act = (_silu(jnp.dot(h, wg_ref[0], preferred_element_type=F32))
           * jnp.dot(h, wu_ref[0], preferred_element_type=F32))
    acc_ref[...] += gate * jnp.dot(act.astype(BF16), wd_ref[0], preferred_element_type=F32)

    @pl.when((e == pl.num_programs(1) - 1) & (j == pl.num_programs(2) - 1))
    def _():
        o_ref[...] = acc_ref[...]


def _moe(h, xm, gates, wg, wu, wd, tm, tf):
    n, d = xm.shape
    n_e, _, f = wg.shape
    return pl.pallas_call(
        _moe_kernel,
        grid=(n // tm, n_e, f // tf),
        in_specs=[pl.BlockSpec((tm, d), lambda i, e, j: (i, 0)),
                  pl.BlockSpec((tm, d), lambda i, e, j: (i, 0)),
                  pl.BlockSpec((tm, n_e), lambda i, e, j: (i, 0)),
                  pl.BlockSpec((1, d, tf), lambda i, e, j: (e, 0, j)),
                  pl.BlockSpec((1, d, tf), lambda i, e, j: (e, 0, j)),
                  pl.BlockSpec((1, tf, d), lambda i, e, j: (e, j, 0))],
        out_specs=pl.BlockSpec((tm, d), lambda i, e, j: (i, 0)),
        out_shape=jax.ShapeDtypeStruct((n, d), F32),
        scratch_shapes=[pltpu.VMEM((tm, d), F32)],
        compiler_params=_cparams("parallel", "arbitrary", "arbitrary"),
        name="moe_ffn",
    )(h, xm, gates, wg, wu, wd)


MOE_ROW_ALIGN = 32
MOE_GATE_LANES = 128
MOE_TILE = 1024
MOE_GROUP_ROWS = 1024
_UNSELECTED = 1e9


def _tile_routing(gates_sel, rank, axis):
    n_e = rank.shape[axis]
    cnt = jnp.max(rank, axis=1 - axis, keepdims=True)
    padded = jnp.floor((cnt + (MOE_ROW_ALIGN - 1)) * (1.0 / MOE_ROW_ALIGN)) * MOE_ROW_ALIGN
    offs, run = [], jnp.zeros((1, 1), F32)
    for e in range(n_e):
        offs.append(run)
        run = run + (padded[e:e + 1] if axis == 0 else padded[:, e:e + 1])
    off = jnp.concatenate(offs, axis=axis)
    dest = jnp.where(gates_sel, off + rank - 1.0, -1.0)
    d_hi = jnp.max(dest, axis=axis, keepdims=True)
    d_lo = jnp.min(jnp.where(gates_sel, dest, _UNSELECTED), axis=axis, keepdims=True)
    d_lo = jnp.where(d_lo == d_hi, -2.0, d_lo)
    return cnt, dest, d_hi, d_lo


def _moe_dispatch_kernel(h_ref, gate_ref, hs_ref, cnt_ref, *, chunk_rows):
    h = h_ref[...]
    gates = gate_ref[...]
    tm, n_e = gates.shape
    d = h.shape[1]
    mt = hs_ref.shape[0]
    eye = (lax.broadcasted_iota(jnp.int32, (n_e, n_e), 0)
           == lax.broadcasted_iota(jnp.int32, (n_e, n_e), 1)).astype(BF16)
    to_rows = lambda u: lax.dot_general(eye, u, (((1,), (1,)), ((), ())), preferred_element_type=F32)
    g_hi3, g_mid3, g_lo3 = _split3(gates)
    gate_row = to_rows(g_hi3) + to_rows(g_mid3) + to_rows(g_lo3)
    sel_row = gate_row > 0.0
    ti = lax.broadcasted_iota(jnp.int32, (tm, tm), 0)
    tj = lax.broadcasted_iota(jnp.int32, (tm, tm), 1)
    rank = jnp.dot(sel_row.astype(BF16), (ti <= tj).astype(BF16), preferred_element_type=F32)
    cnt, dest, d_hi, d_lo = _tile_routing(sel_row, rank, 0)
    cnt_ref[0] = jnp.broadcast_to(cnt, cnt_ref.shape[1:])
    g_hi = jnp.sum(jnp.where(dest == d_hi, gate_row, 0.0), axis=0, keepdims=True)
    g_lo = jnp.sum(jnp.where(dest == d_lo, gate_row, 0.0), axis=0, keepdims=True)
    lane = lax.broadcasted_iota(jnp.int32, (chunk_rows, MOE_GATE_LANES), 1)
    for c0 in range(0, mt, chunk_rows):
        rho = (c0 + lax.broadcasted_iota(jnp.int32, (chunk_rows, 1), 0)).astype(F32)
        is_hi = d_hi == rho
        is_lo = d_lo == rho
        onehot = jnp.where(is_hi, 1.0, jnp.where(is_lo, 1.0, 0.0)).astype(BF16)
        rows = jnp.dot(onehot, h, preferred_element_type=F32)
        g = jnp.sum(jnp.where(is_hi, g_hi, jnp.where(is_lo, g_lo, 0.0)), axis=1, keepdims=True)
        p0, p1, p2 = [t.astype(F32) for t in _split3(g)]
        gcols = jnp.where(lane == 0, p0, jnp.where(lane == 1, p1, jnp.where(lane == 2, p2, 0.0)))
        hs_ref[c0:c0 + chunk_rows, 0:d] = rows.astype(BF16)
        hs_ref[c0:c0 + chunk_rows, d:d + MOE_GATE_LANES] = gcols.astype(BF16)


def _moe_dispatch(h, gates, tm, mt):
    n, d = h.shape
    n_e = gates.shape[1]
    n_tiles = n // tm
    return pl.pallas_call(
        functools.partial(_moe_dispatch_kernel, chunk_rows=256),
        grid=(n_tiles,),
        in_specs=[pl.BlockSpec((tm, d), lambda i: (i, 0)),
                  pl.BlockSpec((tm, n_e), lambda i: (i, 0))],
        out_specs=[pl.BlockSpec((mt, d + MOE_GATE_LANES), lambda i: (i, 0)),
                   pl.BlockSpec((1, n_e, 128), lambda i: (i, 0, 0))],
        out_shape=[jax.ShapeDtypeStruct((n_tiles * mt, d + MOE_GATE_LANES), BF16),
                   jax.ShapeDtypeStruct((n_tiles, n_e, 128), F32)],
        compiler_params=_cparams("parallel"),
        name="moe_dispatch",
    )(h, gates)


def _moe_tables(cnt, mt, group_rows):
    i32 = jnp.int32
    n_tiles, n_e = cnt.shape
    bpt = mt // MOE_ROW_ALIGN
    g = group_rows // MOE_ROW_ALIGN
    nblk = jnp.ceil(cnt / MOE_ROW_ALIGN).astype(i32)
    cum = jnp.cumsum(nblk, axis=1)
    off_blk = cum - nblk
    src_base = jnp.arange(n_tiles, dtype=i32)[:, None] * bpt + off_blk
    tot = jnp.sum(nblk, axis=0)
    totp = ((tot + g - 1) // g) * g
    es = jnp.cumsum(totp) - totp
    nblk_t = nblk.T
    seg_start = es[:, None] + jnp.cumsum(nblk_t, axis=1) - nblk_t
    n_src = n_tiles * bpt
    n_dst = -(-(n_src + n_e * (g - 1)) // g) * g
    dblk = jnp.arange(n_dst, dtype=i32)[:, None]
    s0, sl, ss = seg_start.reshape(1, -1), nblk_t.reshape(1, -1), src_base.T.reshape(1, -1)
    hit = (dblk >= s0) & (dblk < s0 + sl)
    src_of_dst = jnp.sum(jnp.where(hit, ss + dblk - s0, 0), axis=1)
    n_rb = n_dst // g
    eid = jnp.sum(jnp.arange(n_rb, dtype=i32)[:, None] * g >= es[None, :], axis=1).astype(i32) - 1
    n_valid = ((es[-1] + totp[-1]) // g).reshape(1)
    o_of = jnp.arange(bpt, dtype=i32)[None, :, None]
    off3, len3 = off_blk[:, None, :], nblk[:, None, :]
    hit3 = (o_of >= off3) & (o_of < off3 + len3)
    dst_of_src = jnp.sum(jnp.where(hit3, seg_start.T[:, None, :] + o_of - off3, 0), axis=2).reshape(-1)
    return src_of_dst.astype(i32), dst_of_src.astype(i32), eid, n_valid.astype(i32), n_dst


def _moe_group_kernel(idx_ref, eid_ref, nv_ref, *refs, n_src):
    del idx_ref, eid_ref
    src_refs = refs[:n_src]
    wg_ref, wu_ref, wd_ref, ys_ref, hs_ref, act_ref = refs[n_src:]
    i = pl.program_id(0)
    j = pl.program_id(1)
    d = ys_ref.shape[1]
    tf = wg_ref.shape[2]
    rows = src_refs[0].shape[0]

    @pl.when(i < nv_ref[0])
    def _():
        @pl.when(j == 0)
        def _():
            for q, src_ref in enumerate(src_refs):
                hs_ref[q * rows:(q + 1) * rows, :] = src_ref[...]

        act_ref[j] = _swiglu_act(hs_ref[:, 0:d], wg_ref[0], wu_ref[0])

        @pl.when(j == pl.num_programs(1) - 1)
        def _():
            gp = hs_ref[:, d:d + MOE_GATE_LANES].astype(F32)
            gate = gp[:, 0:1] + gp[:, 1:2] + gp[:, 2:3]
            ys_ref[...] = (_down_proj(act_ref, wd_ref[0], tf) * gate).astype(BF16)

    @pl.when((i >= nv_ref[0]) & (j == 0))
    def _():
        ys_ref[...] = jnp.zeros(ys_ref.shape, BF16)


def _moe_group_ffn(hs, src_of_dst, eid, n_valid, wg, wu, wd, n_rb, rb, tf):
    dw = hs.shape[1]
    d = dw - MOE_GATE_LANES
    f = wg.shape[2]
    nj = f // tf
    n_src = rb // MOE_ROW_ALIGN
    live_i = lambda i, nv: jnp.minimum(i, nv[0] - 1)
    live_j = lambda i, j, nv: jnp.where(i < nv[0], j, nj - 1)
    piece = lambda q: pl.BlockSpec((MOE_ROW_ALIGN, dw),
                                   lambda i, j, idx, eid, nv: (idx[live_i(i, nv) * n_src + q], 0))
    return pl.pallas_call(
        functools.partial(_moe_group_kernel, n_src=n_src),
        grid_spec=pltpu.PrefetchScalarGridSpec(
            num_scalar_prefetch=3, grid=(n_rb, nj),
            in_specs=[piece(q) for q in range(n_src)] + [
                pl.BlockSpec((1, d, tf), lambda i, j, idx, eid, nv: (eid[i], 0, live_j(i, j, nv))),
                pl.BlockSpec((1, d, tf), lambda i, j, idx, eid, nv: (eid[i], 0, live_j(i, j, nv))),
                pl.BlockSpec((1, f, d), lambda i, j, idx, eid, nv: (eid[i], 0, 0))],
            out_specs=pl.BlockSpec((rb, d), lambda i, j, idx, eid, nv: (i, 0)),
            scratch_shapes=[pltpu.VMEM((rb, dw), BF16), pltpu.VMEM((nj, rb, tf), BF16)]),
        out_shape=jax.ShapeDtypeStruct((n_rb * rb, d), BF16),
        compiler_params=_cparams("arbitrary", "arbitrary"),
        name="moe_group_ffn",
    )(src_of_dst, eid, n_valid, *([hs] * n_src), wg, wu, wd)


def _moe_combine_kernel(idx_ref, xm_ref, gate_ref, *refs):
    del idx_ref
    o_ref, ys_ref = refs[-2:]
    rows = refs[0].shape[0]
    for q, src_ref in enumerate(refs[:-2]):
        ys_ref[q * rows:(q + 1) * rows, :] = src_ref[...]
    gates = gate_ref[...]
    tm, n_e = gates.shape
    mt = ys_ref.shape[0]
    sel = gates > 0.0
    ti = lax.broadcasted_iota(jnp.int32, (tm, tm), 0)
    tj = lax.broadcasted_iota(jnp.int32, (tm, tm), 1)
    rank = jnp.dot((tj <= ti).astype(BF16), sel.astype(BF16), preferred_element_type=F32)
    _, _, d_hi, d_lo = _tile_routing(sel, rank, 1)
    rho = lax.broadcasted_iota(jnp.int32, (1, mt), 1).astype(F32)
    onehot = jnp.where(d_hi == rho, 1.0, jnp.where(d_lo == rho, 1.0, 0.0)).astype(BF16)
    o_ref[...] = xm_ref[...] + jnp.dot(onehot, ys_ref[...], preferred_element_type=F32)


def _moe_combine(xm, gates, ys_exp, dst_of_src, tm, mt):
    n, d = xm.shape
    n_e = gates.shape[1]
    bpt = mt // MOE_ROW_ALIGN
    piece = lambda q: pl.BlockSpec((MOE_ROW_ALIGN, d), lambda i, idx: (idx[i * bpt + q], 0))
    return pl.pallas_call(
        _moe_combine_kernel,
        grid_spec=pltpu.PrefetchScalarGridSpec(
            num_scalar_prefetch=1, grid=(n // tm,),
            in_specs=[pl.BlockSpec((tm, d), lambda i, idx: (i, 0)),
                      pl.BlockSpec((tm, n_e), lambda i, idx: (i, 0))] + [piece(q) for q in range(bpt)],
            out_specs=pl.BlockSpec((tm, d), lambda i, idx: (i, 0)),
            scratch_shapes=[pltpu.VMEM((mt, d), BF16)]),
        out_shape=jax.ShapeDtypeStruct((n, d), F32),
        compiler_params=_cparams("arbitrary"),
        name="moe_combine",
    )(dst_of_src, xm, gates, *([ys_exp] * bpt))


def _moe_sparse(h, xm, gates, wg, wu, wd, tf):
    n_e = gates.shape[1]
    tm = MOE_TILE
    mt = TOP_K * tm + n_e * MOE_ROW_ALIGN
    hs_tile, cnt = _moe_dispatch(h, gates, tm, mt)
    src_of_dst, dst_of_src, eid, n_valid, n_dst = _moe_tables(cnt[:, :, 0], mt, MOE_GROUP_ROWS)
    per_step = MOE_GROUP_ROWS // MOE_ROW_ALIGN
    ys_exp = _moe_group_ffn(hs_tile, src_of_dst, eid, n_valid, wg, wu, wd, n_dst // per_step,
                            MOE_GROUP_ROWS, tf)
    return _moe_combine(xm, gates, ys_exp, dst_of_src, tm, mt)


def _pick(n, candidates):
    for c in candidates:
        if n % c == 0:
            return c
    raise ValueError(f"no tile for {n}")


def _trunk(x, layers, cache_k, cache_v, state_pool, state_shift, state_wkv):
    prompt = cache_k is None
    b, t, d = x.shape
    n = b * t
    tm = _pick(n, (512, 256, 128))
    tm_ffn = _pick(n, (1024, 512, 256, 128))
    cq = CHUNK if prompt else t
    rq = _pick(t, (2 * CHUNK,)) if prompt else t
    n_part = _pick(t // rq, (4, 2, 1))
    pool_tb = _pick(t, (512, 256, 128, 64, 32))
    r_chunk = CHUNK if prompt else t
    r_tb = _pick(t, (512, 256, 128, 64, 32))
    x2 = x.reshape(n, d)
    nk, nv, npool, nshift, nwkv = [], [], [], [], []
    for l, lp in enumerate(layers):
        proj2 = _norm_matmul(x2, lp["norm1_g"], lp["w_in"], tm)
        proj = proj2.reshape(b, t, -1)
        c_pool = D_A + 2 * D_KV
        c_r = c_pool + D_POOL
        v_raw = proj[:, :, D_A + D_KV:c_pool]
        if prompt:
            cache_kv = None
            prefix = jnp.zeros((b, POOL_HALO, D_POOL), F32)
            prev = jnp.zeros((b, 1, D_R_IN), F32)
            st0 = jnp.zeros((b, HEAD_DIM, D_R), F32)
        else:
            cache_kv = jnp.concatenate([cache_k[l].reshape(b, WINDOW, D_KV),
                                        cache_v[l].reshape(b, WINDOW, D_KV)], axis=-1)
            prefix = jnp.pad(state_pool[l], ((0, 0), (POOL_HALO - POOL_CTX, 0), (0, 0)))
            prev = state_shift[l]
            st0 = _state_to_wide(state_wkv[l])
        ya, k_norm = _attention(proj, cache_kv, lp["q_gain"], lp["k_gain"], lp["sink"], cq, rq, n_part)
        yb = _pool_mix(proj, prefix, lp["pool_w"], lp["pool_scale"], 0 if prompt else POOL_CTX, pool_tb)
        yc, st_fin = _rwkv_mix(proj, prev, st0, lp, r_chunk, r_tb)
        keep = WINDOW if prompt else t
        nk.append(k_norm[:, t - keep:].reshape(b, keep, A_KV_HEADS, HEAD_DIM))
        nv.append(v_raw[:, t - keep:].reshape(b, keep, A_KV_HEADS, HEAD_DIM))
        npool.append(proj[:, t - POOL_CTX:, c_pool:c_r])
        nshift.append(proj[:, t - 1:, c_r:])
        nwkv.append(_wide_to_state(st_fin))
        outs = _mix_out(x2, ya.reshape(n, -1), yb.reshape(n, -1), yc.reshape(n, -1),
                        lp["w_out"], lp["norm2_g"], lp.get("router_w"), tm)
        if "router_w" in lp:
            xm, h2, gates = outs
            if n % MOE_TILE == 0 and n >= 2 * MOE_TILE:
                x2 = _moe_sparse(h2, xm, gates, lp["wg"], lp["wu"], lp["wd"], 512)
            else:
                x2 = _moe(h2, xm, gates, lp["wg"], lp["wu"], lp["wd"], tm_ffn, 512)
        else:
            xm, h2 = outs
            x2 = _ffn(h2, xm, lp["wg"], lp["wu"], lp["wd"], tm_ffn, 512)
    return (x2.reshape(b, t, d), jnp.stack(nk), jnp.stack(nv), jnp.stack(npool),
            jnp.stack(nshift), jnp.stack(nwkv))


def kernel(x_prompt, x_sample, cache_k, cache_v, state_pool, state_shift, state_wkv, norm1_g, w_in, q_gain, k_gain, attn_sink, pool_w, pool_scale, shift_mu, decay_w0, decay_w2, iclr_a0, iclr_a2, gate_g2, k_k, k_a, r_k, lnx_g, lnx_b, w_out, norm2_g, ffn_wg, ffn_wu, ffn_wd, router_w, moe_wg, moe_wu, moe_wd):
    depth = w_in.shape[0]
    r_w = decay_w2.shape[1]
    layers = []
    for l in range(depth):
        lp = dict(
            norm1_g=norm1_g[l][None], w_in=w_in[l].astype(BF16),
            q_gain=q_gain[l][None], k_gain=k_gain[l][None], sink=attn_sink[l],
            pool_w=jax.scipy.linalg.block_diag(*[pool_w[l, gi] for gi in range(len(POOL_WINDOWS))]).astype(BF16),
            pool_scale=pool_scale[l][None],
            mu=shift_mu[l][None], w0=decay_w0[l][None], a0=iclr_a0[l][None],
            w2=jnp.pad(decay_w2[l], ((0, R_WA - r_w), (0, 0))).astype(BF16),
            a2=jnp.pad(iclr_a2[l], ((r_w, 0), (0, 0))).astype(BF16),
            g2=gate_g2[l].astype(BF16),
            k_k=k_k[l][None], k_a=k_a[l][None], r_k=r_k[l].reshape(1, D_R),
            lnx_g=lnx_g[l][None], lnx_b=lnx_b[l][None],
            w_out=w_out[l].astype(BF16), norm2_g=norm2_g[l][None])
        if l % 2 == 0:
            lp.update(wg=ffn_wg[l // 2].astype(BF16), wu=ffn_wu[l // 2].astype(BF16),
                      wd=ffn_wd[l // 2].astype(BF16))
        else:
            lp.update(router_w=router_w[l // 2], wg=moe_wg[l // 2].astype(BF16),
                      wu=moe_wu[l // 2].astype(BF16), wd=moe_wd[l // 2].astype(BF16))
        layers.append(lp)
    y_p, pk, pv, ppool, pshift, pwkv = _trunk(x_prompt, layers, None, None, None, None, None)
    y_s, sk, sv, spool, sshift, swkv = _trunk(x_sample, layers, cache_k, cache_v, state_pool,
                                              state_shift, state_wkv)
    return (y_p, y_s, pk, pv, ppool, pshift, pwkv, sk, sv, spool, sshift, swkv)
```

```python
import functools

import jax
import jax.numpy as jnp
from jax import lax
from jax.experimental import pallas as pl
from jax.experimental.pallas import tpu as pltpu

F32 = jnp.float32
BF16 = jnp.bfloat16

HEAD_DIM = 64
A_HEADS = 8
A_KV_HEADS = 2
A_GROUP = A_HEADS // A_KV_HEADS
D_A = A_HEADS * HEAD_DIM
D_KV = A_KV_HEADS * HEAD_DIM
WINDOW = 128
CHUNK = 64
POOL_WINDOWS = (2, 4, 8, 16)
POOL_CTX = 15
POOL_HALO = 16
D_POOL = 256
POOL_GW = D_POOL // len(POOL_WINDOWS)
D_R = 256
R_HEADS = D_R // HEAD_DIM
R_WA = 128
R_G = 128
D_R_IN = 3 * D_R + R_WA + R_G
TOP_K = 2
NORM_EPS = 1e-6
GN_EPS = 64e-5
NEG_INF = -1e30
VMEM_LIMIT_BYTES = 56 * 1024 * 1024


def _cparams(*sem):
    return pltpu.CompilerParams(dimension_semantics=sem, vmem_limit_bytes=VMEM_LIMIT_BYTES)


def _mm(a, b):
    return jnp.dot(a.astype(BF16), b.astype(BF16), preferred_element_type=F32)


def _mm_nt(a, b):
    return lax.dot_general(a.astype(BF16), b.astype(BF16), (((1,), (1,)), ((), ())),
                           preferred_element_type=F32)


def _split2(x):
    hi = x.astype(BF16)
    lo = (x - hi.astype(F32)).astype(BF16)
    return hi, lo


def _split3(x):
    hi = x.astype(BF16)
    r1 = x - hi.astype(F32)
    mid = r1.astype(BF16)
    lo = (r1 - mid.astype(F32)).astype(BF16)
    return hi, mid, lo


def _mm_exact_rhs(x, b):
    bb = b.astype(BF16)
    hi, lo = _split2(x)
    dot = lambda u: jnp.dot(u, bb, preferred_element_type=F32)
    return dot(hi) + dot(lo)


def _mm_exact_lhs(a, x):
    ab = a.astype(BF16)
    hi, lo = _split2(x)
    dot = lambda u: jnp.dot(ab, u, preferred_element_type=F32)
    return dot(hi) + dot(lo)


def _mm_hi(a, b):
    ah, al = _split2(a)
    bh, bl = _split2(b)
    dot = lambda u, v: jnp.dot(u, v, preferred_element_type=F32)
    return dot(ah, bh) + (dot(ah, bl) + dot(al, bh))


def _sigmoid(x):
    return 1.0 / (1.0 + jnp.exp(-x))


def _silu(x):
    return x * _sigmoid(x)


def _norm_matmul_kernel(x_ref, g_ref, w_ref, o_ref):
    x = x_ref[...]
    h = x * lax.rsqrt(jnp.mean(x * x, axis=-1, keepdims=True) + NORM_EPS) * g_ref[...]
    o_ref[...] = jnp.dot(h.astype(BF16), w_ref[...], preferred_element_type=F32)


def _norm_matmul(x, g, w, tm):
    n, d = x.shape
    dout = w.shape[1]
    return pl.pallas_call(
        _norm_matmul_kernel,
        grid=(n // tm,),
        in_specs=[pl.BlockSpec((tm, d), lambda i: (i, 0)),
                  pl.BlockSpec((1, d), lambda i: (0, 0)),
                  pl.BlockSpec((d, dout), lambda i: (0, 0))],
        out_specs=pl.BlockSpec((tm, dout), lambda i: (i, 0)),
        out_shape=jax.ShapeDtypeStruct((n, dout), F32),
        compiler_params=_cparams("parallel"),
        name="norm_in_proj",
    )(x, g, w)


def _block_ones(n):
    r = lax.broadcasted_iota(jnp.int32, (n, n), 0) // HEAD_DIM
    c = lax.broadcasted_iota(jnp.int32, (n, n), 1) // HEAD_DIM
    return (r == c).astype(BF16)


def _head_rms_scale(z, ones):
    hi, lo = _split2(z * z)
    ss = jnp.dot(hi, ones, preferred_element_type=F32) + jnp.dot(lo, ones, preferred_element_type=F32)
    return lax.rsqrt(ss * (1.0 / HEAD_DIM) + NORM_EPS)


def _attn_kernel(sink_ref, q_ref, kv_ref, prev_ref, bias_ref, qg_ref, kg_ref, ya_ref, kn_ref,
                 *, rq, n_part, prev_is_raw):
    i = pl.program_id(1)
    kp = WINDOW + rq
    gw = A_GROUP * HEAD_DIM
    q = q_ref[0]
    kv = kv_ref[0]
    pv = prev_ref[0]
    ones_q = _block_ones(D_A)
    ones_k = _block_ones(D_KV)
    qn = (q * _head_rms_scale(q, ones_q) * qg_ref[...]).astype(BF16)
    k_cur = kv[:, 0:D_KV]
    k_cur = k_cur * _head_rms_scale(k_cur, ones_k) * kg_ref[...]
    kn_ref[0] = k_cur
    k_prev = pv[:, 0:D_KV]
    if prev_is_raw:
        k_prev = k_prev * _head_rms_scale(k_prev, ones_k) * kg_ref[...]
    k_all = jnp.concatenate([k_prev, k_cur], axis=0).astype(BF16)
    v_all = jnp.concatenate([pv[:, D_KV:], kv[:, D_KV:]], axis=0).astype(BF16)
    nq = A_GROUP * rq
    lane_in = lax.broadcasted_iota(jnp.int32, (D_KV, gw), 0)
    lane_out = lax.broadcasted_iota(jnp.int32, (D_KV, gw), 1)
    lane_out_t = lax.broadcasted_iota(jnp.int32, (gw, D_KV), 0)
    lane_in_t = lax.broadcasted_iota(jnp.int32, (gw, D_KV), 1)
    slot_lane = lax.broadcasted_iota(jnp.int32, (1, gw), 1) // HEAD_DIM
    slot_mask_bf = [(slot_lane == hh).astype(BF16) for hh in range(A_GROUP)]
    slot_row = lax.broadcasted_iota(jnp.int32, (gw, rq), 0) // HEAD_DIM
    key_row = lax.broadcasted_iota(jnp.int32, (kp, nq), 0)
    col_head = lax.broadcasted_iota(jnp.int32, (1, nq), 1) // rq
    for g in range(A_KV_HEADS):
        select = (lane_in // HEAD_DIM == g) & (lane_in % HEAD_DIM == lane_out % HEAD_DIM)
        k_wide = jnp.dot(k_all, select.astype(BF16), preferred_element_type=F32).astype(BF16)
        select_t = (lane_in_t // HEAD_DIM == g) & (lane_in_t % HEAD_DIM == lane_out_t % HEAD_DIM)
        v_wide_t = lax.dot_general(select_t.astype(BF16), v_all, (((1,), (1,)), ((), ())),
                                   preferred_element_type=F32).astype(BF16)
        q_g = qn[:, g * gw:(g + 1) * gw]
        sink_row = jnp.zeros((1, nq), F32)
        for hh in range(A_GROUP):
            sink_row = jnp.where(col_head == hh, sink_ref[g * A_GROUP + hh], sink_row)
        bias_t = bias_ref[g]
        for p in range(n_part):
            rows = slice(p * rq, (p + 1) * rq)
            qs = jnp.concatenate([q_g[rows] * m for m in slot_mask_bf], axis=0)
            kb = k_wide[p * rq:p * rq + kp]
            s = lax.dot_general(kb, qs, (((1,), (1,)), ((), ())), preferred_element_type=F32) + bias_t
            if prev_is_raw and p * rq < WINDOW:
                n_pad = WINDOW - (i * n_part + p) * rq
                s = s + jnp.where(key_row < n_pad, NEG_INF, 0.0)
            m = jnp.maximum(jnp.max(s, axis=0, keepdims=True), sink_row)
            e = jnp.exp(s - m)
            den = jnp.sum(e, axis=0, keepdims=True) + jnp.exp(sink_row - m)
            prob = (e * (1.0 / den)).astype(BF16)
            o_t = jnp.dot(v_wide_t[:, p * rq:p * rq + kp], prob, preferred_element_type=F32)
            out_t = o_t[:, (A_GROUP - 1) * rq:]
            for hh in range(A_GROUP - 2, -1, -1):
                out_t = jnp.where(slot_row == hh, o_t[:, hh * rq:(hh + 1) * rq], out_t)
            ya_ref[0, rows, g * gw:(g + 1) * gw] = out_t.T.astype(BF16)


def _attn_bias(rq, cq):
    kp = WINDOW + rq
    i = jnp.arange(rq)[:, None]
    j = jnp.arange(kp)[None, :]
    jb = j - cq * (i // cq)
    valid = (jb >= 0) & (jb < WINDOW + cq)
    dist = jnp.abs(WINDOW + (i % cq) - jb).astype(F32)
    slopes = jnp.exp2(-8.0 * jnp.arange(1, A_HEADS + 1, dtype=F32) / A_HEADS)
    bias = jnp.where(valid[None], -slopes[:, None, None] * dist[None], NEG_INF)
    return jnp.swapaxes(bias.reshape(A_KV_HEADS, A_GROUP * rq, kp), 1, 2)


def _attention(proj, cache_kv, q_gain, k_gain, sink, cq, rq, n_part):
    b, t, _ = proj.shape
    tq = rq * n_part
    prompt = cache_kv is None
    kv_col = D_A // (2 * D_KV)
    if prompt:
        prev_arr = proj
        prev_spec = pl.BlockSpec((1, WINDOW, 2 * D_KV),
                                 lambda bi, i: (bi, jnp.maximum(i * (tq // WINDOW) - 1, 0), kv_col))
    else:
        prev_arr = cache_kv
        prev_spec = pl.BlockSpec((1, WINDOW, 2 * D_KV), lambda bi, i: (bi, 0, 0))
    kp = WINDOW + rq
    q_gain_t = jnp.tile(q_gain * (HEAD_DIM ** -0.5), (1, A_HEADS))
    k_gain_t = jnp.tile(k_gain, (1, A_KV_HEADS))
    kern = functools.partial(_attn_kernel, rq=rq, n_part=n_part, prev_is_raw=prompt)
    return pl.pallas_call(
        kern,
        grid=(b, t // tq),
        in_specs=[pl.BlockSpec(memory_space=pltpu.SMEM),
                  pl.BlockSpec((1, tq, D_A), lambda bi, i: (bi, i, 0)),
                  pl.BlockSpec((1, tq, 2 * D_KV), lambda bi, i: (bi, i, kv_col)),
                  prev_spec,
                  pl.BlockSpec((A_KV_HEADS, kp, A_GROUP * rq), lambda bi, i: (0, 0, 0)),
                  pl.BlockSpec((1, D_A), lambda bi, i: (0, 0)),
                  pl.BlockSpec((1, D_KV), lambda bi, i: (0, 0))],
        out_specs=[pl.BlockSpec((1, tq, D_A), lambda bi, i: (bi, i, 0)),
                   pl.BlockSpec((1, tq, D_KV), lambda bi, i: (bi, i, 0))],
        out_shape=[jax.ShapeDtypeStruct((b, t, D_A), BF16),
                   jax.ShapeDtypeStruct((b, t, D_KV), F32)],
        compiler_params=_cparams("parallel", "arbitrary"),
        name="swa_attention",
    )(sink, proj, proj, prev_arr, _attn_bias(rq, cq), q_gain_t, k_gain_t)


def _pool_kernel(u_ref, halo_ref, prefix_ref, w_ref, scale_ref, yb_ref, *, n_prefix):
    i = pl.program_id(1)
    u = u_ref[0]
    tb = u.shape[0]
    halo = jnp.where(i == 0, prefix_ref[0], halo_ref[0])
    ext = jnp.concatenate([halo, u], axis=0)
    col = lax.broadcasted_iota(jnp.int32, (1, D_POOL), 1)
    pos = i * tb + lax.broadcasted_iota(jnp.int32, (tb, 1), 0)
    total = None
    count = None
    acc = ext
    span = 1
    for gi, w in enumerate(POOL_WINDOWS):
        while span < w:
            acc = acc + pltpu.roll(acc, span, axis=0)
            span *= 2
        in_group = (col >= gi * POOL_GW) & (col < (gi + 1) * POOL_GW)
        tail = acc[POOL_HALO:]
        total = jnp.where(in_group, tail, 0.0 if total is None else total)
        cnt = jnp.minimum(pos + (1 + n_prefix), w).astype(F32)
        count = jnp.where(in_group, cnt, 1.0 if count is None else count)
    d = total / count - u
    yb_ref[0] = (_mm(d, w_ref[...]) * scale_ref[...]).astype(BF16)


def _pool_mix(proj, prefix, w_blockdiag, scale, n_prefix, tb):
    b, t, _ = proj.shape
    col = (D_A + 2 * D_KV) // D_POOL
    kern = functools.partial(_pool_kernel, n_prefix=n_prefix)
    return pl.pallas_call(
        kern,
        grid=(b, t // tb),
        in_specs=[pl.BlockSpec((1, tb, D_POOL), lambda bi, i: (bi, i, col)),
                  pl.BlockSpec((1, POOL_HALO, D_POOL),
                               lambda bi, i: (bi, jnp.maximum(i * (tb // POOL_HALO) - 1, 0), col)),
                  pl.BlockSpec((1, POOL_HALO, D_POOL), lambda bi, i: (bi, 0, 0)),
                  pl.BlockSpec((D_POOL, D_POOL), lambda bi, i: (0, 0)),
                  pl.BlockSpec((1, D_POOL), lambda bi, i: (0, 0))],
        out_specs=pl.BlockSpec((1, tb, D_POOL), lambda bi, i: (bi, i, 0)),
        out_shape=jax.ShapeDtypeStruct((b, t, D_POOL), BF16),
        compiler_params=_cparams("parallel", "arbitrary"),
        name="pool_mix",
    )(proj, proj, prefix, w_blockdiag, scale)


def _rwkv_kernel(p_ref, prev_ref, st0_ref, mu_ref, w0_ref, a0_ref, w2_ref, a2_ref, g2_ref,
                 kk_ref, ka_ref, rk_ref, lng_ref, lnb_ref, yc_ref, st_ref, carry_ref, y_ref,
                 *, chunk):
    j = pl.program_id(1)

    @pl.when(j == 0)
    def _():
        carry_ref[...] = prev_ref[0]
        st_ref[0] = st0_ref[0]

    p = p_ref[0]
    tb = p.shape[0]
    n_chunk = tb // chunk
    hl = R_HEADS * chunk

    row = lax.broadcasted_iota(jnp.int32, (tb, 1), 0)
    p_prev = jnp.where(row == 0, carry_ref[...], pltpu.roll(p, 1, axis=0))
    carry_ref[...] = p[tb - 1:tb]
    xs = p + mu_ref[...] * (p_prev - p)
    r = xs[:, 0:D_R]
    k = xs[:, D_R:2 * D_R]
    v = xs[:, 2 * D_R:3 * D_R]
    wa = xs[:, 3 * D_R:3 * D_R + R_WA]
    gd = xs[:, 3 * D_R + R_WA:]

    z = -(w0_ref[...] + _mm(jnp.tanh(wa), w2_ref[...]))
    softplus = jnp.maximum(z, 0.0) + jnp.log(1.0 + jnp.exp(-jnp.abs(z)))
    lw = -jnp.exp(-softplus - 0.5)
    a = _sigmoid(a0_ref[...] + _mm(wa, a2_ref[...]))
    g = _mm(_sigmoid(gd), g2_ref[...])

    lane_r = lax.broadcasted_iota(jnp.int32, (D_R, D_R), 0) // HEAD_DIM
    lane_c = lax.broadcasted_iota(jnp.int32, (D_R, D_R), 1) // HEAD_DIM
    head_ones = (lane_r == lane_c).astype(F32)
    seg_sum = lambda t: _mm_exact_rhs(t, head_ones)

    kk = k * kk_ref[...]
    kk = kk / jnp.maximum(jnp.sqrt(seg_sum(kk * kk)), 1e-12)
    k2 = k * (1.0 + (a - 1.0) * ka_ref[...])
    bb = kk * a

    ti = lax.broadcasted_iota(jnp.int32, (tb, tb), 0)
    tj = lax.broadcasted_iota(jnp.int32, (tb, tb), 1)
    cum = _mm_exact_lhs(((ti // chunk == tj // chunk) & (tj <= ti)).astype(F32), lw)

    wi = lax.broadcasted_iota(jnp.int32, (chunk, hl), 0)
    wj = lax.broadcasted_iota(jnp.int32, (chunk, hl), 1) % chunk
    strict = wj < wi
    incl = wj <= wi
    eye_w = (wj == wi).astype(F32)
    diag_k = (lax.broadcasted_iota(jnp.int32, (HEAD_DIM, D_R), 0)
              == lax.broadcasted_iota(jnp.int32, (HEAD_DIM, D_R), 1) % HEAD_DIM)
    lane_k = lax.broadcasted_iota(jnp.int32, (1, D_R), 1) // HEAD_DIM
    lane_t = lax.broadcasted_iota(jnp.int32, (1, hl), 1) // chunk
    mask_k = [(lane_k == h).astype(F32) for h in range(R_HEADS)]
    mask_k_bf = [m.astype(BF16) for m in mask_k]
    mask_t_bf = [(lane_t == h).astype(BF16) for h in range(R_HEADS)]

    def blockdiag(t, masks):
        t16 = t.astype(BF16)
        return jnp.concatenate([t16 * m for m in masks], axis=0)

    def wide_transpose(t):
        tt = jnp.concatenate([t * m for m in mask_k], axis=0).T
        out = tt[0:HEAD_DIM]
        for h in range(1, R_HEADS):
            out = out + tt[h * HEAD_DIM:(h + 1) * HEAD_DIM]
        return out.astype(BF16)

    chunks = []
    for c in range(n_chunk):
        sl = slice(c * chunk, (c + 1) * chunk)
        cum_c = cum[sl]
        cum_last = cum_c[chunk - 1:chunk]
        g_in = jnp.exp(cum_c)
        g_prev = jnp.exp(cum_c - lw[sl])
        g_inv = jnp.exp(-cum_c)
        g_out = jnp.exp(cum_last - cum_c)
        a_n = (kk[sl] * g_prev).astype(BF16)
        r_n = r[sl] * g_in
        ch = dict(sl=sl, r_n=r_n, g_last=jnp.exp(cum_last),
                  a_s=blockdiag(a_n, mask_k_bf),
                  v_s=blockdiag(v[sl], mask_k_bf),
                  bo_w=wide_transpose(bb[sl] * g_out),
                  ko_w=wide_transpose(k2[sl] * g_out))
        ar = jnp.concatenate([a_n, r_n.astype(BF16)], axis=0)
        m_b = _mm_nt(ar, blockdiag(bb[sl] * g_inv, mask_k_bf))
        m_k = _mm_nt(ar, blockdiag(k2[sl] * g_inv, mask_k_bf))
        m_ab = jnp.where(strict, m_b[:chunk], 0.0)
        ch.update(m_rb=jnp.where(incl, m_b[chunk:], 0.0).astype(BF16),
                  m_ak=jnp.where(strict, m_k[:chunk], 0.0).astype(BF16),
                  m_rk=jnp.where(incl, m_k[chunk:], 0.0).astype(BF16),
                  t_inv=eye_w - m_ab, pw=m_ab.astype(BF16))
        chunks.append(ch)

    for ch in chunks:
        ch["pw"] = _mm(ch["pw"], blockdiag(ch["pw"], mask_t_bf)).astype(BF16)
    n = 2
    while n < chunk:
        for ch in chunks:
            rhs = blockdiag(ch["pw"], mask_t_bf)
            if 2 * n < chunk:
                both = _mm(jnp.concatenate([ch["pw"], ch["t_inv"].astype(BF16)], axis=0), rhs)
                ch["pw"] = both[:chunk].astype(BF16)
                ch["t_inv"] = ch["t_inv"] + both[chunk:]
            else:
                ch["t_inv"] = ch["t_inv"] + _mm(ch["t_inv"], rhs)
        n *= 2

    for ch in chunks:
        t_inv = ch["t_inv"].astype(BF16)
        ch["a_bar"] = blockdiag(_mm(t_inv, ch["a_s"]), mask_k_bf)
        on_v = _mm(jnp.concatenate([ch["m_ak"], ch["m_rk"], ch["ko_w"]], axis=0), ch["v_s"])
        ch["mrk_v"], ch["ko_v"] = on_v[chunk:2 * chunk], on_v[2 * chunk:]
        ch["u0"] = blockdiag(-_mm(t_inv, blockdiag(on_v[:chunk], mask_k_bf)), mask_k_bf)
    for ch in chunks:
        lhs = jnp.concatenate([ch["m_rb"], ch["bo_w"]], axis=0)
        on_a = _mm(lhs, ch["a_bar"])
        on_u = _mm(lhs, ch["u0"])
        ch["r_bar"] = (ch["r_n"] - on_a[:chunk]).astype(BF16)
        ch["y0"] = ch["mrk_v"] + on_u[:chunk]
        ch["g_w"] = (jnp.where(diag_k, ch["g_last"], 0.0) - on_a[chunk:]).astype(BF16)
        ch["h_w"] = on_u[chunk:] + ch["ko_v"]

    st_w = st_ref[0]
    for ch in chunks:
        on_st = _mm(jnp.concatenate([ch["r_bar"], ch["g_w"]], axis=0), blockdiag(st_w, mask_k_bf))
        y_ref[ch["sl"], :] = ch["y0"] + on_st[:chunk]
        st_w = on_st[chunk:] + ch["h_w"]
    st_ref[0] = st_w

    y = y_ref[...]
    mean = seg_sum(y) * (1.0 / HEAD_DIM)
    d = y - mean
    var = seg_sum(d * d) * (1.0 / HEAD_DIM)
    yn = d * lax.rsqrt(var + GN_EPS) * lng_ref[...] + lnb_ref[...]
    bonus = seg_sum(r * k2 * rk_ref[...]) * v
    yc_ref[0] = ((yn + bonus) * g).astype(BF16)


def _rwkv_mix(proj, prev, st0, lp, chunk, tb):
    b, t, _ = proj.shape
    col = (D_A + 2 * D_KV + D_POOL) // D_R_IN
    row = lambda n: pl.BlockSpec((1, n), lambda bi, i: (0, 0))
    full = lambda s: pl.BlockSpec(s, lambda bi, i: (0,) * len(s))
    kern = functools.partial(_rwkv_kernel, chunk=chunk)
    return pl.pallas_call(
        kern,
        grid=(b, t // tb),
        in_specs=[pl.BlockSpec((1, tb, D_R_IN), lambda bi, i: (bi, i, col)),
                  pl.BlockSpec((1, 1, D_R_IN), lambda bi, i: (bi, 0, 0)),
                  pl.BlockSpec((1, HEAD_DIM, D_R), lambda bi, i: (bi, 0, 0)),
                  row(D_R_IN), row(D_R), row(D_R),
                  full((R_WA, D_R)), full((R_WA, D_R)), full((R_G, D_R)),
                  row(D_R), row(D_R), row(D_R), row(D_R), row(D_R)],
        out_specs=[pl.BlockSpec((1, tb, D_R), lambda bi, i: (bi, i, 0)),
                   pl.BlockSpec((1, HEAD_DIM, D_R), lambda bi, i: (bi, 0, 0))],
        out_shape=[jax.ShapeDtypeStruct((b, t, D_R), BF16),
                   jax.ShapeDtypeStruct((b, HEAD_DIM, D_R), F32)],
        scratch_shapes=[pltpu.VMEM((1, D_R_IN), F32), pltpu.VMEM((tb, D_R), F32)],
        compiler_params=_cparams("parallel", "arbitrary"),
        name="rwkv7_mix",
    )(proj, prev, st0, lp["mu"], lp["w0"], lp["a0"], lp["w2"], lp["a2"], lp["g2"],
      lp["k_k"], lp["k_a"], lp["r_k"], lp["lnx_g"], lp["lnx_b"])


def _state_to_wide(s):
    b = s.shape[0]
    return jnp.transpose(s, (0, 3, 1, 2)).reshape(b, HEAD_DIM, D_R)


def _wide_to_state(st):
    b = st.shape[0]
    return jnp.transpose(st.reshape(b, HEAD_DIM, R_HEADS, HEAD_DIM), (0, 2, 3, 1))


def _mix_out_kernel(x_ref, ya_ref, yb_ref, yc_ref, w_ref, g_ref, *rest, route):
    if route:
        rw_ref, xm_ref, h_ref, gate_ref = rest
    else:
        xm_ref, h_ref = rest
    dot = lambda u, lo, hi: jnp.dot(u[...], w_ref[lo:hi, :], preferred_element_type=F32)
    xm = (x_ref[...] + dot(ya_ref, 0, D_A) + dot(yb_ref, D_A, D_A + D_POOL)
          + dot(yc_ref, D_A + D_POOL, D_A + D_POOL + D_R))
    xm_ref[...] = xm
    h = xm * lax.rsqrt(jnp.mean(xm * xm, axis=-1, keepdims=True) + NORM_EPS) * g_ref[...]
    h_hi = h.astype(BF16)
    h_ref[...] = h_hi
    if route:
        rw_t = rw_ref[...]
        n_e = rw_t.shape[0]
        rw_hi = rw_t.astype(BF16).astype(F32)
        rw2 = jnp.concatenate([rw_hi, rw_t - rw_hi], axis=0).astype(BF16)
        h_lo = (h - h_hi.astype(F32)).astype(BF16)
        nt = lambda a, b: lax.dot_general(a, b, (((1,), (1,)), ((), ())), preferred_element_type=F32)
        on_hi = nt(rw2, h_hi)
        logits = on_hi[:n_e] + (on_hi[n_e:] + nt(rw2, h_lo)[:n_e])
        row = lax.broadcasted_iota(jnp.int32, logits.shape, 0).astype(F32)
        m1 = jnp.max(logits, axis=0, keepdims=True)
        i1 = jnp.min(jnp.where(logits == m1, row, float(n_e)), axis=0, keepdims=True)
        rest_l = jnp.where(row == i1, -jnp.inf, logits)
        m2 = jnp.max(rest_l, axis=0, keepdims=True)
        i2 = jnp.min(jnp.where(rest_l == m2, row, float(n_e)), axis=0, keepdims=True)
        e2 = jnp.exp(m2 - m1)
        gates_t = (jnp.where(row == i1, 1.0 / (1.0 + e2), 0.0)
                   + jnp.where(row == i2, e2 / (1.0 + e2), 0.0))
        gate_ref[...] = gates_t.T


def _mix_out(x, ya, yb, yc, w_out, g, router_w, tm):
    n, d = x.shape
    route = router_w is not None
    tile = lambda w: pl.BlockSpec((tm, w), lambda i: (i, 0))
    in_specs = [tile(d), tile(D_A), tile(D_POOL), tile(D_R),
                pl.BlockSpec(w_out.shape, lambda i: (0, 0)),
                pl.BlockSpec((1, d), lambda i: (0, 0))]
    out_specs = [tile(d), tile(d)]
    out_shape = [jax.ShapeDtypeStruct((n, d), F32), jax.ShapeDtypeStruct((n, d), BF16)]
    args = [x, ya, yb, yc, w_out, g]
    if route:
        n_e = router_w.shape[1]
        in_specs.append(pl.BlockSpec((n_e, d), lambda i: (0, 0)))
        out_specs.append(tile(n_e))
        out_shape.append(jax.ShapeDtypeStruct((n, n_e), F32))
        args.append(router_w.T)
    return pl.pallas_call(
        functools.partial(_mix_out_kernel, route=route),
        grid=(n // tm,),
        in_specs=in_specs, out_specs=out_specs, out_shape=out_shape,
        compiler_params=_cparams("parallel"),
        name="mix_out_proj",
    )(*args)


def _swiglu_act(h, wg, wu):
    return (_silu(jnp.dot(h, wg, preferred_element_type=F32))
            * jnp.dot(h, wu, preferred_element_type=F32)).astype(BF16)


def _down_proj(act_ref, wd, tf):
    out = None
    for jj in range(act_ref.shape[0]):
        part = jnp.dot(act_ref[jj], wd[jj * tf:(jj + 1) * tf, :], preferred_element_type=F32)
        out = part if out is None else out + part
    return out


def _ffn_kernel(h_ref, xm_ref, wg_ref, wu_ref, wd_ref, o_ref, act_ref):
    j = pl.program_id(1)
    tf = wg_ref.shape[1]
    act_ref[j] = _swiglu_act(h_ref[...], wg_ref[...], wu_ref[...])

    @pl.when(j == pl.num_programs(1) - 1)
    def _():
        o_ref[...] = xm_ref[...] + _down_proj(act_ref, wd_ref, tf)


def _ffn(h, xm, wg, wu, wd, tm, tf):
    n, d = xm.shape
    f = wg.shape[1]
    return pl.pallas_call(
        _ffn_kernel,
        grid=(n // tm, f // tf),
        in_specs=[pl.BlockSpec((tm, d), lambda i, j: (i, 0)),
                  pl.BlockSpec((tm, d), lambda i, j: (i, 0)),
                  pl.BlockSpec((d, tf), lambda i, j: (0, j)),
                  pl.BlockSpec((d, tf), lambda i, j: (0, j)),
                  pl.BlockSpec((f, d), lambda i, j: (0, 0))],
        out_specs=pl.BlockSpec((tm, d), lambda i, j: (i, 0)),
        out_shape=jax.ShapeDtypeStruct((n, d), F32),
        scratch_shapes=[pltpu.VMEM((f // tf, tm, tf), BF16)],
        compiler_params=_cparams("parallel", "arbitrary"),
        name="swiglu_ffn",
    )(h, xm, wg, wu, wd)


def _moe_kernel(h_ref, xm_ref, gate_ref, wg_ref, wu_ref, wd_ref, o_ref, acc_ref):
    e = pl.program_id(1)
    j = pl.program_id(2)

    @pl.when((e == 0) & (j == 0))
    def _():
        acc_ref[...] = xm_ref[...]

    gates = gate_ref[...]
    lane = lax.broadcasted_iota(jnp.int32, gates.shape, 1)
    gate = jnp.sum(jnp.where(lane == e, gates, 0.0), axis=-1, keepdims=True)
    h = h_ref[...]
    act = (_silu(jnp.dot(h, wg_ref[0], preferred_element_type=F32))
           * jnp.dot(h, wu_ref[0], preferred_element_type=F32))
    acc_ref[...] += gate * jnp.dot(act.astype(BF16), wd_ref[0], preferred_element_type=F32)

    @pl.when((e == pl.num_programs(1) - 1) & (j == pl.num_programs(2) - 1))
    def _():
        o_ref[...] = acc_ref[...]


def _moe(h, xm, gates, wg, wu, wd, tm, tf):
    n, d = xm.shape
    n_e, _, f = wg.shape
    return pl.pallas_call(
        _moe_kernel,
        grid=(n // tm, n_e, f // tf),
        in_specs=[pl.BlockSpec((tm, d), lambda i, e, j: (i, 0)),
                  pl.BlockSpec((tm, d), lambda i, e, j: (i, 0)),
                  pl.BlockSpec((tm, n_e), lambda i, e, j: (i, 0)),
                  pl.BlockSpec((1, d, tf), lambda i, e, j: (e, 0, j)),
                  pl.BlockSpec((1, d, tf), lambda i, e, j: (e, 0, j)),
                  pl.BlockSpec((1, tf, d), lambda i, e, j: (e, j, 0))],
        out_specs=pl.BlockSpec((tm, d), lambda i, e, j: (i, 0)),
        out_shape=jax.ShapeDtypeStruct((n, d), F32),
        scratch_shapes=[pltpu.VMEM((tm, d), F32)],
        compiler_params=_cparams("parallel", "arbitrary", "arbitrary"),
        name="moe_ffn",
    )(h, xm, gates, wg, wu, wd)


MOE_ROW_ALIGN = 32
MOE_GATE_LANES = 128
MOE_TILE = 1024
MOE_GROUP_ROWS = 1024
_UNSELECTED = 1e9


def _tile_routing(gates_sel, rank, axis):
    n_e = rank.shape[axis]
    cnt = jnp.max(rank, axis=1 - axis, keepdims=True)
    padded = jnp.floor((cnt + (MOE_ROW_ALIGN - 1)) * (1.0 / MOE_ROW_ALIGN)) * MOE_ROW_ALIGN
    offs, run = [], jnp.zeros((1, 1), F32)
    for e in range(n_e):
        offs.append(run)
        run = run + (padded[e:e + 1] if axis == 0 else padded[:, e:e + 1])
    off = jnp.concatenate(offs, axis=axis)
    dest = jnp.where(gates_sel, off + rank - 1.0, -1.0)
    d_hi = jnp.max(dest, axis=axis, keepdims=True)
    d_lo = jnp.min(jnp.where(gates_sel, dest, _UNSELECTED), axis=axis, keepdims=True)
    d_lo = jnp.where(d_lo == d_hi, -2.0, d_lo)
    return cnt, dest, d_hi, d_lo


def _moe_dispatch_kernel(h_ref, gate_ref, hs_ref, cnt_ref, *, chunk_rows):
    h = h_ref[...]
    gates = gate_ref[...]
    tm, n_e = gates.shape
    d = h.shape[1]
    mt = hs_ref.shape[0]
    eye = (lax.broadcasted_iota(jnp.int32, (n_e, n_e), 0)
           == lax.broadcasted_iota(jnp.int32, (n_e, n_e), 1)).astype(BF16)
    to_rows = lambda u: lax.dot_general(eye, u, (((1,), (1,)), ((), ())), preferred_element_type=F32)
    g_hi3, g_mid3, g_lo3 = _split3(gates)
    gate_row = to_rows(g_hi3) + to_rows(g_mid3) + to_rows(g_lo3)
    sel_row = gate_row > 0.0
    ti = lax.broadcasted_iota(jnp.int32, (tm, tm), 0)
    tj = lax.broadcasted_iota(jnp.int32, (tm, tm), 1)
    rank = jnp.dot(sel_row.astype(BF16), (ti <= tj).astype(BF16), preferred_element_type=F32)
    cnt, dest, d_hi, d_lo = _tile_routing(sel_row, rank, 0)
    cnt_ref[0] = jnp.broadcast_to(cnt, cnt_ref.shape[1:])
    g_hi = jnp.sum(jnp.where(dest == d_hi, gate_row, 0.0), axis=0, keepdims=True)
    g_lo = jnp.sum(jnp.where(dest == d_lo, gate_row, 0.0), axis=0, keepdims=True)
    lane = lax.broadcasted_iota(jnp.int32, (chunk_rows, MOE_GATE_LANES), 1)
    for c0 in range(0, mt, chunk_rows):
        rho = (c0 + lax.broadcasted_iota(jnp.int32, (chunk_rows, 1), 0)).astype(F32)
        is_hi = d_hi == rho
        is_lo = d_lo == rho
        onehot = jnp.where(is_hi, 1.0, jnp.where(is_lo, 1.0, 0.0)).astype(BF16)
        rows = jnp.dot(onehot, h, preferred_element_type=F32)
        g = jnp.sum(jnp.where(is_hi, g_hi, jnp.where(is_lo, g_lo, 0.0)), axis=1, keepdims=True)
        p0, p1, p2 = [t.astype(F32) for t in _split3(g)]
        gcols = jnp.where(lane == 0, p0, jnp.where(lane == 1, p1, jnp.where(lane == 2, p2, 0.0)))
        hs_ref[c0:c0 + chunk_rows, 0:d] = rows.astype(BF16)
        hs_ref[c0:c0 + chunk_rows, d:d + MOE_GATE_LANES] = gcols.astype(BF16)


def _moe_dispatch(h, gates, tm, mt):
    n, d = h.shape
    n_e = gates.shape[1]
    n_tiles = n // tm
    return pl.pallas_call(
        functools.partial(_moe_dispatch_kernel, chunk_rows=256),
        grid=(n_tiles,),
        in_specs=[pl.BlockSpec((tm, d), lambda i: (i, 0)),
                  pl.BlockSpec((tm, n_e), lambda i: (i, 0))],
        out_specs=[pl.BlockSpec((mt, d + MOE_GATE_LANES), lambda i: (i, 0)),
                   pl.BlockSpec((1, n_e, 128), lambda i: (i, 0, 0))],
        out_shape=[jax.ShapeDtypeStruct((n_tiles * mt, d + MOE_GATE_LANES), BF16),
                   jax.ShapeDtypeStruct((n_tiles, n_e, 128), F32)],
        compiler_params=_cparams("parallel"),
        name="moe_dispatch",
    )(h, gates)


def _moe_tables(cnt, mt, group_rows):
    i32 = jnp.int32
    n_tiles, n_e = cnt.shape
    bpt = mt // MOE_ROW_ALIGN
    g = group_rows // MOE_ROW_ALIGN
    nblk = jnp.ceil(cnt / MOE_ROW_ALIGN).astype(i32)
    cum = jnp.cumsum(nblk, axis=1)
    off_blk = cum - nblk
    src_base = jnp.arange(n_tiles, dtype=i32)[:, None] * bpt + off_blk
    tot = jnp.sum(nblk, axis=0)
    totp = ((tot + g - 1) // g) * g
    es = jnp.cumsum(totp) - totp
    nblk_t = nblk.T
    seg_start = es[:, None] + jnp.cumsum(nblk_t, axis=1) - nblk_t
    n_src = n_tiles * bpt
    n_dst = -(-(n_src + n_e * (g - 1)) // g) * g
    dblk = jnp.arange(n_dst, dtype=i32)[:, None]
    s0, sl, ss = seg_start.reshape(1, -1), nblk_t.reshape(1, -1), src_base.T.reshape(1, -1)
    hit = (dblk >= s0) & (dblk < s0 + sl)
    src_of_dst = jnp.sum(jnp.where(hit, ss + dblk - s0, 0), axis=1)
    n_rb = n_dst // g
    eid = jnp.sum(jnp.arange(n_rb, dtype=i32)[:, None] * g >= es[None, :], axis=1).astype(i32) - 1
    n_valid = ((es[-1] + totp[-1]) // g).reshape(1)
    o_of = jnp.arange(bpt, dtype=i32)[None, :, None]
    off3, len3 = off_blk[:, None, :], nblk[:, None, :]
    hit3 = (o_of >= off3) & (o_of < off3 + len3)
    dst_of_src = jnp.sum(jnp.where(hit3, seg_start.T[:, None, :] + o_of - off3, 0), axis=2).reshape(-1)
    return src_of_dst.astype(i32), dst_of_src.astype(i32), eid, n_valid.astype(i32), n_dst


def _moe_group_kernel(idx_ref, eid_ref, nv_ref, *refs, n_src):
    del idx_ref, eid_ref
    src_refs = refs[:n_src]
    wg_ref, wu_ref, wd_ref, ys_ref, hs_ref, act_ref = refs[n_src:]
    i = pl.program_id(0)
    j = pl.program_id(1)
    d = ys_ref.shape[1]
    tf = wg_ref.shape[2]
    rows = src_refs[0].shape[0]

    @pl.when(i < nv_ref[0])
    def _():
        @pl.when(j == 0)
        def _():
            for q, src_ref in enumerate(src_refs):
                hs_ref[q * rows:(q + 1) * rows, :] = src_ref[...]

        act_ref[j] = _swiglu_act(hs_ref[:, 0:d], wg_ref[0], wu_ref[0])

        @pl.when(j == pl.num_programs(1) - 1)
        def _():
            gp = hs_ref[:, d:d + MOE_GATE_LANES].astype(F32)
            gate = gp[:, 0:1] + gp[:, 1:2] + gp[:, 2:3]
            ys_ref[...] = (_down_proj(act_ref, wd_ref[0], tf) * gate).astype(BF16)

    @pl.when((i >= nv_ref[0]) & (j == 0))
    def _():
        ys_ref[...] = jnp.zeros(ys_ref.shape, BF16)


def _moe_group_ffn(hs, src_of_dst, eid, n_valid, wg, wu, wd, n_rb, rb, tf):
    dw = hs.shape[1]
    d = dw - MOE_GATE_LANES
    f = wg.shape[2]
    nj = f // tf
    n_src = rb // MOE_ROW_ALIGN
    live_i = lambda i, nv: jnp.minimum(i, nv[0] - 1)
    live_j = lambda i, j, nv: jnp.where(i < nv[0], j, nj - 1)
    piece = lambda q: pl.BlockSpec((MOE_ROW_ALIGN, dw),
                                   lambda i, j, idx, eid, nv: (idx[live_i(i, nv) * n_src + q], 0))
    return pl.pallas_call(
        functools.partial(_moe_group_kernel, n_src=n_src),
        grid_spec=pltpu.PrefetchScalarGridSpec(
            num_scalar_prefetch=3, grid=(n_rb, nj),
            in_specs=[piece(q) for q in range(n_src)] + [
                pl.BlockSpec((1, d, tf), lambda i, j, idx, eid, nv: (eid[i], 0, live_j(i, j, nv))),
                pl.BlockSpec((1, d, tf), lambda i, j, idx, eid, nv: (eid[i], 0, live_j(i, j, nv))),
                pl.BlockSpec((1, f, d), lambda i, j, idx, eid, nv: (eid[i], 0, 0))],
            out_specs=pl.BlockSpec((rb, d), lambda i, j, idx, eid, nv: (i, 0)),
            scratch_shapes=[pltpu.VMEM((rb, dw), BF16), pltpu.VMEM((nj, rb, tf), BF16)]),
        out_shape=jax.ShapeDtypeStruct((n_rb * rb, d), BF16),
        compiler_params=_cparams("arbitrary", "arbitrary"),
        name="moe_group_ffn",
    )(src_of_dst, eid, n_valid, *([hs] * n_src), wg, wu, wd)


def _moe_combine_kernel(idx_ref, xm_ref, gate_ref, *refs):
    del idx_ref
    o_ref, ys_ref = refs[-2:]
    rows = refs[0].shape[0]
    for q, src_ref in enumerate(refs[:-2]):
        ys_ref[q * rows:(q + 1) * rows, :] = src_ref[...]
    gates = gate_ref[...]
    tm, n_e = gates.shape
    mt = ys_ref.shape[0]
    sel = gates > 0.0
    ti = lax.broadcasted_iota(jnp.int32, (tm, tm), 0)
    tj = lax.broadcasted_iota(jnp.int32, (tm, tm), 1)
    rank = jnp.dot((tj <= ti).astype(BF16), sel.astype(BF16), preferred_element_type=F32)
    _, _, d_hi, d_lo = _tile_routing(sel, rank, 1)
    rho = lax.broadcasted_iota(jnp.int32, (1, mt), 1).astype(F32)
    onehot = jnp.where(d_hi == rho, 1.0, jnp.where(d_lo == rho, 1.0, 0.0)).astype(BF16)
    o_ref[...] = xm_ref[...] + jnp.dot(onehot, ys_ref[...], preferred_element_type=F32)


def _moe_combine(xm, gates, ys_exp, dst_of_src, tm, mt):
    n, d = xm.shape
    n_e = gates.shape[1]
    bpt = mt // MOE_ROW_ALIGN
    piece = lambda q: pl.BlockSpec((MOE_ROW_ALIGN, d), lambda i, idx: (idx[i * bpt + q], 0))
    return pl.pallas_call(
        _moe_combine_kernel,
        grid_spec=pltpu.PrefetchScalarGridSpec(
            num_scalar_prefetch=1, grid=(n // tm,),
            in_specs=[pl.BlockSpec((tm, d), lambda i, idx: (i, 0)),
                      pl.BlockSpec((tm, n_e), lambda i, idx: (i, 0))] + [piece(q) for q in range(bpt)],
            out_specs=pl.BlockSpec((tm, d), lambda i, idx: (i, 0)),
            scratch_shapes=[pltpu.VMEM((mt, d), BF16)]),
        out_shape=jax.ShapeDtypeStruct((n, d), F32),
        compiler_params=_cparams("arbitrary"),
        name="moe_combine",
    )(dst_of_src, xm, gates, *([ys_exp] * bpt))


def _moe_sparse(h, xm, gates, wg, wu, wd, tf):
    n_e = gates.shape[1]
    tm = MOE_TILE
    mt = TOP_K * tm + n_e * MOE_ROW_ALIGN
    hs_tile, cnt = _moe_dispatch(h, gates, tm, mt)
    src_of_dst, dst_of_src, eid, n_valid, n_dst = _moe_tables(cnt[:, :, 0], mt, MOE_GROUP_ROWS)
    per_step = MOE_GROUP_ROWS // MOE_ROW_ALIGN
    ys_exp = _moe_group_ffn(hs_tile, src_of_dst, eid, n_valid, wg, wu, wd, n_dst // per_step,
                            MOE_GROUP_ROWS, tf)
    return _moe_combine(xm, gates, ys_exp, dst_of_src, tm, mt)


def _pick(n, candidates):
    for c in candidates:
        if n % c == 0:
            return c
    raise ValueError(f"no tile for {n}")


def _trunk(x, layers, cache_k, cache_v, state_pool, state_shift, state_wkv):
    prompt = cache_k is None
    b, t, d = x.shape
    n = b * t
    tm = _pick(n, (512, 256, 128))
    tm_ffn = _pick(n, (1024, 512, 256, 128))
    tf = _pick(layers[0]["wg"].shape[-1], (896, 512, 256, 128))
    cq = CHUNK if prompt else t
    rq = _pick(t, (2 * CHUNK,)) if prompt else t
    n_part = _pick(t // rq, (4, 2, 1))
    pool_tb = _pick(t, (512, 256, 128, 64, 32))
    r_chunk = CHUNK if prompt else t
    r_tb = _pick(t, (512, 256, 128, 64, 32))
    x2 = x.reshape(n, d)
    nk, nv, npool, nshift, nwkv = [], [], [], [], []
    for l, lp in enumerate(layers):
        proj2 = _norm_matmul(x2, lp["norm1_g"], lp["w_in"], tm)
        proj = proj2.reshape(b, t, -1)
        c_pool = D_A + 2 * D_KV
        c_r = c_pool + D_POOL
        v_raw = proj[:, :, D_A + D_KV:c_pool]
        if prompt:
            cache_kv = None
            prefix = jnp.zeros((b, POOL_HALO, D_POOL), F32)
            prev = jnp.zeros((b, 1, D_R_IN), F32)
            st0 = jnp.zeros((b, HEAD_DIM, D_R), F32)
        else:
            cache_kv = jnp.concatenate([cache_k[l].reshape(b, WINDOW, D_KV),
                                        cache_v[l].reshape(b, WINDOW, D_KV)], axis=-1)
            prefix = jnp.pad(state_pool[l], ((0, 0), (POOL_HALO - POOL_CTX, 0), (0, 0)))
            prev = state_shift[l]
            st0 = _state_to_wide(state_wkv[l])
        ya, k_norm = _attention(proj, cache_kv, lp["q_gain"], lp["k_gain"], lp["sink"], cq, rq, n_part)
        yb = _pool_mix(proj, prefix, lp["pool_w"], lp["pool_scale"], 0 if prompt else POOL_CTX, pool_tb)
        yc, st_fin = _rwkv_mix(proj, prev, st0, lp, r_chunk, r_tb)
        keep = WINDOW if prompt else t
        nk.append(k_norm[:, t - keep:].reshape(b, keep, A_KV_HEADS, HEAD_DIM))
        nv.append(v_raw[:, t - keep:].reshape(b, keep, A_KV_HEADS, HEAD_DIM))
        npool.append(proj[:, t - POOL_CTX:, c_pool:c_r])
        nshift.append(proj[:, t - 1:, c_r:])
        nwkv.append(_wide_to_state(st_fin))
        outs = _mix_out(x2, ya.reshape(n, -1), yb.reshape(n, -1), yc.reshape(n, -1),
                        lp["w_out"], lp["norm2_g"], lp.get("router_w"), tm)
        if "router_w" in lp:
            xm, h2, gates = outs
            if n % MOE_TILE == 0 and n >= 2 * MOE_TILE:
                x2 = _moe_sparse(h2, xm, gates, lp["wg"], lp["wu"], lp["wd"], tf)
            else:
                x2 = _moe(h2, xm, gates, lp["wg"], lp["wu"], lp["wd"], tm_ffn, tf)
        else:
            xm, h2 = outs
            x2 = _ffn(h2, xm, lp["wg"], lp["wu"], lp["wd"], tm_ffn, tf)
    return (x2.reshape(b, t, d), jnp.stack(nk), jnp.stack(nv), jnp.stack(npool),
            jnp.stack(nshift), jnp.stack(nwkv))


def kernel(x_prompt, x_sample, cache_k, cache_v, state_pool, state_shift, state_wkv, norm1_g, w_in, q_gain, k_gain, attn_sink, pool_w, pool_scale, shift_mu, decay_w0, decay_w2, iclr_a0, iclr_a2, gate_g2, k_k, k_a, r_k, lnx_g, lnx_b, w_out, norm2_g, ffn_wg, ffn_wu, ffn_wd, router_w, moe_wg, moe_wu, moe_wd):
    depth = w_in.shape[0]
    r_w = decay_w2.shape[1]
    layers = []
    for l in range(depth):
        lp = dict(
            norm1_g=norm1_g[l][None], w_in=w_in[l].astype(BF16),
            q_gain=q_gain[l][None], k_gain=k_gain[l][None], sink=attn_sink[l],
            pool_w=jax.scipy.linalg.block_diag(*[pool_w[l, gi] for gi in range(len(POOL_WINDOWS))]).astype(BF16),
            pool_scale=pool_scale[l][None],
            mu=shift_mu[l][None], w0=decay_w0[l][None], a0=iclr_a0[l][None],
            w2=jnp.pad(decay_w2[l], ((0, R_WA - r_w), (0, 0))).astype(BF16),
            a2=jnp.pad(iclr_a2[l], ((r_w, 0), (0, 0))).astype(BF16),
            g2=gate_g2[l].astype(BF16),
            k_k=k_k[l][None], k_a=k_a[l][None], r_k=r_k[l].reshape(1, D_R),
            lnx_g=lnx_g[l][None], lnx_b=lnx_b[l][None],
            w_out=w_out[l].astype(BF16), norm2_g=norm2_g[l][None])
        if l % 2 == 0:
            lp.update(wg=ffn_wg[l // 2].astype(BF16), wu=ffn_wu[l // 2].astype(BF16),
                      wd=ffn_wd[l // 2].astype(BF16))
        else:
            lp.update(router_w=router_w[l // 2], wg=moe_wg[l // 2].astype(BF16),
                      wu=moe_wu[l // 2].astype(BF16), wd=moe_wd[l // 2].astype(BF16))
        layers.append(lp)
    y_p, pk, pv, ppool, pshift, pwkv = _trunk(x_prompt, layers, None, None, None, None, None)
    y_s, sk, sv, spool, sshift, swkv = _trunk(x_sample, layers, cache_k, cache_v, state_pool,
                                              state_shift, state_wkv)
    return (y_p, y_s, pk, pv, ppool, pshift, pwkv, sk, sv, spool, sshift, swkv)
```

```python
import functools

import jax
import jax.numpy as jnp
from jax import lax
from jax.experimental import pallas as pl
from jax.experimental.pallas import tpu as pltpu

F32 = jnp.float32
BF16 = jnp.bfloat16

HEAD_DIM = 64
A_HEADS = 8
A_KV_HEADS = 2
A_GROUP = A_HEADS // A_KV_HEADS
D_A = A_HEADS * HEAD_DIM
D_KV = A_KV_HEADS * HEAD_DIM
WINDOW = 128
CHUNK = 64
POOL_WINDOWS = (2, 4, 8, 16)
POOL_CTX = 15
POOL_HALO = 16
D_POOL = 256
POOL_GW = D_POOL // len(POOL_WINDOWS)
D_R = 256
R_HEADS = D_R // HEAD_DIM
R_WA = 128
R_G = 128
D_R_IN = 3 * D_R + R_WA + R_G
TOP_K = 2
NORM_EPS = 1e-6
GN_EPS = 64e-5
NEG_INF = -1e30
VMEM_LIMIT_BYTES = 56 * 1024 * 1024


def _cparams(*sem):
    return pltpu.CompilerParams(dimension_semantics=sem, vmem_limit_bytes=VMEM_LIMIT_BYTES)


def _mm(a, b):
    return jnp.dot(a.astype(BF16), b.astype(BF16), preferred_element_type=F32)


def _mm_nt(a, b):
    return lax.dot_general(a.astype(BF16), b.astype(BF16), (((1,), (1,)), ((), ())),
                           preferred_element_type=F32)


def _split2(x):
    hi = x.astype(BF16)
    lo = (x - hi.astype(F32)).astype(BF16)
    return hi, lo


def _split3(x):
    hi = x.astype(BF16)
    r1 = x - hi.astype(F32)
    mid = r1.astype(BF16)
    lo = (r1 - mid.astype(F32)).astype(BF16)
    return hi, mid, lo


def _mm_exact_rhs(x, b):
    bb = b.astype(BF16)
    hi, lo = _split2(x)
    dot = lambda u: jnp.dot(u, bb, preferred_element_type=F32)
    return dot(hi) + dot(lo)


def _mm_exact_lhs(a, x):
    ab = a.astype(BF16)
    hi, lo = _split2(x)
    dot = lambda u: jnp.dot(ab, u, preferred_element_type=F32)
    return dot(hi) + dot(lo)


def _mm_hi(a, b):
    ah, al = _split2(a)
    bh, bl = _split2(b)
    dot = lambda u, v: jnp.dot(u, v, preferred_element_type=F32)
    return dot(ah, bh) + (dot(ah, bl) + dot(al, bh))


def _sigmoid(x):
    return 1.0 / (1.0 + jnp.exp(-x))


def _silu(x):
    return x * _sigmoid(x)


def _norm_matmul_kernel(x_ref, g_ref, w_ref, o_ref):
    x = x_ref[...]
    h = x * lax.rsqrt(jnp.mean(x * x, axis=-1, keepdims=True) + NORM_EPS) * g_ref[...]
    o_ref[...] = jnp.dot(h.astype(BF16), w_ref[...], preferred_element_type=F32)


def _norm_matmul(x, g, w, tm):
    n, d = x.shape
    dout = w.shape[1]
    return pl.pallas_call(
        _norm_matmul_kernel,
        grid=(n // tm,),
        in_specs=[pl.BlockSpec((tm, d), lambda i: (i, 0)),
                  pl.BlockSpec((1, d), lambda i: (0, 0)),
                  pl.BlockSpec((d, dout), lambda i: (0, 0))],
        out_specs=pl.BlockSpec((tm, dout), lambda i: (i, 0)),
        out_shape=jax.ShapeDtypeStruct((n, dout), F32),
        compiler_params=_cparams("parallel"),
        name="norm_in_proj",
    )(x, g, w)


def _block_ones(n):
    r = lax.broadcasted_iota(jnp.int32, (n, n), 0) // HEAD_DIM
    c = lax.broadcasted_iota(jnp.int32, (n, n), 1) // HEAD_DIM
    return (r == c).astype(BF16)


def _head_rms_scale(z, ones):
    hi, lo = _split2(z * z)
    ss = jnp.dot(hi, ones, preferred_element_type=F32) + jnp.dot(lo, ones, preferred_element_type=F32)
    return lax.rsqrt(ss * (1.0 / HEAD_DIM) + NORM_EPS)


def _attn_kernel(sink_ref, q_ref, kv_ref, prev_ref, bias_ref, qg_ref, kg_ref, ya_ref, kn_ref,
                 *, rq, n_part, prev_is_raw):
    i = pl.program_id(1)
    kp = WINDOW + rq
    gw = A_GROUP * HEAD_DIM
    q = q_ref[0]
    kv = kv_ref[0]
    pv = prev_ref[0]
    ones_q = _block_ones(D_A)
    ones_k = _block_ones(D_KV)
    qn = (q * _head_rms_scale(q, ones_q) * qg_ref[...]).astype(BF16)
    k_cur = kv[:, 0:D_KV]
    k_cur = k_cur * _head_rms_scale(k_cur, ones_k) * kg_ref[...]
    kn_ref[0] = k_cur
    k_prev = pv[:, 0:D_KV]
    if prev_is_raw:
        k_prev = k_prev * _head_rms_scale(k_prev, ones_k) * kg_ref[...]
    k_all = jnp.concatenate([k_prev, k_cur], axis=0).astype(BF16)
    v_all = jnp.concatenate([pv[:, D_KV:], kv[:, D_KV:]], axis=0).astype(BF16)
    nq = A_GROUP * rq
    lane_in = lax.broadcasted_iota(jnp.int32, (D_KV, gw), 0)
    lane_out = lax.broadcasted_iota(jnp.int32, (D_KV, gw), 1)
    lane_out_t = lax.broadcasted_iota(jnp.int32, (gw, D_KV), 0)
    lane_in_t = lax.broadcasted_iota(jnp.int32, (gw, D_KV), 1)
    slot_lane = lax.broadcasted_iota(jnp.int32, (1, gw), 1) // HEAD_DIM
    slot_mask_bf = [(slot_lane == hh).astype(BF16) for hh in range(A_GROUP)]
    slot_row = lax.broadcasted_iota(jnp.int32, (gw, rq), 0) // HEAD_DIM
    key_row = lax.broadcasted_iota(jnp.int32, (kp, nq), 0)
    col_head = lax.broadcasted_iota(jnp.int32, (1, nq), 1) // rq
    for g in range(A_KV_HEADS):
        select = (lane_in // HEAD_DIM == g) & (lane_in % HEAD_DIM == lane_out % HEAD_DIM)
        k_wide = jnp.dot(k_all, select.astype(BF16), preferred_element_type=F32).astype(BF16)
        select_t = (lane_in_t // HEAD_DIM == g) & (lane_in_t % HEAD_DIM == lane_out_t % HEAD_DIM)
        v_wide_t = lax.dot_general(select_t.astype(BF16), v_all, (((1,), (1,)), ((), ())),
                                   preferred_element_type=F32).astype(BF16)
        q_g = qn[:, g * gw:(g + 1) * gw]
        sink_row = jnp.zeros((1, nq), F32)
        for hh in range(A_GROUP):
            sink_row = jnp.where(col_head == hh, sink_ref[g * A_GROUP + hh], sink_row)
        bias_t = bias_ref[g]
        for p in range(n_part):
            rows = slice(p * rq, (p + 1) * rq)
            qs = jnp.concatenate([q_g[rows] * m for m in slot_mask_bf], axis=0)
            kb = k_wide[p * rq:p * rq + kp]
            s = lax.dot_general(kb, qs, (((1,), (1,)), ((), ())), preferred_element_type=F32) + bias_t
            if prev_is_raw and p * rq < WINDOW:
                n_pad = WINDOW - (i * n_part + p) * rq
                s = s + jnp.where(key_row < n_pad, NEG_INF, 0.0)
            m = jnp.maximum(jnp.max(s, axis=0, keepdims=True), sink_row)
            e = jnp.exp(s - m)
            den = jnp.sum(e, axis=0, keepdims=True) + jnp.exp(sink_row - m)
            prob = (e * (1.0 / den)).astype(BF16)
            o_t = jnp.dot(v_wide_t[:, p * rq:p * rq + kp], prob, preferred_element_type=F32)
            out_t = o_t[:, (A_GROUP - 1) * rq:]
            for hh in range(A_GROUP - 2, -1, -1):
                out_t = jnp.where(slot_row == hh, o_t[:, hh * rq:(hh + 1) * rq], out_t)
            ya_ref[0, rows, g * gw:(g + 1) * gw] = out_t.T.astype(BF16)


def _attn_bias(rq, cq):
    kp = WINDOW + rq
    i = jnp.arange(rq)[:, None]
    j = jnp.arange(kp)[None, :]
    jb = j - cq * (i // cq)
    valid = (jb >= 0) & (jb < WINDOW + cq)
    dist = jnp.abs(WINDOW + (i % cq) - jb).astype(F32)
    slopes = jnp.exp2(-8.0 * jnp.arange(1, A_HEADS + 1, dtype=F32) / A_HEADS)
    bias = jnp.where(valid[None], -slopes[:, None, None] * dist[None], NEG_INF)
    return jnp.swapaxes(bias.reshape(A_KV_HEADS, A_GROUP * rq, kp), 1, 2)


def _attention(proj, cache_kv, q_gain, k_gain, sink, cq, rq, n_part):
    b, t, _ = proj.shape
    tq = rq * n_part
    prompt = cache_kv is None
    kv_col = D_A // (2 * D_KV)
    if prompt:
        prev_arr = proj
        prev_spec = pl.BlockSpec((1, WINDOW, 2 * D_KV),
                                 lambda bi, i: (bi, jnp.maximum(i * (tq // WINDOW) - 1, 0), kv_col))
    else:
        prev_arr = cache_kv
        prev_spec = pl.BlockSpec((1, WINDOW, 2 * D_KV), lambda bi, i: (bi, 0, 0))
    kp = WINDOW + rq
    q_gain_t = jnp.tile(q_gain * (HEAD_DIM ** -0.5), (1, A_HEADS))
    k_gain_t = jnp.tile(k_gain, (1, A_KV_HEADS))
    kern = functools.partial(_attn_kernel, rq=rq, n_part=n_part, prev_is_raw=prompt)
    return pl.pallas_call(
        kern,
        grid=(b, t // tq),
        in_specs=[pl.BlockSpec(memory_space=pltpu.SMEM),
                  pl.BlockSpec((1, tq, D_A), lambda bi, i: (bi, i, 0)),
                  pl.BlockSpec((1, tq, 2 * D_KV), lambda bi, i: (bi, i, kv_col)),
                  prev_spec,
                  pl.BlockSpec((A_KV_HEADS, kp, A_GROUP * rq), lambda bi, i: (0, 0, 0)),
                  pl.BlockSpec((1, D_A), lambda bi, i: (0, 0)),
                  pl.BlockSpec((1, D_KV), lambda bi, i: (0, 0))],
        out_specs=[pl.BlockSpec((1, tq, D_A), lambda bi, i: (bi, i, 0)),
                   pl.BlockSpec((1, tq, D_KV), lambda bi, i: (bi, i, 0))],
        out_shape=[jax.ShapeDtypeStruct((b, t, D_A), BF16),
                   jax.ShapeDtypeStruct((b, t, D_KV), F32)],
        compiler_params=_cparams("parallel", "arbitrary"),
        name="swa_attention",
    )(sink, proj, proj, prev_arr, _attn_bias(rq, cq), q_gain_t, k_gain_t)


def _pool_kernel(u_ref, halo_ref, prefix_ref, w_ref, scale_ref, yb_ref, *, n_prefix):
    i = pl.program_id(1)
    u = u_ref[0]
    tb = u.shape[0]
    halo = jnp.where(i == 0, prefix_ref[0], halo_ref[0])
    ext = jnp.concatenate([halo, u], axis=0)
    col = lax.broadcasted_iota(jnp.int32, (1, D_POOL), 1)
    pos = i * tb + lax.broadcasted_iota(jnp.int32, (tb, 1), 0)
    total = None
    count = None
    acc = ext
    span = 1
    for gi, w in enumerate(POOL_WINDOWS):
        while span < w:
            acc = acc + pltpu.roll(acc, span, axis=0)
            span *= 2
        in_group = (col >= gi * POOL_GW) & (col < (gi + 1) * POOL_GW)
        tail = acc[POOL_HALO:]
        total = jnp.where(in_group, tail, 0.0 if total is None else total)
        cnt = jnp.minimum(pos + (1 + n_prefix), w).astype(F32)
        count = jnp.where(in_group, cnt, 1.0 if count is None else count)
    d = total / count - u
    yb_ref[0] = (_mm(d, w_ref[...]) * scale_ref[...]).astype(BF16)


def _pool_mix(proj, prefix, w_blockdiag, scale, n_prefix, tb):
    b, t, _ = proj.shape
    col = (D_A + 2 * D_KV) // D_POOL
    kern = functools.partial(_pool_kernel, n_prefix=n_prefix)
    return pl.pallas_call(
        kern,
        grid=(b, t // tb),
        in_specs=[pl.BlockSpec((1, tb, D_POOL), lambda bi, i: (bi, i, col)),
                  pl.BlockSpec((1, POOL_HALO, D_POOL),
                               lambda bi, i: (bi, jnp.maximum(i * (tb // POOL_HALO) - 1, 0), col)),
                  pl.BlockSpec((1, POOL_HALO, D_POOL), lambda bi, i: (bi, 0, 0)),
                  pl.BlockSpec((D_POOL, D_POOL), lambda bi, i: (0, 0)),
                  pl.BlockSpec((1, D_POOL), lambda bi, i: (0, 0))],
        out_specs=pl.BlockSpec((1, tb, D_POOL), lambda bi, i: (bi, i, 0)),
        out_shape=jax.ShapeDtypeStruct((b, t, D_POOL), BF16),
        compiler_params=_cparams("parallel", "arbitrary"),
        name="pool_mix",
    )(proj, proj, prefix, w_blockdiag, scale)


def _rwkv_kernel(p_ref, prev_ref, st0_ref, mu_ref, w0_ref, a0_ref, w2_ref, a2_ref, g2_ref,
                 kk_ref, ka_ref, rk_ref, lng_ref, lnb_ref, yc_ref, st_ref, carry_ref, y_ref,
                 *, chunk):
    j = pl.program_id(1)

    @pl.when(j == 0)
    def _():
        carry_ref[...] = prev_ref[0]
        st_ref[0] = st0_ref[0]

    p = p_ref[0]
    tb = p.shape[0]
    n_chunk = tb // chunk
    hl = R_HEADS * chunk

    row = lax.broadcasted_iota(jnp.int32, (tb, 1), 0)
    p_prev = jnp.where(row == 0, carry_ref[...], pltpu.roll(p, 1, axis=0))
    carry_ref[...] = p[tb - 1:tb]
    xs = p + mu_ref[...] * (p_prev - p)
    r = xs[:, 0:D_R]
    k = xs[:, D_R:2 * D_R]
    v = xs[:, 2 * D_R:3 * D_R]
    wa = xs[:, 3 * D_R:3 * D_R + R_WA]
    gd = xs[:, 3 * D_R + R_WA:]

    z = -(w0_ref[...] + _mm(jnp.tanh(wa), w2_ref[...]))
    softplus = jnp.maximum(z, 0.0) + jnp.log(1.0 + jnp.exp(-jnp.abs(z)))
    lw = -jnp.exp(-softplus - 0.5)
    a = _sigmoid(a0_ref[...] + _mm(wa, a2_ref[...]))
    g = _mm(_sigmoid(gd), g2_ref[...])

    lane_r = lax.broadcasted_iota(jnp.int32, (D_R, D_R), 0) // HEAD_DIM
    lane_c = lax.broadcasted_iota(jnp.int32, (D_R, D_R), 1) // HEAD_DIM
    head_ones = (lane_r == lane_c).astype(F32)
    seg_sum = lambda t: _mm_exact_rhs(t, head_ones)

    kk = k * kk_ref[...]
    kk = kk / jnp.maximum(jnp.sqrt(seg_sum(kk * kk)), 1e-12)
    k2 = k * (1.0 + (a - 1.0) * ka_ref[...])
    bb = kk * a

    ti = lax.broadcasted_iota(jnp.int32, (tb, tb), 0)
    tj = lax.broadcasted_iota(jnp.int32, (tb, tb), 1)
    cum = _mm_exact_lhs(((ti // chunk == tj // chunk) & (tj <= ti)).astype(F32), lw)

    wi = lax.broadcasted_iota(jnp.int32, (chunk, hl), 0)
    wj = lax.broadcasted_iota(jnp.int32, (chunk, hl), 1) % chunk
    strict = wj < wi
    incl = wj <= wi
    eye_w = (wj == wi).astype(F32)
    diag_k = (lax.broadcasted_iota(jnp.int32, (HEAD_DIM, D_R), 0)
              == lax.broadcasted_iota(jnp.int32, (HEAD_DIM, D_R), 1) % HEAD_DIM)
    lane_k = lax.broadcasted_iota(jnp.int32, (1, D_R), 1) // HEAD_DIM
    lane_t = lax.broadcasted_iota(jnp.int32, (1, hl), 1) // chunk
    mask_k = [(lane_k == h).astype(F32) for h in range(R_HEADS)]
    mask_k_bf = [m.astype(BF16) for m in mask_k]
    mask_t_bf = [(lane_t == h).astype(BF16) for h in range(R_HEADS)]

    def blockdiag(t, masks):
        t16 = t.astype(BF16)
        return jnp.concatenate([t16 * m for m in masks], axis=0)

    def wide_transpose(t):
        tt = jnp.concatenate([t * m for m in mask_k], axis=0).T
        out = tt[0:HEAD_DIM]
        for h in range(1, R_HEADS):
            out = out + tt[h * HEAD_DIM:(h + 1) * HEAD_DIM]
        return out.astype(BF16)

    chunks = []
    for c in range(n_chunk):
        sl = slice(c * chunk, (c + 1) * chunk)
        cum_c = cum[sl]
        cum_last = cum_c[chunk - 1:chunk]
        g_in = jnp.exp(cum_c)
        g_prev = jnp.exp(cum_c - lw[sl])
        g_inv = jnp.exp(-cum_c)
        g_out = jnp.exp(cum_last - cum_c)
        a_n = (kk[sl] * g_prev).astype(BF16)
        r_n = r[sl] * g_in
        ch = dict(sl=sl, r_n=r_n, g_last=jnp.exp(cum_last),
                  a_s=blockdiag(a_n, mask_k_bf),
                  v_s=blockdiag(v[sl], mask_k_bf),
                  bo_w=wide_transpose(bb[sl] * g_out),
                  ko_w=wide_transpose(k2[sl] * g_out))
        ar = jnp.concatenate([a_n, r_n.astype(BF16)], axis=0)
        m_b = _mm_nt(ar, blockdiag(bb[sl] * g_inv, mask_k_bf))
        m_k = _mm_nt(ar, blockdiag(k2[sl] * g_inv, mask_k_bf))
        m_ab = jnp.where(strict, m_b[:chunk], 0.0)
        ch.update(m_rb=jnp.where(incl, m_b[chunk:], 0.0).astype(BF16),
                  m_ak=jnp.where(strict, m_k[:chunk], 0.0).astype(BF16),
                  m_rk=jnp.where(incl, m_k[chunk:], 0.0).astype(BF16),
                  t_inv=eye_w - m_ab, pw=m_ab.astype(BF16))
        chunks.append(ch)

    for ch in chunks:
        ch["pw"] = _mm(ch["pw"], blockdiag(ch["pw"], mask_t_bf)).astype(BF16)
    n = 2
    while n < chunk:
        for ch in chunks:
            rhs = blockdiag(ch["pw"], mask_t_bf)
            if 2 * n < chunk:
                both = _mm(jnp.concatenate([ch["pw"], ch["t_inv"].astype(BF16)], axis=0), rhs)
                ch["pw"] = both[:chunk].astype(BF16)
                ch["t_inv"] = ch["t_inv"] + both[chunk:]
            else:
                ch["t_inv"] = ch["t_inv"] + _mm(ch["t_inv"], rhs)
        n *= 2

    for ch in chunks:
        t_inv = ch["t_inv"].astype(BF16)
        ch["a_bar"] = blockdiag(_mm(t_inv, ch["a_s"]), mask_k_bf)
        on_v = _mm(jnp.concatenate([ch["m_ak"], ch["m_rk"], ch["ko_w"]], axis=0), ch["v_s"])
        ch["mrk_v"], ch["ko_v"] = on_v[chunk:2 * chunk], on_v[2 * chunk:]
        ch["u0"] = blockdiag(-_mm(t_inv, blockdiag(on_v[:chunk], mask_k_bf)), mask_k_bf)
    for ch in chunks:
        lhs = jnp.concatenate([ch["m_rb"], ch["bo_w"]], axis=0)
        on_a = _mm(lhs, ch["a_bar"])
        on_u = _mm(lhs, ch["u0"])
        ch["r_bar"] = (ch["r_n"] - on_a[:chunk]).astype(BF16)
        ch["y0"] = ch["mrk_v"] + on_u[:chunk]
        ch["g_w"] = (jnp.where(diag_k, ch["g_last"], 0.0) - on_a[chunk:]).astype(BF16)
        ch["h_w"] = on_u[chunk:] + ch["ko_v"]

    st_w = st_ref[0]
    for ch in chunks:
        on_st = _mm(jnp.concatenate([ch["r_bar"], ch["g_w"]], axis=0), blockdiag(st_w, mask_k_bf))
        y_ref[ch["sl"], :] = ch["y0"] + on_st[:chunk]
        st_w = on_st[chunk:] + ch["h_w"]
    st_ref[0] = st_w

    y = y_ref[...]
    mean = seg_sum(y) * (1.0 / HEAD_DIM)
    d = y - mean
    var = seg_sum(d * d) * (1.0 / HEAD_DIM)
    yn = d * lax.rsqrt(var + GN_EPS) * lng_ref[...] + lnb_ref[...]
    bonus = seg_sum(r * k2 * rk_ref[...]) * v
    yc_ref[0] = ((yn + bonus) * g).astype(BF16)


def _rwkv_mix(proj, prev, st0, lp, chunk, tb):
    b, t, _ = proj.shape
    col = (D_A + 2 * D_KV + D_POOL) // D_R_IN
    row = lambda n: pl.BlockSpec((1, n), lambda bi, i: (0, 0))
    full = lambda s: pl.BlockSpec(s, lambda bi, i: (0,) * len(s))
    kern = functools.partial(_rwkv_kernel, chunk=chunk)
    return pl.pallas_call(
        kern,
        grid=(b, t // tb),
        in_specs=[pl.BlockSpec((1, tb, D_R_IN), lambda bi, i: (bi, i, col)),
                  pl.BlockSpec((1, 1, D_R_IN), lambda bi, i: (bi, 0, 0)),
                  pl.BlockSpec((1, HEAD_DIM, D_R), lambda bi, i: (bi, 0, 0)),
                  row(D_R_IN), row(D_R), row(D_R),
                  full((R_WA, D_R)), full((R_WA, D_R)), full((R_G, D_R)),
                  row(D_R), row(D_R), row(D_R), row(D_R), row(D_R)],
        out_specs=[pl.BlockSpec((1, tb, D_R), lambda bi, i: (bi, i, 0)),
                   pl.BlockSpec((1, HEAD_DIM, D_R), lambda bi, i: (bi, 0, 0))],
        out_shape=[jax.ShapeDtypeStruct((b, t, D_R), BF16),
                   jax.ShapeDtypeStruct((b, HEAD_DIM, D_R), F32)],
        scratch_shapes=[pltpu.VMEM((1, D_R_IN), F32), pltpu.VMEM((tb, D_R), F32)],
        compiler_params=_cparams("parallel", "arbitrary"),
        name="rwkv7_mix",
    )(proj, prev, st0, lp["mu"], lp["w0"], lp["a0"], lp["w2"], lp["a2"], lp["g2"],
      lp["k_k"], lp["k_a"], lp["r_k"], lp["lnx_g"], lp["lnx_b"])


def _state_to_wide(s):
    b = s.shape[0]
    return jnp.transpose(s, (0, 3, 1, 2)).reshape(b, HEAD_DIM, D_R)


def _wide_to_state(st):
    b = st.shape[0]
    return jnp.transpose(st.reshape(b, HEAD_DIM, R_HEADS, HEAD_DIM), (0, 2, 3, 1))


def _mix_out_kernel(x_ref, ya_ref, yb_ref, yc_ref, w_ref, g_ref, *rest, route):
    if route:
        rw_ref, xm_ref, h_ref, gate_ref = rest
    else:
        xm_ref, h_ref = rest
    dot = lambda u, lo, hi: jnp.dot(u[...], w_ref[lo:hi, :], preferred_element_type=F32)
    xm = (x_ref[...] + dot(ya_ref, 0, D_A) + dot(yb_ref, D_A, D_A + D_POOL)
          + dot(yc_ref, D_A + D_POOL, D_A + D_POOL + D_R))
    xm_ref[...] = xm
    h = xm * lax.rsqrt(jnp.mean(xm * xm, axis=-1, keepdims=True) + NORM_EPS) * g_ref[...]
    h_hi = h.astype(BF16)
    h_ref[...] = h_hi
    if route:
        rw_t = rw_ref[...]
        n_e = rw_t.shape[0]
        rw_hi = rw_t.astype(BF16).astype(F32)
        rw2 = jnp.concatenate([rw_hi, rw_t - rw_hi], axis=0).astype(BF16)
        h_lo = (h - h_hi.astype(F32)).astype(BF16)
        nt = lambda a, b: lax.dot_general(a, b, (((1,), (1,)), ((), ())), preferred_element_type=F32)
        on_hi = nt(rw2, h_hi)
        logits = on_hi[:n_e] + (on_hi[n_e:] + nt(rw2, h_lo)[:n_e])
        row = lax.broadcasted_iota(jnp.int32, logits.shape, 0).astype(F32)
        m1 = jnp.max(logits, axis=0, keepdims=True)
        i1 = jnp.min(jnp.where(logits == m1, row, float(n_e)), axis=0, keepdims=True)
        rest_l = jnp.where(row == i1, -jnp.inf, logits)
        m2 = jnp.max(rest_l, axis=0, keepdims=True)
        i2 = jnp.min(jnp.where(rest_l == m2, row, float(n_e)), axis=0, keepdims=True)
        e2 = jnp.exp(m2 - m1)
        gates_t = (jnp.where(row == i1, 1.0 / (1.0 + e2), 0.0)
                   + jnp.where(row == i2, e2 / (1.0 + e2), 0.0))
        gate_ref[...] = gates_t.T


def _mix_out(x, ya, yb, yc, w_out, g, router_w, tm):
    n, d = x.shape
    route = router_w is not None
    tile = lambda w: pl.BlockSpec((tm, w), lambda i: (i, 0))
    in_specs = [tile(d), tile(D_A), tile(D_POOL), tile(D_R),
                pl.BlockSpec(w_out.shape, lambda i: (0, 0)),
                pl.BlockSpec((1, d), lambda i: (0, 0))]
    out_specs = [tile(d), tile(d)]
    out_shape = [jax.ShapeDtypeStruct((n, d), F32), jax.ShapeDtypeStruct((n, d), BF16)]
    args = [x, ya, yb, yc, w_out, g]
    if route:
        n_e = router_w.shape[1]
        in_specs.append(pl.BlockSpec((n_e, d), lambda i: (0, 0)))
        out_specs.append(tile(n_e))
        out_shape.append(jax.ShapeDtypeStruct((n, n_e), F32))
        args.append(router_w.T)
    return pl.pallas_call(
        functools.partial(_mix_out_kernel, route=route),
        grid=(n // tm,),
        in_specs=in_specs, out_specs=out_specs, out_shape=out_shape,
        compiler_params=_cparams("parallel"),
        name="mix_out_proj",
    )(*args)


def _swiglu_act(h, wg, wu):
    return (_silu(jnp.dot(h, wg, preferred_element_type=F32))
            * jnp.dot(h, wu, preferred_element_type=F32)).astype(BF16)


def _down_proj(act_ref, wd, tf):
    out = None
    for jj in range(act_ref.shape[0]):
        part = jnp.dot(act_ref[jj], wd[jj * tf:(jj + 1) * tf, :], preferred_element_type=F32)
        out = part if out is None else out + part
    return out


def _ffn_kernel(h_ref, xm_ref, wg_ref, wu_ref, wd_ref, o_ref, act_ref):
    j = pl.program_id(1)
    tf = wg_ref.shape[1]
    act_ref[j] = _swiglu_act(h_ref[...], wg_ref[...], wu_ref[...])

    @pl.when(j == pl.num_programs(1) - 1)
    def _():
        o_ref[...] = xm_ref[...] + _down_proj(act_ref, wd_ref, tf)


def _ffn(h, xm, wg, wu, wd, tm, tf):
    n, d = xm.shape
    f = wg.shape[1]
    return pl.pallas_call(
        _ffn_kernel,
        grid=(n // tm, f // tf),
        in_specs=[pl.BlockSpec((tm, d), lambda i, j: (i, 0)),
                  pl.BlockSpec((tm, d), lambda i, j: (i, 0)),
                  pl.BlockSpec((d, tf), lambda i, j: (0, j)),
                  pl.BlockSpec((d, tf), lambda i, j: (0, j)),
                  pl.BlockSpec((f, d), lambda i, j: (0, 0))],
        out_specs=pl.BlockSpec((tm, d), lambda i, j: (i, 0)),
        out_shape=jax.ShapeDtypeStruct((n, d), F32),
        scratch_shapes=[pltpu.VMEM((f // tf, tm, tf), BF16)],
        compiler_params=_cparams("parallel", "arbitrary"),
        name="swiglu_ffn",
    )(h, xm, wg, wu, wd)


def _moe_kernel(h_ref, xm_ref, gate_ref, wg_ref, wu_ref, wd_ref, o_ref, acc_ref):
    e = pl.program_id(1)
    j = pl.program_id(2)

    @pl.when((e == 0) & (j == 0))
    def _():
        acc_ref[...] = xm_ref[...]

    gates = gate_ref[...]
    lane = lax.broadcasted_iota(jnp.int32, gates.shape, 1)
    gate = jnp.sum(jnp.where(lane == e, gates, 0.0), axis=-1, keepdims=True)
    h = h_ref[...]
    act = (_silu(jnp.dot(h, wg_ref[0], preferred_element_type=F32))
           * jnp.dot(h, wu_ref[0], preferred_element_type=F32))
    acc_ref[...] += gate * jnp.dot(act.astype(BF16), wd_ref[0], preferred_element_type=F32)

    @pl.when((e == pl.num_programs(1) - 1) & (j == pl.num_programs(2) - 1))
    def _():
        o_ref[...] = acc_ref[...]


def _moe(h, xm, gates, wg, wu, wd, tm, tf):
    n, d = xm.shape
    n_e, _, f = wg.shape
    return pl.pallas_call(
        _moe_kernel,
        grid=(n // tm, n_e, f // tf),
        in_specs=[pl.BlockSpec((tm, d), lambda i, e, j: (i, 0)),
                  pl.BlockSpec((tm, d), lambda i, e, j: (i, 0)),
                  pl.BlockSpec((tm, n_e), lambda i, e, j: (i, 0)),
                  pl.BlockSpec((1, d, tf), lambda i, e, j: (e, 0, j)),
                  pl.BlockSpec((1, d, tf), lambda i, e, j: (e, 0, j)),
                  pl.BlockSpec((1, tf, d), lambda i, e, j: (e, j, 0))],
        out_specs=pl.BlockSpec((tm, d), lambda i, e, j: (i, 0)),
        out_shape=jax.ShapeDtypeStruct((n, d), F32),
        scratch_shapes=[pltpu.VMEM((tm, d), F32)],
        compiler_params=_cparams("parallel", "arbitrary", "arbitrary"),
        name="moe_ffn",
    )(h, xm, gates, wg, wu, wd)


MOE_ROW_ALIGN = 32
MOE_GATE_LANES = 128
MOE_TILE = 1024
MOE_GROUP_ROWS = 1024
_UNSELECTED = 1e9


def _tile_routing(gates_sel, rank, axis):
    n_e = rank.shape[axis]
    cnt = jnp.max(rank, axis=1 - axis, keepdims=True)
    padded = jnp.floor((cnt + (MOE_ROW_ALIGN - 1)) * (1.0 / MOE_ROW_ALIGN)) * MOE_ROW_ALIGN
    offs, run = [], jnp.zeros((1, 1), F32)
    for e in range(n_e):
        offs.append(run)
        run = run + (padded[e:e + 1] if axis == 0 else padded[:, e:e + 1])
    off = jnp.concatenate(offs, axis=axis)
    dest = jnp.where(gates_sel, off + rank - 1.0, -1.0)
    d_hi = jnp.max(dest, axis=axis, keepdims=True)
    d_lo = jnp.min(jnp.where(gates_sel, dest, _UNSELECTED), axis=axis, keepdims=True)
    d_lo = jnp.where(d_lo == d_hi, -2.0, d_lo)
    return cnt, dest, d_hi, d_lo


def _moe_dispatch_kernel(h_ref, gate_ref, hs_ref, cnt_ref, *, chunk_rows):
    h = h_ref[...]
    gates = gate_ref[...]
    tm, n_e = gates.shape
    d = h.shape[1]
    mt = hs_ref.shape[0]
    eye = (lax.broadcasted_iota(jnp.int32, (n_e, n_e), 0)
           == lax.broadcasted_iota(jnp.int32, (n_e, n_e), 1)).astype(BF16)
    to_rows = lambda u: lax.dot_general(eye, u, (((1,), (1,)), ((), ())), preferred_element_type=F32)
    g_hi3, g_mid3, g_lo3 = _split3(gates)
    gate_row = to_rows(g_hi3) + to_rows(g_mid3) + to_rows(g_lo3)
    sel_row = gate_row > 0.0
    ti = lax.broadcasted_iota(jnp.int32, (tm, tm), 0)
    tj = lax.broadcasted_iota(jnp.int32, (tm, tm), 1)
    rank = jnp.dot(sel_row.astype(BF16), (ti <= tj).astype(BF16), preferred_element_type=F32)
    cnt, dest, d_hi, d_lo = _tile_routing(sel_row, rank, 0)
    cnt_ref[0] = jnp.broadcast_to(cnt, cnt_ref.shape[1:])
    g_hi = jnp.sum(jnp.where(dest == d_hi, gate_row, 0.0), axis=0, keepdims=True)
    g_lo = jnp.sum(jnp.where(dest == d_lo, gate_row, 0.0), axis=0, keepdims=True)
    lane = lax.broadcasted_iota(jnp.int32, (chunk_rows, MOE_GATE_LANES), 1)
    for c0 in range(0, mt, chunk_rows):
        rho = (c0 + lax.broadcasted_iota(jnp.int32, (chunk_rows, 1), 0)).astype(F32)
        is_hi = d_hi == rho
        is_lo = d_lo == rho
        onehot = jnp.where(is_hi, 1.0, jnp.where(is_lo, 1.0, 0.0)).astype(BF16)
        rows = jnp.dot(onehot, h, preferred_element_type=F32)
        g = jnp.sum(jnp.where(is_hi, g_hi, jnp.where(is_lo, g_lo, 0.0)), axis=1, keepdims=True)
        p0, p1, p2 = [t.astype(F32) for t in _split3(g)]
        gcols = jnp.where(lane == 0, p0, jnp.where(lane == 1, p1, jnp.where(lane == 2, p2, 0.0)))
        hs_ref[c0:c0 + chunk_rows, 0:d] = rows.astype(BF16)
        hs_ref[c0:c0 + chunk_rows, d:d + MOE_GATE_LANES] = gcols.astype(BF16)


def _moe_dispatch(h, gates, tm, mt):
    n, d = h.shape
    n_e = gates.shape[1]
    n_tiles = n // tm
    return pl.pallas_call(
        functools.partial(_moe_dispatch_kernel, chunk_rows=256),
        grid=(n_tiles,),
        in_specs=[pl.BlockSpec((tm, d), lambda i: (i, 0)),
                  pl.BlockSpec((tm, n_e), lambda i: (i, 0))],
        out_specs=[pl.BlockSpec((mt, d + MOE_GATE_LANES), lambda i: (i, 0)),
                   pl.BlockSpec((1, n_e, 128), lambda i: (i, 0, 0))],
        out_shape=[jax.ShapeDtypeStruct((n_tiles * mt, d + MOE_GATE_LANES), BF16),
                   jax.ShapeDtypeStruct((n_tiles, n_e, 128), F32)],
        compiler_params=_cparams("parallel"),
        name="moe_dispatch",
    )(h, gates)


def _moe_tables(cnt, mt, group_rows):
    i32 = jnp.int32
    n_tiles, n_e = cnt.shape
    bpt = mt // MOE_ROW_ALIGN
    g = group_rows // MOE_ROW_ALIGN
    nblk = jnp.ceil(cnt / MOE_ROW_ALIGN).astype(i32)
    cum = jnp.cumsum(nblk, axis=1)
    off_blk = cum - nblk
    src_base = jnp.arange(n_tiles, dtype=i32)[:, None] * bpt + off_blk
    tot = jnp.sum(nblk, axis=0)
    totp = ((tot + g - 1) // g) * g
    es = jnp.cumsum(totp) - totp
    nblk_t = nblk.T
    seg_start = es[:, None] + jnp.cumsum(nblk_t, axis=1) - nblk_t
    n_src = n_tiles * bpt
    n_dst = -(-(n_src + n_e * (g - 1)) // g) * g
    dblk = jnp.arange(n_dst, dtype=i32)[:, None]
    s0, sl, ss = seg_start.reshape(1, -1), nblk_t.reshape(1, -1), src_base.T.reshape(1, -1)
    hit = (dblk >= s0) & (dblk < s0 + sl)
    src_of_dst = jnp.sum(jnp.where(hit, ss + dblk - s0, 0), axis=1)
    n_rb = n_dst // g
    eid = jnp.sum(jnp.arange(n_rb, dtype=i32)[:, None] * g >= es[None, :], axis=1).astype(i32) - 1
    n_valid = ((es[-1] + totp[-1]) // g).reshape(1)
    o_of = jnp.arange(bpt, dtype=i32)[None, :, None]
    off3, len3 = off_blk[:, None, :], nblk[:, None, :]
    hit3 = (o_of >= off3) & (o_of < off3 + len3)
    dst_of_src = jnp.sum(jnp.where(hit3, seg_start.T[:, None, :] + o_of - off3, 0), axis=2).reshape(-1)
    return src_of_dst.astype(i32), dst_of_src.astype(i32), eid, n_valid.astype(i32), n_dst


def _block_gather_kernel(idx_ref, *refs):
    del idx_ref
    out_ref = refs[-1]
    rows = refs[0].shape[0]
    for q, src_ref in enumerate(refs[:-1]):
        out_ref[q * rows:(q + 1) * rows, :] = src_ref[...]


def _block_gather(idx, src, n_blocks, per_step, name):
    rows = MOE_ROW_ALIGN
    width = src.shape[1]
    pick = lambda q: pl.BlockSpec((rows, width), lambda i, idx: (idx[i * per_step + q], 0))
    return pl.pallas_call(
        _block_gather_kernel,
        grid_spec=pltpu.PrefetchScalarGridSpec(
            num_scalar_prefetch=1, grid=(n_blocks // per_step,),
            in_specs=[pick(q) for q in range(per_step)],
            out_specs=pl.BlockSpec((per_step * rows, width), lambda i, idx: (i, 0))),
        out_shape=jax.ShapeDtypeStruct((n_blocks * rows, width), src.dtype),
        compiler_params=_cparams("arbitrary"),
        name=name,
    )(idx, *([src] * per_step))


def _moe_group_kernel(eid_ref, nv_ref, hs_ref, wg_ref, wu_ref, wd_ref, ys_ref, act_ref):
    del eid_ref
    i = pl.program_id(0)
    j = pl.program_id(1)
    d = ys_ref.shape[1]
    tf = wg_ref.shape[2]

    @pl.when(i < nv_ref[0])
    def _():
        act_ref[j] = _swiglu_act(hs_ref[:, 0:d], wg_ref[0], wu_ref[0])

        @pl.when(j == pl.num_programs(1) - 1)
        def _():
            gp = hs_ref[:, d:d + MOE_GATE_LANES].astype(F32)
            gate = gp[:, 0:1] + gp[:, 1:2] + gp[:, 2:3]
            ys_ref[...] = (_down_proj(act_ref, wd_ref[0], tf) * gate).astype(BF16)

    @pl.when((i >= nv_ref[0]) & (j == 0))
    def _():
        ys_ref[...] = jnp.zeros(ys_ref.shape, BF16)


def _moe_group_ffn(hs, eid, n_valid, wg, wu, wd, rb, tf):
    rows, dw = hs.shape
    d = dw - MOE_GATE_LANES
    f = wg.shape[2]
    nj = f // tf
    live_j = lambda i, j, nv: jnp.where(i < nv[0], j, nj - 1)
    return pl.pallas_call(
        _moe_group_kernel,
        grid_spec=pltpu.PrefetchScalarGridSpec(
            num_scalar_prefetch=2, grid=(rows // rb, nj),
            in_specs=[pl.BlockSpec((rb, dw), lambda i, j, eid, nv: (jnp.minimum(i, nv[0] - 1), 0)),
                      pl.BlockSpec((1, d, tf), lambda i, j, eid, nv: (eid[i], 0, live_j(i, j, nv))),
                      pl.BlockSpec((1, d, tf), lambda i, j, eid, nv: (eid[i], 0, live_j(i, j, nv))),
                      pl.BlockSpec((1, f, d), lambda i, j, eid, nv: (eid[i], 0, 0))],
            out_specs=pl.BlockSpec((rb, d), lambda i, j, eid, nv: (i, 0)),
            scratch_shapes=[pltpu.VMEM((nj, rb, tf), BF16)]),
        out_shape=jax.ShapeDtypeStruct((rows, d), BF16),
        compiler_params=_cparams("arbitrary", "arbitrary"),
        name="moe_group_ffn",
    )(eid, n_valid, hs, wg, wu, wd)


def _moe_combine_kernel(idx_ref, xm_ref, gate_ref, *refs):
    del idx_ref
    o_ref, ys_ref = refs[-2:]
    rows = refs[0].shape[0]
    for q, src_ref in enumerate(refs[:-2]):
        ys_ref[q * rows:(q + 1) * rows, :] = src_ref[...]
    gates = gate_ref[...]
    tm, n_e = gates.shape
    mt = ys_ref.shape[0]
    sel = gates > 0.0
    ti = lax.broadcasted_iota(jnp.int32, (tm, tm), 0)
    tj = lax.broadcasted_iota(jnp.int32, (tm, tm), 1)
    rank = jnp.dot((tj <= ti).astype(BF16), sel.astype(BF16), preferred_element_type=F32)
    _, _, d_hi, d_lo = _tile_routing(sel, rank, 1)
    rho = lax.broadcasted_iota(jnp.int32, (1, mt), 1).astype(F32)
    onehot = jnp.where(d_hi == rho, 1.0, jnp.where(d_lo == rho, 1.0, 0.0)).astype(BF16)
    o_ref[...] = xm_ref[...] + jnp.dot(onehot, ys_ref[...], preferred_element_type=F32)


def _moe_combine(xm, gates, ys_exp, dst_of_src, tm, mt):
    n, d = xm.shape
    n_e = gates.shape[1]
    bpt = mt // MOE_ROW_ALIGN
    piece = lambda q: pl.BlockSpec((MOE_ROW_ALIGN, d), lambda i, idx: (idx[i * bpt + q], 0))
    return pl.pallas_call(
        _moe_combine_kernel,
        grid_spec=pltpu.PrefetchScalarGridSpec(
            num_scalar_prefetch=1, grid=(n // tm,),
            in_specs=[pl.BlockSpec((tm, d), lambda i, idx: (i, 0)),
                      pl.BlockSpec((tm, n_e), lambda i, idx: (i, 0))] + [piece(q) for q in range(bpt)],
            out_specs=pl.BlockSpec((tm, d), lambda i, idx: (i, 0)),
            scratch_shapes=[pltpu.VMEM((mt, d), BF16)]),
        out_shape=jax.ShapeDtypeStruct((n, d), F32),
        compiler_params=_cparams("arbitrary"),
        name="moe_combine",
    )(dst_of_src, xm, gates, *([ys_exp] * bpt))


def _moe_sparse(h, xm, gates, wg, wu, wd, tf):
    n_e = gates.shape[1]
    tm = MOE_TILE
    mt = TOP_K * tm + n_e * MOE_ROW_ALIGN
    hs_tile, cnt = _moe_dispatch(h, gates, tm, mt)
    src_of_dst, dst_of_src, eid, n_valid, n_dst = _moe_tables(cnt[:, :, 0], mt, MOE_GROUP_ROWS)
    per_step = MOE_GROUP_ROWS // MOE_ROW_ALIGN
    hs_exp = _block_gather(src_of_dst, hs_tile, n_dst, per_step, "moe_gather_blocks")
    ys_exp = _moe_group_ffn(hs_exp, eid, n_valid, wg, wu, wd, MOE_GROUP_ROWS, tf)
    return _moe_combine(xm, gates, ys_exp, dst_of_src, tm, mt)


def _pick(n, candidates):
    for c in candidates:
        if n % c == 0:
            return c
    raise ValueError(f"no tile for {n}")


def _trunk(x, layers, cache_k, cache_v, state_pool, state_shift, state_wkv):
    prompt = cache_k is None
    b, t, d = x.shape
    n = b * t
    tm = _pick(n, (1024, 512, 256, 128))
    tm_ffn = _pick(n, (1024, 512, 256, 128))
    tf = _pick(layers[0]["wg"].shape[-1], (512, 256, 128))
    cq = CHUNK if prompt else t
    rq = _pick(t, (2 * CHUNK,)) if prompt else t
    n_part = _pick(t // rq, (4, 2, 1))
    pool_tb = _pick(t, (512, 256, 128, 64, 32))
    r_chunk = CHUNK if prompt else t
    r_tb = _pick(t, (512, 256, 128, 64, 32))
    x2 = x.reshape(n, d)
    nk, nv, npool, nshift, nwkv = [], [], [], [], []
    for l, lp in enumerate(layers):
        proj2 = _norm_matmul(x2, lp["norm1_g"], lp["w_in"], tm)
        proj = proj2.reshape(b, t, -1)
        c_pool = D_A + 2 * D_KV
        c_r = c_pool + D_POOL
        v_raw = proj[:, :, D_A + D_KV:c_pool]
        if prompt:
            cache_kv = None
            prefix = jnp.zeros((b, POOL_HALO, D_POOL), F32)
            prev = jnp.zeros((b, 1, D_R_IN), F32)
            st0 = jnp.zeros((b, HEAD_DIM, D_R), F32)
        else:
            cache_kv = jnp.concatenate([cache_k[l].reshape(b, WINDOW, D_KV),
                                        cache_v[l].reshape(b, WINDOW, D_KV)], axis=-1)
            prefix = jnp.pad(state_pool[l], ((0, 0), (POOL_HALO - POOL_CTX, 0), (0, 0)))
            prev = state_shift[l]
            st0 = _state_to_wide(state_wkv[l])
        ya, k_norm = _attention(proj, cache_kv, lp["q_gain"], lp["k_gain"], lp["sink"], cq, rq, n_part)
        yb = _pool_mix(proj, prefix, lp["pool_w"], lp["pool_scale"], 0 if prompt else POOL_CTX, pool_tb)
        yc, st_fin = _rwkv_mix(proj, prev, st0, lp, r_chunk, r_tb)
        keep = WINDOW if prompt else t
        nk.append(k_norm[:, t - keep:].reshape(b, keep, A_KV_HEADS, HEAD_DIM))
        nv.append(v_raw[:, t - keep:].reshape(b, keep, A_KV_HEADS, HEAD_DIM))
        npool.append(proj[:, t - POOL_CTX:, c_pool:c_r])
        nshift.append(proj[:, t - 1:, c_r:])
        nwkv.append(_wide_to_state(st_fin))
        outs = _mix_out(x2, ya.reshape(n, -1), yb.reshape(n, -1), yc.reshape(n, -1),
                        lp["w_out"], lp["norm2_g"], lp.get("router_w"), tm)
        if "router_w" in lp:
            xm, h2, gates = outs
            if n % MOE_TILE == 0 and n >= 2 * MOE_TILE:
                x2 = _moe_sparse(h2, xm, gates, lp["wg"], lp["wu"], lp["wd"], tf)
            else:
                x2 = _moe(h2, xm, gates, lp["wg"], lp["wu"], lp["wd"], tm_ffn, tf)
        else:
            xm, h2 = outs
            x2 = _ffn(h2, xm, lp["wg"], lp["wu"], lp["wd"], tm_ffn, tf)
    return (x2.reshape(b, t, d), jnp.stack(nk), jnp.stack(nv), jnp.stack(npool),
            jnp.stack(nshift), jnp.stack(nwkv))


def kernel(x_prompt, x_sample, cache_k, cache_v, state_pool, state_shift, state_wkv, norm1_g, w_in, q_gain, k_gain, attn_sink, pool_w, pool_scale, shift_mu, decay_w0, decay_w2, iclr_a0, iclr_a2, gate_g2, k_k, k_a, r_k, lnx_g, lnx_b, w_out, norm2_g, ffn_wg, ffn_wu, ffn_wd, router_w, moe_wg, moe_wu, moe_wd):
    depth = w_in.shape[0]
    r_w = decay_w2.shape[1]
    layers = []
    for l in range(depth):
        lp = dict(
            norm1_g=norm1_g[l][None], w_in=w_in[l].astype(BF16),
            q_gain=q_gain[l][None], k_gain=k_gain[l][None], sink=attn_sink[l],
            pool_w=jax.scipy.linalg.block_diag(*[pool_w[l, gi] for gi in range(len(POOL_WINDOWS))]).astype(BF16),
            pool_scale=pool_scale[l][None],
            mu=shift_mu[l][None], w0=decay_w0[l][None], a0=iclr_a0[l][None],
            w2=jnp.pad(decay_w2[l], ((0, R_WA - r_w), (0, 0))).astype(BF16),
            a2=jnp.pad(iclr_a2[l], ((r_w, 0), (0, 0))).astype(BF16),
            g2=gate_g2[l].astype(BF16),
            k_k=k_k[l][None], k_a=k_a[l][None], r_k=r_k[l].reshape(1, D_R),
            lnx_g=lnx_g[l][None], lnx_b=lnx_b[l][None],
            w_out=w_out[l].astype(BF16), norm2_g=norm2_g[l][None])
        if l % 2 == 0:
            lp.update(wg=ffn_wg[l // 2].astype(BF16), wu=ffn_wu[l // 2].astype(BF16),
                      wd=ffn_wd[l // 2].astype(BF16))
        else:
            lp.update(router_w=router_w[l // 2], wg=moe_wg[l // 2].astype(BF16),
                      wu=moe_wu[l // 2].astype(BF16), wd=moe_wd[l // 2].astype(BF16))
        layers.append(lp)
    y_p, pk, pv, ppool, pshift, pwkv = _trunk(x_prompt, layers, None, None, None, None, None)
    y_s, sk, sv, spool, sshift, swkv = _trunk(x_sample, layers, cache_k, cache_v, state_pool,
                                              state_shift, state_wkv)
    return (y_p, y_s, pk, pv, ppool, pshift, pwkv, sk, sv, spool, sshift, swkv)
```

```python
import functools

import jax
import jax.numpy as jnp
from jax import lax
from jax.experimental import pallas as pl
from jax.experimental.pallas import tpu as pltpu

F32 = jnp.float32
BF16 = jnp.bfloat16

HEAD_DIM = 64
A_HEADS = 8
A_KV_HEADS = 2
A_GROUP = A_HEADS // A_KV_HEADS
D_A = A_HEADS * HEAD_DIM
D_KV = A_KV_HEADS * HEAD_DIM
WINDOW = 128
CHUNK = 64
POOL_WINDOWS = (2, 4, 8, 16)
POOL_CTX = 15
POOL_HALO = 16
D_POOL = 256
POOL_GW = D_POOL // len(POOL_WINDOWS)
D_R = 256
R_HEADS = D_R // HEAD_DIM
R_WA = 128
R_G = 128
D_R_IN = 3 * D_R + R_WA + R_G
TOP_K = 2
NORM_EPS = 1e-6
GN_EPS = 64e-5
NEG_INF = -1e30
LOG2_E = 1.4426950408889634
VMEM_LIMIT_BYTES = 56 * 1024 * 1024


def _cparams(*sem):
    return pltpu.CompilerParams(dimension_semantics=sem, vmem_limit_bytes=VMEM_LIMIT_BYTES)


def _mm(a, b):
    return jnp.dot(a.astype(BF16), b.astype(BF16), preferred_element_type=F32)


def _mm_nt(a, b):
    return lax.dot_general(a.astype(BF16), b.astype(BF16), (((1,), (1,)), ((), ())),
                           preferred_element_type=F32)


def _split2(x):
    hi = x.astype(BF16)
    lo = (x - hi.astype(F32)).astype(BF16)
    return hi, lo


def _split3(x):
    hi = x.astype(BF16)
    r1 = x - hi.astype(F32)
    mid = r1.astype(BF16)
    lo = (r1 - mid.astype(F32)).astype(BF16)
    return hi, mid, lo


def _mm_exact_rhs(x, b):
    bb = b.astype(BF16)
    hi, lo = _split2(x)
    dot = lambda u: jnp.dot(u, bb, preferred_element_type=F32)
    return dot(hi) + dot(lo)


def _mm_exact_lhs(a, x):
    ab = a.astype(BF16)
    hi, lo = _split2(x)
    dot = lambda u: jnp.dot(ab, u, preferred_element_type=F32)
    return dot(hi) + dot(lo)


def _mm_hi(a, b):
    ah, al = _split2(a)
    bh, bl = _split2(b)
    dot = lambda u, v: jnp.dot(u, v, preferred_element_type=F32)
    return dot(ah, bh) + (dot(ah, bl) + dot(al, bh))


def _sigmoid(x):
    return 1.0 / (1.0 + jnp.exp(-x))


def _silu(x):
    return x * _sigmoid(x)


def _norm_matmul_kernel(x_ref, g_ref, w_ref, o_ref):
    x = x_ref[...]
    h = x * lax.rsqrt(jnp.mean(x * x, axis=-1, keepdims=True) + NORM_EPS) * g_ref[...]
    o_ref[...] = jnp.dot(h.astype(BF16), w_ref[...], preferred_element_type=F32)


def _norm_matmul(x, g, w, tm):
    n, d = x.shape
    dout = w.shape[1]
    return pl.pallas_call(
        _norm_matmul_kernel,
        grid=(n // tm,),
        in_specs=[pl.BlockSpec((tm, d), lambda i: (i, 0)),
                  pl.BlockSpec((1, d), lambda i: (0, 0)),
                  pl.BlockSpec((d, dout), lambda i: (0, 0))],
        out_specs=pl.BlockSpec((tm, dout), lambda i: (i, 0)),
        out_shape=jax.ShapeDtypeStruct((n, dout), F32),
        compiler_params=_cparams("parallel"),
        name="norm_in_proj",
    )(x, g, w)


def _block_ones(n):
    r = lax.broadcasted_iota(jnp.int32, (n, n), 0) // HEAD_DIM
    c = lax.broadcasted_iota(jnp.int32, (n, n), 1) // HEAD_DIM
    return (r == c).astype(BF16)


def _head_rms_scale(z, ones):
    hi, lo = _split2(z * z)
    ss = jnp.dot(hi, ones, preferred_element_type=F32) + jnp.dot(lo, ones, preferred_element_type=F32)
    return lax.rsqrt(ss * (1.0 / HEAD_DIM) + NORM_EPS)


def _attn_kernel(sink_ref, q_ref, kv_ref, prev_ref, bias_ref, qg_ref, kg_ref, ya_ref, kn_ref,
                 *, rq, n_part, prev_is_raw):
    i = pl.program_id(1)
    kp = WINDOW + rq
    gw = A_GROUP * HEAD_DIM
    q = q_ref[0]
    kv = kv_ref[0]
    pv = prev_ref[0]
    ones_q = _block_ones(D_A)
    ones_k = _block_ones(D_KV)
    qn = (q * _head_rms_scale(q, ones_q) * qg_ref[...]).astype(BF16)
    k_cur = kv[:, 0:D_KV]
    k_cur = k_cur * _head_rms_scale(k_cur, ones_k) * kg_ref[...]
    kn_ref[0] = k_cur
    k_prev = pv[:, 0:D_KV]
    if prev_is_raw:
        k_prev = k_prev * _head_rms_scale(k_prev, ones_k) * kg_ref[...]
    k_all = jnp.concatenate([k_prev, k_cur], axis=0).astype(BF16)
    v_all = jnp.concatenate([pv[:, D_KV:], kv[:, D_KV:]], axis=0).astype(BF16)
    nq = A_GROUP * rq
    lane_in = lax.broadcasted_iota(jnp.int32, (D_KV, gw), 0)
    lane_out = lax.broadcasted_iota(jnp.int32, (D_KV, gw), 1)
    lane_out_t = lax.broadcasted_iota(jnp.int32, (HEAD_DIM, D_KV), 0)
    lane_in_t = lax.broadcasted_iota(jnp.int32, (HEAD_DIM, D_KV), 1)
    slot_lane = lax.broadcasted_iota(jnp.int32, (1, gw), 1) // HEAD_DIM
    slot_mask_bf = [(slot_lane == hh).astype(BF16) for hh in range(A_GROUP)]
    key_row = lax.broadcasted_iota(jnp.int32, (kp, nq), 0)
    col_head = lax.broadcasted_iota(jnp.int32, (1, nq), 1) // rq
    for g in range(A_KV_HEADS):
        select = (lane_in // HEAD_DIM == g) & (lane_in % HEAD_DIM == lane_out % HEAD_DIM)
        k_wide = jnp.dot(k_all, select.astype(BF16), preferred_element_type=F32).astype(BF16)
        v_t = lax.dot_general((lane_in_t == g * HEAD_DIM + lane_out_t).astype(BF16), v_all,
                              (((1,), (1,)), ((), ())), preferred_element_type=F32).astype(BF16)
        q_g = qn[:, g * gw:(g + 1) * gw]
        sink_row = jnp.zeros((1, nq), F32)
        for hh in range(A_GROUP):
            sink_row = jnp.where(col_head == hh, sink_ref[g * A_GROUP + hh], sink_row)
        bias_t = bias_ref[g]
        for p in range(n_part):
            rows = slice(p * rq, (p + 1) * rq)
            qs = jnp.concatenate([q_g[rows] * m for m in slot_mask_bf], axis=0)
            kb = k_wide[p * rq:p * rq + kp]
            s = lax.dot_general(kb, qs, (((1,), (1,)), ((), ())), preferred_element_type=F32) + bias_t
            if prev_is_raw and p * rq < WINDOW:
                n_pad = WINDOW - (i * n_part + p) * rq
                s = s + jnp.where(key_row < n_pad, NEG_INF, 0.0)
            m = jnp.maximum(jnp.max(s, axis=0, keepdims=True), sink_row)
            e = jnp.exp2(s - m)
            den = jnp.sum(e, axis=0, keepdims=True) + jnp.exp2(sink_row - m)
            prob = (e * (1.0 / den)).astype(BF16)
            vb_t = v_t[:, p * rq:p * rq + kp]
            out_t = jnp.concatenate(
                [jnp.dot(vb_t, prob[:, hh * rq:(hh + 1) * rq], preferred_element_type=F32)
                 for hh in range(A_GROUP)], axis=0)
            ya_ref[0, rows, g * gw:(g + 1) * gw] = out_t.T.astype(BF16)


def _attn_bias(rq, cq):
    kp = WINDOW + rq
    i = jnp.arange(rq)[:, None]
    j = jnp.arange(kp)[None, :]
    jb = j - cq * (i // cq)
    valid = (jb >= 0) & (jb < WINDOW + cq)
    dist = jnp.abs(WINDOW + (i % cq) - jb).astype(F32)
    slopes = jnp.exp2(-8.0 * jnp.arange(1, A_HEADS + 1, dtype=F32) / A_HEADS)
    bias = jnp.where(valid[None], -slopes[:, None, None] * dist[None], NEG_INF)
    return jnp.swapaxes(bias.reshape(A_KV_HEADS, A_GROUP * rq, kp), 1, 2) * LOG2_E


def _attention(proj, cache_kv, q_gain, k_gain, sink, cq, rq, n_part):
    b, t, _ = proj.shape
    tq = rq * n_part
    prompt = cache_kv is None
    kv_col = D_A // (2 * D_KV)
    if prompt:
        prev_arr = proj
        prev_spec = pl.BlockSpec((1, WINDOW, 2 * D_KV),
                                 lambda bi, i: (bi, jnp.maximum(i * (tq // WINDOW) - 1, 0), kv_col))
    else:
        prev_arr = cache_kv
        prev_spec = pl.BlockSpec((1, WINDOW, 2 * D_KV), lambda bi, i: (bi, 0, 0))
    kp = WINDOW + rq
    q_gain_t = jnp.tile(q_gain * (HEAD_DIM ** -0.5 * LOG2_E), (1, A_HEADS))
    k_gain_t = jnp.tile(k_gain, (1, A_KV_HEADS))
    kern = functools.partial(_attn_kernel, rq=rq, n_part=n_part, prev_is_raw=prompt)
    return pl.pallas_call(
        kern,
        grid=(b, t // tq),
        in_specs=[pl.BlockSpec(memory_space=pltpu.SMEM),
                  pl.BlockSpec((1, tq, D_A), lambda bi, i: (bi, i, 0)),
                  pl.BlockSpec((1, tq, 2 * D_KV), lambda bi, i: (bi, i, kv_col)),
                  prev_spec,
                  pl.BlockSpec((A_KV_HEADS, kp, A_GROUP * rq), lambda bi, i: (0, 0, 0)),
                  pl.BlockSpec((1, D_A), lambda bi, i: (0, 0)),
                  pl.BlockSpec((1, D_KV), lambda bi, i: (0, 0))],
        out_specs=[pl.BlockSpec((1, tq, D_A), lambda bi, i: (bi, i, 0)),
                   pl.BlockSpec((1, tq, D_KV), lambda bi, i: (bi, i, 0))],
        out_shape=[jax.ShapeDtypeStruct((b, t, D_A), BF16),
                   jax.ShapeDtypeStruct((b, t, D_KV), F32)],
        compiler_params=_cparams("parallel", "arbitrary"),
        name="swa_attention",
    )(sink * LOG2_E, proj, proj, prev_arr, _attn_bias(rq, cq), q_gain_t, k_gain_t)


def _pool_kernel(u_ref, halo_ref, prefix_ref, w_ref, scale_ref, yb_ref, *, n_prefix):
    i = pl.program_id(1)
    u = u_ref[0]
    tb = u.shape[0]
    halo = jnp.where(i == 0, prefix_ref[0], halo_ref[0])
    ext = jnp.concatenate([halo, u], axis=0)
    col = lax.broadcasted_iota(jnp.int32, (1, D_POOL), 1)
    pos = i * tb + lax.broadcasted_iota(jnp.int32, (tb, 1), 0)
    total = None
    count = None
    acc = ext
    span = 1
    for gi, w in enumerate(POOL_WINDOWS):
        while span < w:
            acc = acc + pltpu.roll(acc, span, axis=0)
            span *= 2
        in_group = (col >= gi * POOL_GW) & (col < (gi + 1) * POOL_GW)
        tail = acc[POOL_HALO:]
        total = jnp.where(in_group, tail, 0.0 if total is None else total)
        cnt = jnp.minimum(pos + (1 + n_prefix), w).astype(F32)
        count = jnp.where(in_group, cnt, 1.0 if count is None else count)
    d = total / count - u
    yb_ref[0] = (_mm(d, w_ref[...]) * scale_ref[...]).astype(BF16)


def _pool_mix(proj, prefix, w_blockdiag, scale, n_prefix, tb):
    b, t, _ = proj.shape
    col = (D_A + 2 * D_KV) // D_POOL
    kern = functools.partial(_pool_kernel, n_prefix=n_prefix)
    return pl.pallas_call(
        kern,
        grid=(b, t // tb),
        in_specs=[pl.BlockSpec((1, tb, D_POOL), lambda bi, i: (bi, i, col)),
                  pl.BlockSpec((1, POOL_HALO, D_POOL),
                               lambda bi, i: (bi, jnp.maximum(i * (tb // POOL_HALO) - 1, 0), col)),
                  pl.BlockSpec((1, POOL_HALO, D_POOL), lambda bi, i: (bi, 0, 0)),
                  pl.BlockSpec((D_POOL, D_POOL), lambda bi, i: (0, 0)),
                  pl.BlockSpec((1, D_POOL), lambda bi, i: (0, 0))],
        out_specs=pl.BlockSpec((1, tb, D_POOL), lambda bi, i: (bi, i, 0)),
        out_shape=jax.ShapeDtypeStruct((b, t, D_POOL), BF16),
        compiler_params=_cparams("parallel", "arbitrary"),
        name="pool_mix",
    )(proj, proj, prefix, w_blockdiag, scale)


def _rwkv_kernel(p_ref, prev_ref, st0_ref, mu_ref, w0_ref, a0_ref, w2_ref, a2_ref, g2_ref,
                 kk_ref, ka_ref, rk_ref, lng_ref, lnb_ref, yc_ref, st_ref, carry_ref, y_ref,
                 *, chunk):
    j = pl.program_id(1)

    @pl.when(j == 0)
    def _():
        carry_ref[...] = prev_ref[0]
        st_ref[0] = st0_ref[0]

    p = p_ref[0]
    tb = p.shape[0]
    n_chunk = tb // chunk
    hl = R_HEADS * chunk

    row = lax.broadcasted_iota(jnp.int32, (tb, 1), 0)
    p_prev = jnp.where(row == 0, carry_ref[...], pltpu.roll(p, 1, axis=0))
    carry_ref[...] = p[tb - 1:tb]
    xs = p + mu_ref[...] * (p_prev - p)
    r = xs[:, 0:D_R]
    k = xs[:, D_R:2 * D_R]
    v = xs[:, 2 * D_R:3 * D_R]
    wa = xs[:, 3 * D_R:3 * D_R + R_WA]
    gd = xs[:, 3 * D_R + R_WA:]

    z = -(w0_ref[...] + _mm(jnp.tanh(wa), w2_ref[...]))
    softplus = jnp.maximum(z, 0.0) + jnp.log(1.0 + jnp.exp(-jnp.abs(z)))
    lw = -jnp.exp(-softplus - 0.5)
    a = _sigmoid(a0_ref[...] + _mm(wa, a2_ref[...]))
    g = _mm(_sigmoid(gd), g2_ref[...])

    lane_r = lax.broadcasted_iota(jnp.int32, (D_R, D_R), 0) // HEAD_DIM
    lane_c = lax.broadcasted_iota(jnp.int32, (D_R, D_R), 1) // HEAD_DIM
    head_ones = (lane_r == lane_c).astype(F32)
    seg_sum = lambda t: _mm_exact_rhs(t, head_ones)

    kk = k * kk_ref[...]
    kk = kk / jnp.maximum(jnp.sqrt(seg_sum(kk * kk)), 1e-12)
    k2 = k * (1.0 + (a - 1.0) * ka_ref[...])
    bb = kk * a

    ti = lax.broadcasted_iota(jnp.int32, (tb, tb), 0)
    tj = lax.broadcasted_iota(jnp.int32, (tb, tb), 1)
    cum = _mm_exact_lhs(((ti // chunk == tj // chunk) & (tj <= ti)).astype(F32), lw)

    wi = lax.broadcasted_iota(jnp.int32, (chunk, hl), 0)
    wj = lax.broadcasted_iota(jnp.int32, (chunk, hl), 1) % chunk
    strict = wj < wi
    incl = wj <= wi
    eye_w = (wj == wi).astype(F32)
    diag_k = (lax.broadcasted_iota(jnp.int32, (HEAD_DIM, D_R), 0)
              == lax.broadcasted_iota(jnp.int32, (HEAD_DIM, D_R), 1) % HEAD_DIM)
    lane_k = lax.broadcasted_iota(jnp.int32, (1, D_R), 1) // HEAD_DIM
    lane_t = lax.broadcasted_iota(jnp.int32, (1, hl), 1) // chunk
    mask_k = [(lane_k == h).astype(F32) for h in range(R_HEADS)]
    mask_k_bf = [m.astype(BF16) for m in mask_k]
    mask_t_bf = [(lane_t == h).astype(BF16) for h in range(R_HEADS)]

    def blockdiag(t, masks):
        t16 = t.astype(BF16)
        return jnp.concatenate([t16 * m for m in masks], axis=0)

    def wide_transpose(t):
        tt = jnp.concatenate([t * m for m in mask_k], axis=0).T
        out = tt[0:HEAD_DIM]
        for h in range(1, R_HEADS):
            out = out + tt[h * HEAD_DIM:(h + 1) * HEAD_DIM]
        return out.astype(BF16)

    chunks = []
    for c in range(n_chunk):
        sl = slice(c * chunk, (c + 1) * chunk)
        cum_c = cum[sl]
        cum_last = cum_c[chunk - 1:chunk]
        g_in = jnp.exp(cum_c)
        g_prev = jnp.exp(cum_c - lw[sl])
        g_inv = jnp.exp(-cum_c)
        g_out = jnp.exp(cum_last - cum_c)
        a_n = (kk[sl] * g_prev).astype(BF16)
        r_n = r[sl] * g_in
        ch = dict(sl=sl, r_n=r_n, g_last=jnp.exp(cum_last),
                  a_s=blockdiag(a_n, mask_k_bf),
                  v_s=blockdiag(v[sl], mask_k_bf),
                  bo_w=wide_transpose(bb[sl] * g_out),
                  ko_w=wide_transpose(k2[sl] * g_out))
        ar = jnp.concatenate([a_n, r_n.astype(BF16)], axis=0)
        m_b = _mm_nt(ar, blockdiag(bb[sl] * g_inv, mask_k_bf))
        m_k = _mm_nt(ar, blockdiag(k2[sl] * g_inv, mask_k_bf))
        m_ab = jnp.where(strict, m_b[:chunk], 0.0)
        ch.update(m_rb=jnp.where(incl, m_b[chunk:], 0.0).astype(BF16),
                  m_ak=jnp.where(strict, m_k[:chunk], 0.0).astype(BF16),
                  m_rk=jnp.where(incl, m_k[chunk:], 0.0).astype(BF16),
                  t_inv=eye_w - m_ab, pw=m_ab.astype(BF16))
        chunks.append(ch)

    for ch in chunks:
        ch["pw"] = _mm(ch["pw"], blockdiag(ch["pw"], mask_t_bf)).astype(BF16)
    n = 2
    while n < chunk:
        for ch in chunks:
            rhs = blockdiag(ch["pw"], mask_t_bf)
            if 2 * n < chunk:
                both = _mm(jnp.concatenate([ch["pw"], ch["t_inv"].astype(BF16)], axis=0), rhs)
                ch["pw"] = both[:chunk].astype(BF16)
                ch["t_inv"] = ch["t_inv"] + both[chunk:]
            else:
                ch["t_inv"] = ch["t_inv"] + _mm(ch["t_inv"], rhs)
        n *= 2

    for ch in chunks:
        t_inv = ch["t_inv"].astype(BF16)
        ch["a_bar"] = blockdiag(_mm(t_inv, ch["a_s"]), mask_k_bf)
        on_v = _mm(jnp.concatenate([ch["m_ak"], ch["m_rk"], ch["ko_w"]], axis=0), ch["v_s"])
        ch["mrk_v"], ch["ko_v"] = on_v[chunk:2 * chunk], on_v[2 * chunk:]
        ch["u0"] = blockdiag(-_mm(t_inv, blockdiag(on_v[:chunk], mask_k_bf)), mask_k_bf)
    for ch in chunks:
        lhs = jnp.concatenate([ch["m_rb"], ch["bo_w"]], axis=0)
        on_a = _mm(lhs, ch["a_bar"])
        on_u = _mm(lhs, ch["u0"])
        ch["r_bar"] = (ch["r_n"] - on_a[:chunk]).astype(BF16)
        ch["y0"] = ch["mrk_v"] + on_u[:chunk]
        ch["g_w"] = (jnp.where(diag_k, ch["g_last"], 0.0) - on_a[chunk:]).astype(BF16)
        ch["h_w"] = on_u[chunk:] + ch["ko_v"]

    st_w = st_ref[0]
    for ch in chunks:
        on_st = _mm(jnp.concatenate([ch["r_bar"], ch["g_w"]], axis=0), blockdiag(st_w, mask_k_bf))
        y_ref[ch["sl"], :] = ch["y0"] + on_st[:chunk]
        st_w = on_st[chunk:] + ch["h_w"]
    st_ref[0] = st_w

    y = y_ref[...]
    mean = seg_sum(y) * (1.0 / HEAD_DIM)
    d = y - mean
    var = seg_sum(d * d) * (1.0 / HEAD_DIM)
    yn = d * lax.rsqrt(var + GN_EPS) * lng_ref[...] + lnb_ref[...]
    bonus = seg_sum(r * k2 * rk_ref[...]) * v
    yc_ref[0] = ((yn + bonus) * g).astype(BF16)


def _rwkv_mix(proj, prev, st0, lp, chunk, tb):
    b, t, _ = proj.shape
    col = (D_A + 2 * D_KV + D_POOL) // D_R_IN
    row = lambda n: pl.BlockSpec((1, n), lambda bi, i: (0, 0))
    full = lambda s: pl.BlockSpec(s, lambda bi, i: (0,) * len(s))
    kern = functools.partial(_rwkv_kernel, chunk=chunk)
    return pl.pallas_call(
        kern,
        grid=(b, t // tb),
        in_specs=[pl.BlockSpec((1, tb, D_R_IN), lambda bi, i: (bi, i, col)),
                  pl.BlockSpec((1, 1, D_R_IN), lambda bi, i: (bi, 0, 0)),
                  pl.BlockSpec((1, HEAD_DIM, D_R), lambda bi, i: (bi, 0, 0)),
                  row(D_R_IN), row(D_R), row(D_R),
                  full((R_WA, D_R)), full((R_WA, D_R)), full((R_G, D_R)),
                  row(D_R), row(D_R), row(D_R), row(D_R), row(D_R)],
        out_specs=[pl.BlockSpec((1, tb, D_R), lambda bi, i: (bi, i, 0)),
                   pl.BlockSpec((1, HEAD_DIM, D_R), lambda bi, i: (bi, 0, 0))],
        out_shape=[jax.ShapeDtypeStruct((b, t, D_R), BF16),
                   jax.ShapeDtypeStruct((b, HEAD_DIM, D_R), F32)],
        scratch_shapes=[pltpu.VMEM((1, D_R_IN), F32), pltpu.VMEM((tb, D_R), F32)],
        compiler_params=_cparams("parallel", "arbitrary"),
        name="rwkv7_mix",
    )(proj, prev, st0, lp["mu"], lp["w0"], lp["a0"], lp["w2"], lp["a2"], lp["g2"],
      lp["k_k"], lp["k_a"], lp["r_k"], lp["lnx_g"], lp["lnx_b"])


def _state_to_wide(s):
    b = s.shape[0]
    return jnp.transpose(s, (0, 3, 1, 2)).reshape(b, HEAD_DIM, D_R)


def _wide_to_state(st):
    b = st.shape[0]
    return jnp.transpose(st.reshape(b, HEAD_DIM, R_HEADS, HEAD_DIM), (0, 2, 3, 1))


def _mix_out_kernel(x_ref, ya_ref, yb_ref, yc_ref, w_ref, g_ref, *rest, route):
    if route:
        rw_ref, xm_ref, h_ref, gate_ref = rest
    else:
        xm_ref, h_ref = rest
    dot = lambda u, lo, hi: jnp.dot(u[...], w_ref[lo:hi, :], preferred_element_type=F32)
    xm = (x_ref[...] + dot(ya_ref, 0, D_A) + dot(yb_ref, D_A, D_A + D_POOL)
          + dot(yc_ref, D_A + D_POOL, D_A + D_POOL + D_R))
    xm_ref[...] = xm
    h = xm * lax.rsqrt(jnp.mean(xm * xm, axis=-1, keepdims=True) + NORM_EPS) * g_ref[...]
    h_hi = h.astype(BF16)
    h_ref[...] = h_hi
    if route:
        rw_t = rw_ref[...]
        n_e = rw_t.shape[0]
        rw_hi = rw_t.astype(BF16).astype(F32)
        rw2 = jnp.concatenate([rw_hi, rw_t - rw_hi], axis=0).astype(BF16)
        h_lo = (h - h_hi.astype(F32)).astype(BF16)
        nt = lambda a, b: lax.dot_general(a, b, (((1,), (1,)), ((), ())), preferred_element_type=F32)
        on_hi = nt(rw2, h_hi)
        logits = on_hi[:n_e] + (on_hi[n_e:] + nt(rw2, h_lo)[:n_e])
        row = lax.broadcasted_iota(jnp.int32, logits.shape, 0).astype(F32)
        m1 = jnp.max(logits, axis=0, keepdims=True)
        i1 = jnp.min(jnp.where(logits == m1, row, float(n_e)), axis=0, keepdims=True)
        rest_l = jnp.where(row == i1, -jnp.inf, logits)
        m2 = jnp.max(rest_l, axis=0, keepdims=True)
        i2 = jnp.min(jnp.where(rest_l == m2, row, float(n_e)), axis=0, keepdims=True)
        e2 = jnp.exp(m2 - m1)
        gates_t = (jnp.where(row == i1, 1.0 / (1.0 + e2), 0.0)
                   + jnp.where(row == i2, e2 / (1.0 + e2), 0.0))
        gate_ref[...] = gates_t.T


def _mix_out(x, ya, yb, yc, w_out, g, router_w, tm):
    n, d = x.shape
    route = router_w is not None
    tile = lambda w: pl.BlockSpec((tm, w), lambda i: (i, 0))
    in_specs = [tile(d), tile(D_A), tile(D_POOL), tile(D_R),
                pl.BlockSpec(w_out.shape, lambda i: (0, 0)),
                pl.BlockSpec((1, d), lambda i: (0, 0))]
    out_specs = [tile(d), tile(d)]
    out_shape = [jax.ShapeDtypeStruct((n, d), F32), jax.ShapeDtypeStruct((n, d), BF16)]
    args = [x, ya, yb, yc, w_out, g]
    if route:
        n_e = router_w.shape[1]
        in_specs.append(pl.BlockSpec((n_e, d), lambda i: (0, 0)))
        out_specs.append(tile(n_e))
        out_shape.append(jax.ShapeDtypeStruct((n, n_e), F32))
        args.append(router_w.T)
    return pl.pallas_call(
        functools.partial(_mix_out_kernel, route=route),
        grid=(n // tm,),
        in_specs=in_specs, out_specs=out_specs, out_shape=out_shape,
        compiler_params=_cparams("parallel"),
        name="mix_out_proj",
    )(*args)


def _swiglu_act(h, wg, wu):
    return (_silu(jnp.dot(h, wg, preferred_element_type=F32))
            * jnp.dot(h, wu, preferred_element_type=F32)).astype(BF16)


def _down_proj(act_ref, wd, tf):
    out = None
    for jj in range(act_ref.shape[0]):
        part = jnp.dot(act_ref[jj], wd[jj * tf:(jj + 1) * tf, :], preferred_element_type=F32)
        out = part if out is None else out + part
    return out


def _ffn_kernel(h_ref, xm_ref, wg_ref, wu_ref, wd_ref, o_ref, act_ref):
    j = pl.program_id(1)
    tf = wg_ref.shape[1]
    act_ref[j] = _swiglu_act(h_ref[...], wg_ref[...], wu_ref[...])

    @pl.when(j == pl.num_programs(1) - 1)
    def _():
        o_ref[...] = xm_ref[...] + _down_proj(act_ref, wd_ref, tf)


def _ffn(h, xm, wg, wu, wd, tm, tf):
    n, d = xm.shape
    f = wg.shape[1]
    return pl.pallas_call(
        _ffn_kernel,
        grid=(n // tm, f // tf),
        in_specs=[pl.BlockSpec((tm, d), lambda i, j: (i, 0)),
                  pl.BlockSpec((tm, d), lambda i, j: (i, 0)),
                  pl.BlockSpec((d, tf), lambda i, j: (0, j)),
                  pl.BlockSpec((d, tf), lambda i, j: (0, j)),
                  pl.BlockSpec((f, d), lambda i, j: (0, 0))],
        out_specs=pl.BlockSpec((tm, d), lambda i, j: (i, 0)),
        out_shape=jax.ShapeDtypeStruct((n, d), F32),
        scratch_shapes=[pltpu.VMEM((f // tf, tm, tf), BF16)],
        compiler_params=_cparams("parallel", "arbitrary"),
        name="swiglu_ffn",
    )(h, xm, wg, wu, wd)


def _moe_kernel(h_ref, xm_ref, gate_ref, wg_ref, wu_ref, wd_ref, o_ref, acc_ref):
    e = pl.program_id(1)
    j = pl.program_id(2)

    @pl.when((e == 0) & (j == 0))
    def _():
        acc_ref[...] = xm_ref[...]

    gates = gate_ref[...]
    lane = lax.broadcasted_iota(jnp.int32, gates.shape, 1)
    gate = jnp.sum(jnp.where(lane == e, gates, 0.0), axis=-1, keepdims=True)
    h = h_ref[...]
    act = (_silu(jnp.dot(h, wg_ref[0], preferred_element_type=F32))
           * jnp.dot(h, wu_ref[0], preferred_element_type=F32))
    acc_ref[...] += gate * jnp.dot(act.astype(BF16), wd_ref[0], preferred_element_type=F32)

    @pl.when((e == pl.num_programs(1) - 1) & (j == pl.num_programs(2) - 1))
    def _():
        o_ref[...] = acc_ref[...]


def _moe(h, xm, gates, wg, wu, wd, tm, tf):
    n, d = xm.shape
    n_e, _, f = wg.shape
    return pl.pallas_call(
        _moe_kernel,
        grid=(n // tm, n_e, f // tf),
        in_specs=[pl.BlockSpec((tm, d), lambda i, e, j: (i, 0)),
                  pl.BlockSpec((tm, d), lambda i, e, j: (i, 0)),
                  pl.BlockSpec((tm, n_e), lambda i, e, j: (i, 0)),
                  pl.BlockSpec((1, d, tf), lambda i, e, j: (e, 0, j)),
                  pl.BlockSpec((1, d, tf), lambda i, e, j: (e, 0, j)),
                  pl.BlockSpec((1, tf, d), lambda i, e, j: (e, j, 0))],
        out_specs=pl.BlockSpec((tm, d), lambda i, e, j: (i, 0)),
        out_shape=jax.ShapeDtypeStruct((n, d), F32),
        scratch_shapes=[pltpu.VMEM((tm, d), F32)],
        compiler_params=_cparams("parallel", "arbitrary", "arbitrary"),
        name="moe_ffn",
    )(h, xm, gates, wg, wu, wd)


MOE_ROW_ALIGN = 32
MOE_GATE_LANES = 128
MOE_TILE = 1024
MOE_GROUP_ROWS = 1024
_UNSELECTED = 1e9


def _tile_routing(gates_sel, rank, axis):
    n_e = rank.shape[axis]
    cnt = jnp.max(rank, axis=1 - axis, keepdims=True)
    padded = jnp.floor((cnt + (MOE_ROW_ALIGN - 1)) * (1.0 / MOE_ROW_ALIGN)) * MOE_ROW_ALIGN
    offs, run = [], jnp.zeros((1, 1), F32)
    for e in range(n_e):
        offs.append(run)
        run = run + (padded[e:e + 1] if axis == 0 else padded[:, e:e + 1])
    off = jnp.concatenate(offs, axis=axis)
    dest = jnp.where(gates_sel, off + rank - 1.0, -1.0)
    d_hi = jnp.max(dest, axis=axis, keepdims=True)
    d_lo = jnp.min(jnp.where(gates_sel, dest, _UNSELECTED), axis=axis, keepdims=True)
    d_lo = jnp.where(d_lo == d_hi, -2.0, d_lo)
    return cnt, dest, d_hi, d_lo


def _moe_dispatch_kernel(h_ref, gate_ref, hs_ref, cnt_ref, *, chunk_rows):
    h = h_ref[...]
    gates = gate_ref[...]
    tm, n_e = gates.shape
    d = h.shape[1]
    mt = hs_ref.shape[0]
    eye = (lax.broadcasted_iota(jnp.int32, (n_e, n_e), 0)
           == lax.broadcasted_iota(jnp.int32, (n_e, n_e), 1)).astype(BF16)
    to_rows = lambda u: lax.dot_general(eye, u, (((1,), (1,)), ((), ())), preferred_element_type=F32)
    g_hi3, g_mid3, g_lo3 = _split3(gates)
    gate_row = to_rows(g_hi3) + to_rows(g_mid3) + to_rows(g_lo3)
    sel_row = gate_row > 0.0
    ti = lax.broadcasted_iota(jnp.int32, (tm, tm), 0)
    tj = lax.broadcasted_iota(jnp.int32, (tm, tm), 1)
    rank = jnp.dot(sel_row.astype(BF16), (ti <= tj).astype(BF16), preferred_element_type=F32)
    cnt, dest, d_hi, d_lo = _tile_routing(sel_row, rank, 0)
    cnt_ref[0] = jnp.broadcast_to(cnt, cnt_ref.shape[1:])
    g_hi = jnp.sum(jnp.where(dest == d_hi, gate_row, 0.0), axis=0, keepdims=True)
    g_lo = jnp.sum(jnp.where(dest == d_lo, gate_row, 0.0), axis=0, keepdims=True)
    lane = lax.broadcasted_iota(jnp.int32, (chunk_rows, MOE_GATE_LANES), 1)
    for c0 in range(0, mt, chunk_rows):
        rho = (c0 + lax.broadcasted_iota(jnp.int32, (chunk_rows, 1), 0)).astype(F32)
        is_hi = d_hi == rho
        is_lo = d_lo == rho
        onehot = jnp.where(is_hi, 1.0, jnp.where(is_lo, 1.0, 0.0)).astype(BF16)
        rows = jnp.dot(onehot, h, preferred_element_type=F32)
        g = jnp.sum(jnp.where(is_hi, g_hi, jnp.where(is_lo, g_lo, 0.0)), axis=1, keepdims=True)
        p0, p1, p2 = [t.astype(F32) for t in _split3(g)]
        gcols = jnp.where(lane == 0, p0, jnp.where(lane == 1, p1, jnp.where(lane == 2, p2, 0.0)))
        hs_ref[c0:c0 + chunk_rows, 0:d] = rows.astype(BF16)
        hs_ref[c0:c0 + chunk_rows, d:d + MOE_GATE_LANES] = gcols.astype(BF16)


def _moe_dispatch(h, gates, tm, mt):
    n, d = h.shape
    n_e = gates.shape[1]
    n_tiles = n // tm
    return pl.pallas_call(
        functools.partial(_moe_dispatch_kernel, chunk_rows=256),
        grid=(n_tiles,),
        in_specs=[pl.BlockSpec((tm, d), lambda i: (i, 0)),
                  pl.BlockSpec((tm, n_e), lambda i: (i, 0))],
        out_specs=[pl.BlockSpec((mt, d + MOE_GATE_LANES), lambda i: (i, 0)),
                   pl.BlockSpec((1, n_e, 128), lambda i: (i, 0, 0))],
        out_shape=[jax.ShapeDtypeStruct((n_tiles * mt, d + MOE_GATE_LANES), BF16),
                   jax.ShapeDtypeStruct((n_tiles, n_e, 128), F32)],
        compiler_params=_cparams("parallel"),
        name="moe_dispatch",
    )(h, gates)


def _moe_tables(cnt, mt, group_rows):
    i32 = jnp.int32
    n_tiles, n_e = cnt.shape
    bpt = mt // MOE_ROW_ALIGN
    g = group_rows // MOE_ROW_ALIGN
    nblk = jnp.ceil(cnt / MOE_ROW_ALIGN).astype(i32)
    cum = jnp.cumsum(nblk, axis=1)
    off_blk = cum - nblk
    src_base = jnp.arange(n_tiles, dtype=i32)[:, None] * bpt + off_blk
    tot = jnp.sum(nblk, axis=0)
    totp = ((tot + g - 1) // g) * g
    es = jnp.cumsum(totp) - totp
    nblk_t = nblk.T
    seg_start = es[:, None] + jnp.cumsum(nblk_t, axis=1) - nblk_t
    n_src = n_tiles * bpt
    n_dst = -(-(n_src + n_e * (g - 1)) // g) * g
    dblk = jnp.arange(n_dst, dtype=i32)[:, None]
    s0, sl, ss = seg_start.reshape(1, -1), nblk_t.reshape(1, -1), src_base.T.reshape(1, -1)
    hit = (dblk >= s0) & (dblk < s0 + sl)
    src_of_dst = jnp.sum(jnp.where(hit, ss + dblk - s0, 0), axis=1)
    n_rb = n_dst // g
    eid = jnp.sum(jnp.arange(n_rb, dtype=i32)[:, None] * g >= es[None, :], axis=1).astype(i32) - 1
    n_valid = ((es[-1] + totp[-1]) // g).reshape(1)
    o_of = jnp.arange(bpt, dtype=i32)[None, :, None]
    off3, len3 = off_blk[:, None, :], nblk[:, None, :]
    hit3 = (o_of >= off3) & (o_of < off3 + len3)
    dst_of_src = jnp.sum(jnp.where(hit3, seg_start.T[:, None, :] + o_of - off3, 0), axis=2).reshape(-1)
    return src_of_dst.astype(i32), dst_of_src.astype(i32), eid, n_valid.astype(i32), n_dst


def _block_gather_kernel(idx_ref, *refs):
    del idx_ref
    out_ref = refs[-1]
    rows = refs[0].shape[0]
    for q, src_ref in enumerate(refs[:-1]):
        out_ref[q * rows:(q + 1) * rows, :] = src_ref[...]


def _block_gather(idx, src, n_blocks, per_step, name):
    rows = MOE_ROW_ALIGN
    width = src.shape[1]
    pick = lambda q: pl.BlockSpec((rows, width), lambda i, idx: (idx[i * per_step + q], 0))
    return pl.pallas_call(
        _block_gather_kernel,
        grid_spec=pltpu.PrefetchScalarGridSpec(
            num_scalar_prefetch=1, grid=(n_blocks // per_step,),
            in_specs=[pick(q) for q in range(per_step)],
            out_specs=pl.BlockSpec((per_step * rows, width), lambda i, idx: (i, 0))),
        out_shape=jax.ShapeDtypeStruct((n_blocks * rows, width), src.dtype),
        compiler_params=_cparams("arbitrary"),
        name=name,
    )(idx, *([src] * per_step))


def _moe_group_kernel(eid_ref, nv_ref, hs_ref, wg_ref, wu_ref, wd_ref, ys_ref, act_ref):
    del eid_ref
    i = pl.program_id(0)
    j = pl.program_id(1)
    d = ys_ref.shape[1]
    tf = wg_ref.shape[2]

    @pl.when(i < nv_ref[0])
    def _():
        act_ref[j] = _swiglu_act(hs_ref[:, 0:d], wg_ref[0], wu_ref[0])

        @pl.when(j == pl.num_programs(1) - 1)
        def _():
            gp = hs_ref[:, d:d + MOE_GATE_LANES].astype(F32)
            gate = gp[:, 0:1] + gp[:, 1:2] + gp[:, 2:3]
            ys_ref[...] = (_down_proj(act_ref, wd_ref[0], tf) * gate).astype(BF16)

    @pl.when((i >= nv_ref[0]) & (j == 0))
    def _():
        ys_ref[...] = jnp.zeros(ys_ref.shape, BF16)


def _moe_group_ffn(hs, eid, n_valid, wg, wu, wd, rb, tf):
    rows, dw = hs.shape
    d = dw - MOE_GATE_LANES
    f = wg.shape[2]
    nj = f // tf
    live_j = lambda i, j, nv: jnp.where(i < nv[0], j, nj - 1)
    return pl.pallas_call(
        _moe_group_kernel,
        grid_spec=pltpu.PrefetchScalarGridSpec(
            num_scalar_prefetch=2, grid=(rows // rb, nj),
            in_specs=[pl.BlockSpec((rb, dw), lambda i, j, eid, nv: (jnp.minimum(i, nv[0] - 1), 0)),
                      pl.BlockSpec((1, d, tf), lambda i, j, eid, nv: (eid[i], 0, live_j(i, j, nv))),
                      pl.BlockSpec((1, d, tf), lambda i, j, eid, nv: (eid[i], 0, live_j(i, j, nv))),
                      pl.BlockSpec((1, f, d), lambda i, j, eid, nv: (eid[i], 0, 0))],
            out_specs=pl.BlockSpec((rb, d), lambda i, j, eid, nv: (i, 0)),
            scratch_shapes=[pltpu.VMEM((nj, rb, tf), BF16)]),
        out_shape=jax.ShapeDtypeStruct((rows, d), BF16),
        compiler_params=_cparams("arbitrary", "arbitrary"),
        name="moe_group_ffn",
    )(eid, n_valid, hs, wg, wu, wd)


def _moe_combine_kernel(idx_ref, xm_ref, gate_ref, *refs):
    del idx_ref
    o_ref, ys_ref = refs[-2:]
    rows = refs[0].shape[0]
    for q, src_ref in enumerate(refs[:-2]):
        ys_ref[q * rows:(q + 1) * rows, :] = src_ref[...]
    gates = gate_ref[...]
    tm, n_e = gates.shape
    mt = ys_ref.shape[0]
    sel = gates > 0.0
    ti = lax.broadcasted_iota(jnp.int32, (tm, tm), 0)
    tj = lax.broadcasted_iota(jnp.int32, (tm, tm), 1)
    rank = jnp.dot((tj <= ti).astype(BF16), sel.astype(BF16), preferred_element_type=F32)
    _, _, d_hi, d_lo = _tile_routing(sel, rank, 1)
    rho = lax.broadcasted_iota(jnp.int32, (1, mt), 1).astype(F32)
    onehot = jnp.where(d_hi == rho, 1.0, jnp.where(d_lo == rho, 1.0, 0.0)).astype(BF16)
    o_ref[...] = xm_ref[...] + jnp.dot(onehot, ys_ref[...], preferred_element_type=F32)


def _moe_combine(xm, gates, ys_exp, dst_of_src, tm, mt):
    n, d = xm.shape
    n_e = gates.shape[1]
    bpt = mt // MOE_ROW_ALIGN
    piece = lambda q: pl.BlockSpec((MOE_ROW_ALIGN, d), lambda i, idx: (idx[i * bpt + q], 0))
    return pl.pallas_call(
        _moe_combine_kernel,
        grid_spec=pltpu.PrefetchScalarGridSpec(
            num_scalar_prefetch=1, grid=(n // tm,),
            in_specs=[pl.BlockSpec((tm, d), lambda i, idx: (i, 0)),
                      pl.BlockSpec((tm, n_e), lambda i, idx: (i, 0))] + [piece(q) for q in range(bpt)],
            out_specs=pl.BlockSpec((tm, d), lambda i, idx: (i, 0)),
            scratch_shapes=[pltpu.VMEM((mt, d), BF16)]),
        out_shape=jax.ShapeDtypeStruct((n, d), F32),
        compiler_params=_cparams("arbitrary"),
        name="moe_combine",
    )(dst_of_src, xm, gates, *([ys_exp] * bpt))


def _moe_sparse(h, xm, gates, wg, wu, wd, tf):
    n_e = gates.shape[1]
    tm = MOE_TILE
    mt = TOP_K * tm + n_e * MOE_ROW_ALIGN
    hs_tile, cnt = _moe_dispatch(h, gates, tm, mt)
    src_of_dst, dst_of_src, eid, n_valid, n_dst = _moe_tables(cnt[:, :, 0], mt, MOE_GROUP_ROWS)
    per_step = MOE_GROUP_ROWS // MOE_ROW_ALIGN
    hs_exp = _block_gather(src_of_dst, hs_tile, n_dst, per_step, "moe_gather_blocks")
    ys_exp = _moe_group_ffn(hs_exp, eid, n_valid, wg, wu, wd, MOE_GROUP_ROWS, tf)
    return _moe_combine(xm, gates, ys_exp, dst_of_src, tm, mt)


def _pick(n, candidates):
    for c in candidates:
        if n % c == 0:
            return c
    raise ValueError(f"no tile for {n}")


def _trunk(x, layers, cache_k, cache_v, state_pool, state_shift, state_wkv):
    prompt = cache_k is None
    b, t, d = x.shape
    n = b * t
    tm = _pick(n, (1024, 512, 256, 128))
    tm_ffn = _pick(n, (1024, 512, 256, 128))
    tf = _pick(layers[0]["wg"].shape[-1], (512, 256, 128))
    cq = CHUNK if prompt else t
    rq = _pick(t, (2 * CHUNK,)) if prompt else t
    n_part = _pick(t // rq, (4, 2, 1))
    pool_tb = _pick(t, (2048, 1024, 512, 256, 128, 64, 32))
    r_chunk = CHUNK if prompt else t
    r_tb = _pick(t, (512, 256, 128, 64, 32))
    x2 = x.reshape(n, d)
    nk, nv, npool, nshift, nwkv = [], [], [], [], []
    for l, lp in enumerate(layers):
        proj2 = _norm_matmul(x2, lp["norm1_g"], lp["w_in"], tm)
        proj = proj2.reshape(b, t, -1)
        c_pool = D_A + 2 * D_KV
        c_r = c_pool + D_POOL
        v_raw = proj[:, :, D_A + D_KV:c_pool]
        if prompt:
            cache_kv = None
            prefix = jnp.zeros((b, POOL_HALO, D_POOL), F32)
            prev = jnp.zeros((b, 1, D_R_IN), F32)
            st0 = jnp.zeros((b, HEAD_DIM, D_R), F32)
        else:
            cache_kv = jnp.concatenate([cache_k[l].reshape(b, WINDOW, D_KV),
                                        cache_v[l].reshape(b, WINDOW, D_KV)], axis=-1)
            prefix = jnp.pad(state_pool[l], ((0, 0), (POOL_HALO - POOL_CTX, 0), (0, 0)))
            prev = state_shift[l]
            st0 = _state_to_wide(state_wkv[l])
        ya, k_norm = _attention(proj, cache_kv, lp["q_gain"], lp["k_gain"], lp["sink"], cq, rq, n_part)
        yb = _pool_mix(proj, prefix, lp["pool_w"], lp["pool_scale"], 0 if prompt else POOL_CTX, pool_tb)
        yc, st_fin = _rwkv_mix(proj, prev, st0, lp, r_chunk, r_tb)
        keep = WINDOW if prompt else t
        nk.append(k_norm[:, t - keep:].reshape(b, keep, A_KV_HEADS, HEAD_DIM))
        nv.append(v_raw[:, t - keep:].reshape(b, keep, A_KV_HEADS, HEAD_DIM))
        npool.append(proj[:, t - POOL_CTX:, c_pool:c_r])
        nshift.append(proj[:, t - 1:, c_r:])
        nwkv.append(_wide_to_state(st_fin))
        outs = _mix_out(x2, ya.reshape(n, -1), yb.reshape(n, -1), yc.reshape(n, -1),
                        lp["w_out"], lp["norm2_g"], lp.get("router_w"), tm)
        if "router_w" in lp:
            xm, h2, gates = outs
            if n % MOE_TILE == 0 and n >= 2 * MOE_TILE:
                x2 = _moe_sparse(h2, xm, gates, lp["wg"], lp["wu"], lp["wd"], tf)
            else:
                tf_stream = _pick(lp["wg"].shape[-1], (1792, 512, 256, 128))
                x2 = _moe(h2, xm, gates, lp["wg"], lp["wu"], lp["wd"], tm_ffn, tf_stream)
        else:
            xm, h2 = outs
            x2 = _ffn(h2, xm, lp["wg"], lp["wu"], lp["wd"], tm_ffn, tf)
    return (x2.reshape(b, t, d), jnp.stack(nk), jnp.stack(nv), jnp.stack(npool),
            jnp.stack(nshift), jnp.stack(nwkv))


def kernel(x_prompt, x_sample, cache_k, cache_v, state_pool, state_shift, state_wkv, norm1_g, w_in, q_gain, k_gain, attn_sink, pool_w, pool_scale, shift_mu, decay_w0, decay_w2, iclr_a0, iclr_a2, gate_g2, k_k, k_a, r_k, lnx_g, lnx_b, w_out, norm2_g, ffn_wg, ffn_wu, ffn_wd, router_w, moe_wg, moe_wu, moe_wd):
    depth = w_in.shape[0]
    r_w = decay_w2.shape[1]
    layers = []
    for l in range(depth):
        lp = dict(
            norm1_g=norm1_g[l][None], w_in=w_in[l].astype(BF16),
            q_gain=q_gain[l][None], k_gain=k_gain[l][None], sink=attn_sink[l],
            pool_w=jax.scipy.linalg.block_diag(*[pool_w[l, gi] for gi in range(len(POOL_WINDOWS))]).astype(BF16),
            pool_scale=pool_scale[l][None],
            mu=shift_mu[l][None], w0=decay_w0[l][None], a0=iclr_a0[l][None],
            w2=jnp.pad(decay_w2[l], ((0, R_WA - r_w), (0, 0))).astype(BF16),
            a2=jnp.pad(iclr_a2[l], ((r_w, 0), (0, 0))).astype(BF16),
            g2=gate_g2[l].astype(BF16),
            k_k=k_k[l][None], k_a=k_a[l][None], r_k=r_k[l].reshape(1, D_R),
            lnx_g=lnx_g[l][None], lnx_b=lnx_b[l][None],
            w_out=w_out[l].astype(BF16), norm2_g=norm2_g[l][None])
        if l % 2 == 0:
            lp.update(wg=ffn_wg[l // 2].astype(BF16), wu=ffn_wu[l // 2].astype(BF16),
                      wd=ffn_wd[l // 2].astype(BF16))
        else:
            lp.update(router_w=router_w[l // 2], wg=moe_wg[l // 2].astype(BF16),
                      wu=moe_wu[l // 2].astype(BF16), wd=moe_wd[l // 2].astype(BF16))
        layers.append(lp)
    y_p, pk, pv, ppool, pshift, pwkv = _trunk(x_prompt, layers, None, None, None, None, None)
    y_s, sk, sv, spool, sshift, swkv = _trunk(x_sample, layers, cache_k, cache_v, state_pool,
                                              state_shift, state_wkv)
    return (y_p, y_s, pk, pv, ppool, pshift, pwkv, sk, sv, spool, sshift, swkv)
```

```python
import functools

import jax
import jax.numpy as jnp
from jax import lax
from jax.experimental import pallas as pl
from jax.experimental.pallas import tpu as pltpu

F32 = jnp.float32
BF16 = jnp.bfloat16

HEAD_DIM = 64
A_HEADS = 8
A_KV_HEADS = 2
A_GROUP = A_HEADS // A_KV_HEADS
D_A = A_HEADS * HEAD_DIM
D_KV = A_KV_HEADS * HEAD_DIM
WINDOW = 128
CHUNK = 64
POOL_WINDOWS = (2, 4, 8, 16)
POOL_CTX = 15
POOL_HALO = 16
D_POOL = 256
POOL_GW = D_POOL // len(POOL_WINDOWS)
D_R = 256
R_HEADS = D_R // HEAD_DIM
R_WA = 128
R_G = 128
D_R_IN = 3 * D_R + R_WA + R_G
TOP_K = 2
NORM_EPS = 1e-6
GN_EPS = 64e-5
NEG_INF = -1e30
LOG2_E = 1.4426950408889634
VMEM_LIMIT_BYTES = 56 * 1024 * 1024


def _cparams(*sem):
    return pltpu.CompilerParams(dimension_semantics=sem, vmem_limit_bytes=VMEM_LIMIT_BYTES)


def _mm(a, b):
    return jnp.dot(a.astype(BF16), b.astype(BF16), preferred_element_type=F32)


def _mm_nt(a, b):
    return lax.dot_general(a.astype(BF16), b.astype(BF16), (((1,), (1,)), ((), ())),
                           preferred_element_type=F32)


def _split2(x):
    hi = x.astype(BF16)
    lo = (x - hi.astype(F32)).astype(BF16)
    return hi, lo


def _split3(x):
    hi = x.astype(BF16)
    r1 = x - hi.astype(F32)
    mid = r1.astype(BF16)
    lo = (r1 - mid.astype(F32)).astype(BF16)
    return hi, mid, lo


def _mm_exact_rhs(x, b):
    bb = b.astype(BF16)
    hi, lo = _split2(x)
    dot = lambda u: jnp.dot(u, bb, preferred_element_type=F32)
    return dot(hi) + dot(lo)


def _mm_exact_lhs(a, x):
    ab = a.astype(BF16)
    hi, lo = _split2(x)
    dot = lambda u: jnp.dot(ab, u, preferred_element_type=F32)
    return dot(hi) + dot(lo)


def _mm_hi(a, b):
    ah, al = _split2(a)
    bh, bl = _split2(b)
    dot = lambda u, v: jnp.dot(u, v, preferred_element_type=F32)
    return dot(ah, bh) + (dot(ah, bl) + dot(al, bh))


def _sigmoid(x):
    return 1.0 / (1.0 + jnp.exp(-x))


def _silu(x):
    return x * _sigmoid(x)


def _norm_matmul_kernel(x_ref, g_ref, w_ref, o_ref):
    x = x_ref[...]
    h = x * lax.rsqrt(jnp.mean(x * x, axis=-1, keepdims=True) + NORM_EPS) * g_ref[...]
    o_ref[...] = jnp.dot(h.astype(BF16), w_ref[...], preferred_element_type=F32)


def _norm_matmul(x, g, w, tm):
    n, d = x.shape
    dout = w.shape[1]
    return pl.pallas_call(
        _norm_matmul_kernel,
        grid=(n // tm,),
        in_specs=[pl.BlockSpec((tm, d), lambda i: (i, 0)),
                  pl.BlockSpec((1, d), lambda i: (0, 0)),
                  pl.BlockSpec((d, dout), lambda i: (0, 0))],
        out_specs=pl.BlockSpec((tm, dout), lambda i: (i, 0)),
        out_shape=jax.ShapeDtypeStruct((n, dout), F32),
        compiler_params=_cparams("parallel"),
        name="norm_in_proj",
    )(x, g, w)


def _block_ones(n):
    r = lax.broadcasted_iota(jnp.int32, (n, n), 0) // HEAD_DIM
    c = lax.broadcasted_iota(jnp.int32, (n, n), 1) // HEAD_DIM
    return (r == c).astype(BF16)


def _head_rms_scale(z, ones):
    hi, lo = _split2(z * z)
    ss = jnp.dot(hi, ones, preferred_element_type=F32) + jnp.dot(lo, ones, preferred_element_type=F32)
    return lax.rsqrt(ss * (1.0 / HEAD_DIM) + NORM_EPS)


def _attn_kernel(sink_ref, q_ref, kv_ref, prev_ref, bias_ref, qg_ref, kg_ref, ya_ref, kn_ref,
                 *, rq, n_part, prev_is_raw):
    i = pl.program_id(1)
    kp = WINDOW + rq
    gw = A_GROUP * HEAD_DIM
    q = q_ref[0]
    kv = kv_ref[0]
    pv = prev_ref[0]
    ones_q = _block_ones(D_A)
    ones_k = _block_ones(D_KV)
    qn = (q * _head_rms_scale(q, ones_q) * qg_ref[...]).astype(BF16)
    k_cur = kv[:, 0:D_KV]
    k_cur = k_cur * _head_rms_scale(k_cur, ones_k) * kg_ref[...]
    kn_ref[0] = k_cur
    k_prev = pv[:, 0:D_KV]
    if prev_is_raw:
        k_prev = k_prev * _head_rms_scale(k_prev, ones_k) * kg_ref[...]
    k_all = jnp.concatenate([k_prev, k_cur], axis=0).astype(BF16)
    v_all = jnp.concatenate([pv[:, D_KV:], kv[:, D_KV:]], axis=0).astype(BF16)
    nq = A_GROUP * rq
    lane_in = lax.broadcasted_iota(jnp.int32, (D_KV, gw), 0)
    lane_out = lax.broadcasted_iota(jnp.int32, (D_KV, gw), 1)
    lane_out_t = lax.broadcasted_iota(jnp.int32, (HEAD_DIM, D_KV), 0)
    lane_in_t = lax.broadcasted_iota(jnp.int32, (HEAD_DIM, D_KV), 1)
    slot_lane = lax.broadcasted_iota(jnp.int32, (1, gw), 1) // HEAD_DIM
    slot_mask_bf = [(slot_lane == hh).astype(BF16) for hh in range(A_GROUP)]
    key_row = lax.broadcasted_iota(jnp.int32, (kp, nq), 0)
    col_head = lax.broadcasted_iota(jnp.int32, (1, nq), 1) // rq
    for g in range(A_KV_HEADS):
        select = (lane_in // HEAD_DIM == g) & (lane_in % HEAD_DIM == lane_out % HEAD_DIM)
        k_wide = jnp.dot(k_all, select.astype(BF16), preferred_element_type=F32).astype(BF16)
        v_t = lax.dot_general((lane_in_t == g * HEAD_DIM + lane_out_t).astype(BF16), v_all,
                              (((1,), (1,)), ((), ())), preferred_element_type=F32).astype(BF16)
        q_g = qn[:, g * gw:(g + 1) * gw]
        sink_row = jnp.zeros((1, nq), F32)
        for hh in range(A_GROUP):
            sink_row = jnp.where(col_head == hh, sink_ref[g * A_GROUP + hh], sink_row)
        bias_t = bias_ref[g]
        for p in range(n_part):
            rows = slice(p * rq, (p + 1) * rq)
            qs = jnp.concatenate([q_g[rows] * m for m in slot_mask_bf], axis=0)
            kb = k_wide[p * rq:p * rq + kp]
            s = lax.dot_general(kb, qs, (((1,), (1,)), ((), ())), preferred_element_type=F32) + bias_t
            if prev_is_raw and p * rq < WINDOW:
                n_pad = WINDOW - (i * n_part + p) * rq
                s = s + jnp.where(key_row < n_pad, NEG_INF, 0.0)
            m = jnp.maximum(jnp.max(s, axis=0, keepdims=True), sink_row)
            e = jnp.exp2(s - m)
            den = jnp.sum(e, axis=0, keepdims=True) + jnp.exp2(sink_row - m)
            prob = (e * (1.0 / den)).astype(BF16)
            vb_t = v_t[:, p * rq:p * rq + kp]
            out_t = jnp.concatenate(
                [jnp.dot(vb_t, prob[:, hh * rq:(hh + 1) * rq], preferred_element_type=F32)
                 for hh in range(A_GROUP)], axis=0)
            ya_ref[0, rows, g * gw:(g + 1) * gw] = out_t.T.astype(BF16)


def _attn_bias(rq, cq):
    kp = WINDOW + rq
    i = jnp.arange(rq)[:, None]
    j = jnp.arange(kp)[None, :]
    jb = j - cq * (i // cq)
    valid = (jb >= 0) & (jb < WINDOW + cq)
    dist = jnp.abs(WINDOW + (i % cq) - jb).astype(F32)
    slopes = jnp.exp2(-8.0 * jnp.arange(1, A_HEADS + 1, dtype=F32) / A_HEADS)
    bias = jnp.where(valid[None], -slopes[:, None, None] * dist[None], NEG_INF)
    return jnp.swapaxes(bias.reshape(A_KV_HEADS, A_GROUP * rq, kp), 1, 2) * LOG2_E


def _attention(proj, cache_kv, q_gain, k_gain, sink, cq, rq, n_part):
    b, t, _ = proj.shape
    tq = rq * n_part
    prompt = cache_kv is None
    kv_col = D_A // (2 * D_KV)
    if prompt:
        prev_arr = proj
        prev_spec = pl.BlockSpec((1, WINDOW, 2 * D_KV),
                                 lambda bi, i: (bi, jnp.maximum(i * (tq // WINDOW) - 1, 0), kv_col))
    else:
        prev_arr = cache_kv
        prev_spec = pl.BlockSpec((1, WINDOW, 2 * D_KV), lambda bi, i: (bi, 0, 0))
    kp = WINDOW + rq
    q_gain_t = jnp.tile(q_gain * (HEAD_DIM ** -0.5 * LOG2_E), (1, A_HEADS))
    k_gain_t = jnp.tile(k_gain, (1, A_KV_HEADS))
    kern = functools.partial(_attn_kernel, rq=rq, n_part=n_part, prev_is_raw=prompt)
    return pl.pallas_call(
        kern,
        grid=(b, t // tq),
        in_specs=[pl.BlockSpec(memory_space=pltpu.SMEM),
                  pl.BlockSpec((1, tq, D_A), lambda bi, i: (bi, i, 0)),
                  pl.BlockSpec((1, tq, 2 * D_KV), lambda bi, i: (bi, i, kv_col)),
                  prev_spec,
                  pl.BlockSpec((A_KV_HEADS, kp, A_GROUP * rq), lambda bi, i: (0, 0, 0)),
                  pl.BlockSpec((1, D_A), lambda bi, i: (0, 0)),
                  pl.BlockSpec((1, D_KV), lambda bi, i: (0, 0))],
        out_specs=[pl.BlockSpec((1, tq, D_A), lambda bi, i: (bi, i, 0)),
                   pl.BlockSpec((1, tq, D_KV), lambda bi, i: (bi, i, 0))],
        out_shape=[jax.ShapeDtypeStruct((b, t, D_A), BF16),
                   jax.ShapeDtypeStruct((b, t, D_KV), F32)],
        compiler_params=_cparams("parallel", "arbitrary"),
        name="swa_attention",
    )(sink * LOG2_E, proj, proj, prev_arr, _attn_bias(rq, cq), q_gain_t, k_gain_t)


def _pool_kernel(u_ref, halo_ref, prefix_ref, w_ref, scale_ref, yb_ref, *, n_prefix):
    i = pl.program_id(1)
    u = u_ref[0]
    tb = u.shape[0]
    halo = jnp.where(i == 0, prefix_ref[0], halo_ref[0])
    ext = jnp.concatenate([halo, u], axis=0)
    col = lax.broadcasted_iota(jnp.int32, (1, D_POOL), 1)
    pos = i * tb + lax.broadcasted_iota(jnp.int32, (tb, 1), 0)
    total = None
    count = None
    acc = ext
    span = 1
    for gi, w in enumerate(POOL_WINDOWS):
        while span < w:
            acc = acc + pltpu.roll(acc, span, axis=0)
            span *= 2
        in_group = (col >= gi * POOL_GW) & (col < (gi + 1) * POOL_GW)
        tail = acc[POOL_HALO:]
        total = jnp.where(in_group, tail, 0.0 if total is None else total)
        cnt = jnp.minimum(pos + (1 + n_prefix), w).astype(F32)
        count = jnp.where(in_group, cnt, 1.0 if count is None else count)
    d = total / count - u
    yb_ref[0] = (_mm(d, w_ref[...]) * scale_ref[...]).astype(BF16)


def _pool_mix(proj, prefix, w_blockdiag, scale, n_prefix, tb):
    b, t, _ = proj.shape
    col = (D_A + 2 * D_KV) // D_POOL
    kern = functools.partial(_pool_kernel, n_prefix=n_prefix)
    return pl.pallas_call(
        kern,
        grid=(b, t // tb),
        in_specs=[pl.BlockSpec((1, tb, D_POOL), lambda bi, i: (bi, i, col)),
                  pl.BlockSpec((1, POOL_HALO, D_POOL),
                               lambda bi, i: (bi, jnp.maximum(i * (tb // POOL_HALO) - 1, 0), col)),
                  pl.BlockSpec((1, POOL_HALO, D_POOL), lambda bi, i: (bi, 0, 0)),
                  pl.BlockSpec((D_POOL, D_POOL), lambda bi, i: (0, 0)),
                  pl.BlockSpec((1, D_POOL), lambda bi, i: (0, 0))],
        out_specs=pl.BlockSpec((1, tb, D_POOL), lambda bi, i: (bi, i, 0)),
        out_shape=jax.ShapeDtypeStruct((b, t, D_POOL), BF16),
        compiler_params=_cparams("parallel", "arbitrary"),
        name="pool_mix",
    )(proj, proj, prefix, w_blockdiag, scale)


def _rwkv_kernel(p_ref, prev_ref, st0_ref, mu_ref, w0_ref, a0_ref, w2_ref, a2_ref, g2_ref,
                 kk_ref, ka_ref, rk_ref, lng_ref, lnb_ref, yc_ref, st_ref, carry_ref, y_ref,
                 *, chunk):
    j = pl.program_id(1)

    @pl.when(j == 0)
    def _():
        carry_ref[...] = prev_ref[0]
        st_ref[0] = st0_ref[0]

    p = p_ref[0]
    tb = p.shape[0]
    n_chunk = tb // chunk
    hl = R_HEADS * chunk

    row = lax.broadcasted_iota(jnp.int32, (tb, 1), 0)
    p_prev = jnp.where(row == 0, carry_ref[...], pltpu.roll(p, 1, axis=0))
    carry_ref[...] = p[tb - 1:tb]
    xs = p + mu_ref[...] * (p_prev - p)
    r = xs[:, 0:D_R]
    k = xs[:, D_R:2 * D_R]
    v = xs[:, 2 * D_R:3 * D_R]
    wa = xs[:, 3 * D_R:3 * D_R + R_WA]
    gd = xs[:, 3 * D_R + R_WA:]

    z = -(w0_ref[...] + _mm(jnp.tanh(wa), w2_ref[...]))
    softplus = jnp.maximum(z, 0.0) + jnp.log(1.0 + jnp.exp(-jnp.abs(z)))
    lw = -jnp.exp(-softplus - 0.5)
    a = _sigmoid(a0_ref[...] + _mm(wa, a2_ref[...]))
    g = _mm(_sigmoid(gd), g2_ref[...])

    lane_r = lax.broadcasted_iota(jnp.int32, (D_R, D_R), 0) // HEAD_DIM
    lane_c = lax.broadcasted_iota(jnp.int32, (D_R, D_R), 1) // HEAD_DIM
    head_ones = (lane_r == lane_c).astype(F32)
    seg_sum = lambda t: _mm_exact_rhs(t, head_ones)

    kk = k * kk_ref[...]
    kk = kk / jnp.maximum(jnp.sqrt(seg_sum(kk * kk)), 1e-12)
    k2 = k * (1.0 + (a - 1.0) * ka_ref[...])
    bb = kk * a

    ti = lax.broadcasted_iota(jnp.int32, (tb, tb), 0)
    tj = lax.broadcasted_iota(jnp.int32, (tb, tb), 1)
    cum = _mm_exact_lhs(((ti // chunk == tj // chunk) & (tj <= ti)).astype(F32), lw)

    wi = lax.broadcasted_iota(jnp.int32, (chunk, hl), 0)
    wj = lax.broadcasted_iota(jnp.int32, (chunk, hl), 1) % chunk
    strict = wj < wi
    incl = wj <= wi
    eye_w = (wj == wi).astype(F32)
    diag_k = (lax.broadcasted_iota(jnp.int32, (HEAD_DIM, D_R), 0)
              == lax.broadcasted_iota(jnp.int32, (HEAD_DIM, D_R), 1) % HEAD_DIM)
    lane_k = lax.broadcasted_iota(jnp.int32, (1, D_R), 1) // HEAD_DIM
    lane_t = lax.broadcasted_iota(jnp.int32, (1, hl), 1) // chunk
    mask_k = [(lane_k == h).astype(F32) for h in range(R_HEADS)]
    mask_k_bf = [m.astype(BF16) for m in mask_k]
    mask_t_bf = [(lane_t == h).astype(BF16) for h in range(R_HEADS)]

    def blockdiag(t, masks):
        t16 = t.astype(BF16)
        return jnp.concatenate([t16 * m for m in masks], axis=0)

    def wide_transpose(t):
        tt = jnp.concatenate([t * m for m in mask_k], axis=0).T
        out = tt[0:HEAD_DIM]
        for h in range(1, R_HEADS):
            out = out + tt[h * HEAD_DIM:(h + 1) * HEAD_DIM]
        return out.astype(BF16)

    chunks = []
    for c in range(n_chunk):
        sl = slice(c * chunk, (c + 1) * chunk)
        cum_c = cum[sl]
        cum_last = cum_c[chunk - 1:chunk]
        g_in = jnp.exp(cum_c)
        g_prev = jnp.exp(cum_c - lw[sl])
        g_inv = jnp.exp(-cum_c)
        g_out = jnp.exp(cum_last - cum_c)
        a_n = (kk[sl] * g_prev).astype(BF16)
        r_n = r[sl] * g_in
        ch = dict(sl=sl, r_n=r_n, g_last=jnp.exp(cum_last),
                  a_s=blockdiag(a_n, mask_k_bf),
                  v_s=blockdiag(v[sl], mask_k_bf),
                  bo_w=wide_transpose(bb[sl] * g_out),
                  ko_w=wide_transpose(k2[sl] * g_out))
        ar = jnp.concatenate([a_n, r_n.astype(BF16)], axis=0)
        m_b = _mm_nt(ar, blockdiag(bb[sl] * g_inv, mask_k_bf))
        m_k = _mm_nt(ar, blockdiag(k2[sl] * g_inv, mask_k_bf))
        m_ab = jnp.where(strict, m_b[:chunk], 0.0)
        ch.update(m_rb=jnp.where(incl, m_b[chunk:], 0.0).astype(BF16),
                  m_ak=jnp.where(strict, m_k[:chunk], 0.0).astype(BF16),
                  m_rk=jnp.where(incl, m_k[chunk:], 0.0).astype(BF16),
                  t_inv=eye_w - m_ab, pw=m_ab.astype(BF16))
        chunks.append(ch)

    for ch in chunks:
        ch["pw"] = _mm(ch["pw"], blockdiag(ch["pw"], mask_t_bf)).astype(BF16)
    n = 2
    while n < chunk:
        for ch in chunks:
            rhs = blockdiag(ch["pw"], mask_t_bf)
            if 2 * n < chunk:
                both = _mm(jnp.concatenate([ch["pw"], ch["t_inv"].astype(BF16)], axis=0), rhs)
                ch["pw"] = both[:chunk].astype(BF16)
                ch["t_inv"] = ch["t_inv"] + both[chunk:]
            else:
                ch["t_inv"] = ch["t_inv"] + _mm(ch["t_inv"], rhs)
        n *= 2

    for ch in chunks:
        t_inv = ch["t_inv"].astype(BF16)
        ch["a_bar"] = blockdiag(_mm(t_inv, ch["a_s"]), mask_k_bf)
        on_v = _mm(jnp.concatenate([ch["m_ak"], ch["m_rk"], ch["ko_w"]], axis=0), ch["v_s"])
        ch["mrk_v"], ch["ko_v"] = on_v[chunk:2 * chunk], on_v[2 * chunk:]
        ch["u0"] = blockdiag(-_mm(t_inv, blockdiag(on_v[:chunk], mask_k_bf)), mask_k_bf)
    for ch in chunks:
        lhs = jnp.concatenate([ch["m_rb"], ch["bo_w"]], axis=0)
        on_a = _mm(lhs, ch["a_bar"])
        on_u = _mm(lhs, ch["u0"])
        ch["r_bar"] = (ch["r_n"] - on_a[:chunk]).astype(BF16)
        ch["y0"] = ch["mrk_v"] + on_u[:chunk]
        ch["g_w"] = (jnp.where(diag_k, ch["g_last"], 0.0) - on_a[chunk:]).astype(BF16)
        ch["h_w"] = on_u[chunk:] + ch["ko_v"]

    st_w = st_ref[0]
    for ch in chunks:
        on_st = _mm(jnp.concatenate([ch["r_bar"], ch["g_w"]], axis=0), blockdiag(st_w, mask_k_bf))
        y_ref[ch["sl"], :] = ch["y0"] + on_st[:chunk]
        st_w = on_st[chunk:] + ch["h_w"]
    st_ref[0] = st_w

    y = y_ref[...]
    mean = seg_sum(y) * (1.0 / HEAD_DIM)
    d = y - mean
    var = seg_sum(d * d) * (1.0 / HEAD_DIM)
    yn = d * lax.rsqrt(var + GN_EPS) * lng_ref[...] + lnb_ref[...]
    bonus = seg_sum(r * k2 * rk_ref[...]) * v
    yc_ref[0] = ((yn + bonus) * g).astype(BF16)


def _rwkv_mix(proj, prev, st0, lp, chunk, tb):
    b, t, _ = proj.shape
    col = (D_A + 2 * D_KV + D_POOL) // D_R_IN
    row = lambda n: pl.BlockSpec((1, n), lambda bi, i: (0, 0))
    full = lambda s: pl.BlockSpec(s, lambda bi, i: (0,) * len(s))
    kern = functools.partial(_rwkv_kernel, chunk=chunk)
    return pl.pallas_call(
        kern,
        grid=(b, t // tb),
        in_specs=[pl.BlockSpec((1, tb, D_R_IN), lambda bi, i: (bi, i, col)),
                  pl.BlockSpec((1, 1, D_R_IN), lambda bi, i: (bi, 0, 0)),
                  pl.BlockSpec((1, HEAD_DIM, D_R), lambda bi, i: (bi, 0, 0)),
                  row(D_R_IN), row(D_R), row(D_R),
                  full((R_WA, D_R)), full((R_WA, D_R)), full((R_G, D_R)),
                  row(D_R), row(D_R), row(D_R), row(D_R), row(D_R)],
        out_specs=[pl.BlockSpec((1, tb, D_R), lambda bi, i: (bi, i, 0)),
                   pl.BlockSpec((1, HEAD_DIM, D_R), lambda bi, i: (bi, 0, 0))],
        out_shape=[jax.ShapeDtypeStruct((b, t, D_R), BF16),
                   jax.ShapeDtypeStruct((b, HEAD_DIM, D_R), F32)],
        scratch_shapes=[pltpu.VMEM((1, D_R_IN), F32), pltpu.VMEM((tb, D_R), F32)],
        compiler_params=_cparams("parallel", "arbitrary"),
        name="rwkv7_mix",
    )(proj, prev, st0, lp["mu"], lp["w0"], lp["a0"], lp["w2"], lp["a2"], lp["g2"],
      lp["k_k"], lp["k_a"], lp["r_k"], lp["lnx_g"], lp["lnx_b"])


def _state_to_wide(s):
    b = s.shape[0]
    return jnp.transpose(s, (0, 3, 1, 2)).reshape(b, HEAD_DIM, D_R)


def _wide_to_state(st):
    b = st.shape[0]
    return jnp.transpose(st.reshape(b, HEAD_DIM, R_HEADS, HEAD_DIM), (0, 2, 3, 1))


def _mix_out_kernel(x_ref, ya_ref, yb_ref, yc_ref, w_ref, g_ref, *rest, route):
    if route:
        rw_ref, xm_ref, h_ref, gate_ref = rest
    else:
        xm_ref, h_ref = rest
    dot = lambda u, lo, hi: jnp.dot(u[...], w_ref[lo:hi, :], preferred_element_type=F32)
    xm = (x_ref[...] + dot(ya_ref, 0, D_A) + dot(yb_ref, D_A, D_A + D_POOL)
          + dot(yc_ref, D_A + D_POOL, D_A + D_POOL + D_R))
    xm_ref[...] = xm
    h = xm * lax.rsqrt(jnp.mean(xm * xm, axis=-1, keepdims=True) + NORM_EPS) * g_ref[...]
    h_hi = h.astype(BF16)
    h_ref[...] = h_hi
    if route:
        rw_t = rw_ref[...]
        n_e = rw_t.shape[0]
        rw_hi = rw_t.astype(BF16).astype(F32)
        rw2 = jnp.concatenate([rw_hi, rw_t - rw_hi], axis=0).astype(BF16)
        h_lo = (h - h_hi.astype(F32)).astype(BF16)
        nt = lambda a, b: lax.dot_general(a, b, (((1,), (1,)), ((), ())), preferred_element_type=F32)
        on_hi = nt(rw2, h_hi)
        logits = on_hi[:n_e] + (on_hi[n_e:] + nt(rw2, h_lo)[:n_e])
        row = lax.broadcasted_iota(jnp.int32, logits.shape, 0).astype(F32)
        m1 = jnp.max(logits, axis=0, keepdims=True)
        i1 = jnp.min(jnp.where(logits == m1, row, float(n_e)), axis=0, keepdims=True)
        rest_l = jnp.where(row == i1, -jnp.inf, logits)
        m2 = jnp.max(rest_l, axis=0, keepdims=True)
        i2 = jnp.min(jnp.where(rest_l == m2, row, float(n_e)), axis=0, keepdims=True)
        e2 = jnp.exp(m2 - m1)
        gates_t = (jnp.where(row == i1, 1.0 / (1.0 + e2), 0.0)
                   + jnp.where(row == i2, e2 / (1.0 + e2), 0.0))
        gate_ref[...] = gates_t.T


def _mix_out(x, ya, yb, yc, w_out, g, router_w, tm):
    n, d = x.shape
    route = router_w is not None
    tile = lambda w: pl.BlockSpec((tm, w), lambda i: (i, 0))
    in_specs = [tile(d), tile(D_A), tile(D_POOL), tile(D_R),
                pl.BlockSpec(w_out.shape, lambda i: (0, 0)),
                pl.BlockSpec((1, d), lambda i: (0, 0))]
    out_specs = [tile(d), tile(d)]
    out_shape = [jax.ShapeDtypeStruct((n, d), F32), jax.ShapeDtypeStruct((n, d), BF16)]
    args = [x, ya, yb, yc, w_out, g]
    if route:
        n_e = router_w.shape[1]
        in_specs.append(pl.BlockSpec((n_e, d), lambda i: (0, 0)))
        out_specs.append(tile(n_e))
        out_shape.append(jax.ShapeDtypeStruct((n, n_e), F32))
        args.append(router_w.T)
    return pl.pallas_call(
        functools.partial(_mix_out_kernel, route=route),
        grid=(n // tm,),
        in_specs=in_specs, out_specs=out_specs, out_shape=out_shape,
        compiler_params=_cparams("parallel"),
        name="mix_out_proj",
    )(*args)


def _swiglu_act(h, wg, wu):
    return (_silu(jnp.dot(h, wg, preferred_element_type=F32))
            * jnp.dot(h, wu, preferred_element_type=F32)).astype(BF16)


def _down_proj(act_ref, wd, tf):
    out = None
    for jj in range(act_ref.shape[0]):
        part = jnp.dot(act_ref[jj], wd[jj * tf:(jj + 1) * tf, :], preferred_element_type=F32)
        out = part if out is None else out + part
    return out


def _ffn_kernel(h_ref, xm_ref, wg_ref, wu_ref, wd_ref, o_ref, act_ref):
    j = pl.program_id(1)
    tf = wg_ref.shape[1]
    act_ref[j] = _swiglu_act(h_ref[...], wg_ref[...], wu_ref[...])

    @pl.when(j == pl.num_programs(1) - 1)
    def _():
        o_ref[...] = xm_ref[...] + _down_proj(act_ref, wd_ref, tf)


def _ffn(h, xm, wg, wu, wd, tm, tf):
    n, d = xm.shape
    f = wg.shape[1]
    return pl.pallas_call(
        _ffn_kernel,
        grid=(n // tm, f // tf),
        in_specs=[pl.BlockSpec((tm, d), lambda i, j: (i, 0)),
                  pl.BlockSpec((tm, d), lambda i, j: (i, 0)),
                  pl.BlockSpec((d, tf), lambda i, j: (0, j)),
                  pl.BlockSpec((d, tf), lambda i, j: (0, j)),
                  pl.BlockSpec((f, d), lambda i, j: (0, 0))],
        out_specs=pl.BlockSpec((tm, d), lambda i, j: (i, 0)),
        out_shape=jax.ShapeDtypeStruct((n, d), F32),
        scratch_shapes=[pltpu.VMEM((f // tf, tm, tf), BF16)],
        compiler_params=_cparams("parallel", "arbitrary"),
        name="swiglu_ffn",
    )(h, xm, wg, wu, wd)


def _moe_kernel(h_ref, xm_ref, gate_ref, wg_ref, wu_ref, wd_ref, o_ref, acc_ref):
    e = pl.program_id(1)
    j = pl.program_id(2)

    @pl.when((e == 0) & (j == 0))
    def _():
        acc_ref[...] = xm_ref[...]

    gates = gate_ref[...]
    lane = lax.broadcasted_iota(jnp.int32, gates.shape, 1)
    gate = jnp.sum(jnp.where(lane == e, gates, 0.0), axis=-1, keepdims=True)
    h = h_ref[...]
    act = (_silu(jnp.dot(h, wg_ref[0], preferred_element_type=F32))
           * jnp.dot(h, wu_ref[0], preferred_element_type=F32))
    acc_ref[...] += gate * jnp.dot(act.astype(BF16), wd_ref[0], preferred_element_type=F32)

    @pl.when((e == pl.num_programs(1) - 1) & (j == pl.num_programs(2) - 1))
    def _():
        o_ref[...] = acc_ref[...]


def _moe(h, xm, gates, wg, wu, wd, tm, tf):
    n, d = xm.shape
    n_e, _, f = wg.shape
    return pl.pallas_call(
        _moe_kernel,
        grid=(n // tm, n_e, f // tf),
        in_specs=[pl.BlockSpec((tm, d), lambda i, e, j: (i, 0)),
                  pl.BlockSpec((tm, d), lambda i, e, j: (i, 0)),
                  pl.BlockSpec((tm, n_e), lambda i, e, j: (i, 0)),
                  pl.BlockSpec((1, d, tf), lambda i, e, j: (e, 0, j)),
                  pl.BlockSpec((1, d, tf), lambda i, e, j: (e, 0, j)),
                  pl.BlockSpec((1, tf, d), lambda i, e, j: (e, j, 0))],
        out_specs=pl.BlockSpec((tm, d), lambda i, e, j: (i, 0)),
        out_shape=jax.ShapeDtypeStruct((n, d), F32),
        scratch_shapes=[pltpu.VMEM((tm, d), F32)],
        compiler_params=_cparams("parallel", "arbitrary", "arbitrary"),
        name="moe_ffn",
    )(h, xm, gates, wg, wu, wd)


MOE_ROW_ALIGN = 16
MOE_GATE_LANES = 128
MOE_TILE = 512
MOE_GROUP_ROWS = 1024
_UNSELECTED = 1e9


def _tile_routing(gates_sel, rank, axis):
    n_e = rank.shape[axis]
    cnt = jnp.max(rank, axis=1 - axis, keepdims=True)
    padded = jnp.floor((cnt + (MOE_ROW_ALIGN - 1)) * (1.0 / MOE_ROW_ALIGN)) * MOE_ROW_ALIGN
    offs, run = [], jnp.zeros((1, 1), F32)
    for e in range(n_e):
        offs.append(run)
        run = run + (padded[e:e + 1] if axis == 0 else padded[:, e:e + 1])
    off = jnp.concatenate(offs, axis=axis)
    dest = jnp.where(gates_sel, off + rank - 1.0, -1.0)
    d_hi = jnp.max(dest, axis=axis, keepdims=True)
    d_lo = jnp.min(jnp.where(gates_sel, dest, _UNSELECTED), axis=axis, keepdims=True)
    d_lo = jnp.where(d_lo == d_hi, -2.0, d_lo)
    return cnt, dest, d_hi, d_lo


def _moe_dispatch_kernel(h_ref, gate_ref, hs_ref, cnt_ref, *, chunk_rows):
    h = h_ref[...]
    gates = gate_ref[...]
    tm, n_e = gates.shape
    d = h.shape[1]
    mt = hs_ref.shape[0]
    eye = (lax.broadcasted_iota(jnp.int32, (n_e, n_e), 0)
           == lax.broadcasted_iota(jnp.int32, (n_e, n_e), 1)).astype(BF16)
    to_rows = lambda u: lax.dot_general(eye, u, (((1,), (1,)), ((), ())), preferred_element_type=F32)
    g_hi3, g_mid3, g_lo3 = _split3(gates)
    gate_row = to_rows(g_hi3) + to_rows(g_mid3) + to_rows(g_lo3)
    sel_row = gate_row > 0.0
    ti = lax.broadcasted_iota(jnp.int32, (tm, tm), 0)
    tj = lax.broadcasted_iota(jnp.int32, (tm, tm), 1)
    rank = jnp.dot(sel_row.astype(BF16), (ti <= tj).astype(BF16), preferred_element_type=F32)
    cnt, dest, d_hi, d_lo = _tile_routing(sel_row, rank, 0)
    cnt_ref[0] = jnp.broadcast_to(cnt, cnt_ref.shape[1:])
    g_hi = jnp.sum(jnp.where(dest == d_hi, gate_row, 0.0), axis=0, keepdims=True)
    g_lo = jnp.sum(jnp.where(dest == d_lo, gate_row, 0.0), axis=0, keepdims=True)
    lane = lax.broadcasted_iota(jnp.int32, (chunk_rows, MOE_GATE_LANES), 1)
    for c0 in range(0, mt, chunk_rows):
        rho = (c0 + lax.broadcasted_iota(jnp.int32, (chunk_rows, 1), 0)).astype(F32)
        is_hi = d_hi == rho
        is_lo = d_lo == rho
        onehot = jnp.where(is_hi, 1.0, jnp.where(is_lo, 1.0, 0.0)).astype(BF16)
        rows = jnp.dot(onehot, h, preferred_element_type=F32)
        g = jnp.sum(jnp.where(is_hi, g_hi, jnp.where(is_lo, g_lo, 0.0)), axis=1, keepdims=True)
        p0, p1, p2 = [t.astype(F32) for t in _split3(g)]
        gcols = jnp.where(lane == 0, p0, jnp.where(lane == 1, p1, jnp.where(lane == 2, p2, 0.0)))
        hs_ref[c0:c0 + chunk_rows, 0:d] = rows.astype(BF16)
        hs_ref[c0:c0 + chunk_rows, d:d + MOE_GATE_LANES] = gcols.astype(BF16)


def _moe_dispatch(h, gates, tm, mt):
    n, d = h.shape
    n_e = gates.shape[1]
    n_tiles = n // tm
    return pl.pallas_call(
        functools.partial(_moe_dispatch_kernel, chunk_rows=_pick(mt, (384, 256, 128, 64, 32, 16))),
        grid=(n_tiles,),
        in_specs=[pl.BlockSpec((tm, d), lambda i: (i, 0)),
                  pl.BlockSpec((tm, n_e), lambda i: (i, 0))],
        out_specs=[pl.BlockSpec((mt, d + MOE_GATE_LANES), lambda i: (i, 0)),
                   pl.BlockSpec((1, n_e, 128), lambda i: (i, 0, 0))],
        out_shape=[jax.ShapeDtypeStruct((n_tiles * mt, d + MOE_GATE_LANES), BF16),
                   jax.ShapeDtypeStruct((n_tiles, n_e, 128), F32)],
        compiler_params=_cparams("parallel"),
        name="moe_dispatch",
    )(h, gates)


def _moe_tables(cnt, mt, group_rows):
    i32 = jnp.int32
    n_tiles, n_e = cnt.shape
    bpt = mt // MOE_ROW_ALIGN
    g = group_rows // MOE_ROW_ALIGN
    nblk = jnp.ceil(cnt / MOE_ROW_ALIGN).astype(i32)
    cum = jnp.cumsum(nblk, axis=1)
    off_blk = cum - nblk
    src_base = jnp.arange(n_tiles, dtype=i32)[:, None] * bpt + off_blk
    tot = jnp.sum(nblk, axis=0)
    totp = ((tot + g - 1) // g) * g
    es = jnp.cumsum(totp) - totp
    nblk_t = nblk.T
    seg_start = es[:, None] + jnp.cumsum(nblk_t, axis=1) - nblk_t
    n_src = n_tiles * bpt
    n_dst = -(-(n_src + n_e * (g - 1)) // g) * g
    dblk = jnp.arange(n_dst, dtype=i32)[:, None]
    s0, sl, ss = seg_start.reshape(1, -1), nblk_t.reshape(1, -1), src_base.T.reshape(1, -1)
    hit = (dblk >= s0) & (dblk < s0 + sl)
    src_of_dst = jnp.sum(jnp.where(hit, ss + dblk - s0, 0), axis=1)
    n_rb = n_dst // g
    eid = jnp.sum(jnp.arange(n_rb, dtype=i32)[:, None] * g >= es[None, :], axis=1).astype(i32) - 1
    n_valid = ((es[-1] + totp[-1]) // g).reshape(1)
    o_of = jnp.arange(bpt, dtype=i32)[None, :, None]
    off3, len3 = off_blk[:, None, :], nblk[:, None, :]
    hit3 = (o_of >= off3) & (o_of < off3 + len3)
    dst_of_src = jnp.sum(jnp.where(hit3, seg_start.T[:, None, :] + o_of - off3, 0), axis=2).reshape(-1)
    return src_of_dst.astype(i32), dst_of_src.astype(i32), eid, n_valid.astype(i32), n_dst


def _block_gather_kernel(idx_ref, *refs):
    del idx_ref
    out_ref = refs[-1]
    rows = refs[0].shape[0]
    for q, src_ref in enumerate(refs[:-1]):
        out_ref[q * rows:(q + 1) * rows, :] = src_ref[...]


def _block_gather(idx, src, n_blocks, per_step, name):
    rows = MOE_ROW_ALIGN
    width = src.shape[1]
    pick = lambda q: pl.BlockSpec((rows, width), lambda i, idx: (idx[i * per_step + q], 0))
    return pl.pallas_call(
        _block_gather_kernel,
        grid_spec=pltpu.PrefetchScalarGridSpec(
            num_scalar_prefetch=1, grid=(n_blocks // per_step,),
            in_specs=[pick(q) for q in range(per_step)],
            out_specs=pl.BlockSpec((per_step * rows, width), lambda i, idx: (i, 0))),
        out_shape=jax.ShapeDtypeStruct((n_blocks * rows, width), src.dtype),
        compiler_params=_cparams("arbitrary"),
        name=name,
    )(idx, *([src] * per_step))


def _moe_group_kernel(eid_ref, nv_ref, hs_ref, wg_ref, wu_ref, wd_ref, ys_ref, act_ref):
    del eid_ref
    i = pl.program_id(0)
    j = pl.program_id(1)
    d = ys_ref.shape[1]
    tf = wg_ref.shape[2]

    @pl.when(i < nv_ref[0])
    def _():
        act_ref[j] = _swiglu_act(hs_ref[:, 0:d], wg_ref[0], wu_ref[0])

        @pl.when(j == pl.num_programs(1) - 1)
        def _():
            gp = hs_ref[:, d:d + MOE_GATE_LANES].astype(F32)
            gate = gp[:, 0:1] + gp[:, 1:2] + gp[:, 2:3]
            ys_ref[...] = (_down_proj(act_ref, wd_ref[0], tf) * gate).astype(BF16)

    @pl.when((i >= nv_ref[0]) & (j == 0))
    def _():
        ys_ref[...] = jnp.zeros(ys_ref.shape, BF16)


def _moe_group_ffn(hs, eid, n_valid, wg, wu, wd, rb, tf):
    rows, dw = hs.shape
    d = dw - MOE_GATE_LANES
    f = wg.shape[2]
    nj = f // tf
    live_j = lambda i, j, nv: jnp.where(i < nv[0], j, nj - 1)
    return pl.pallas_call(
        _moe_group_kernel,
        grid_spec=pltpu.PrefetchScalarGridSpec(
            num_scalar_prefetch=2, grid=(rows // rb, nj),
            in_specs=[pl.BlockSpec((rb, dw), lambda i, j, eid, nv: (jnp.minimum(i, nv[0] - 1), 0)),
                      pl.BlockSpec((1, d, tf), lambda i, j, eid, nv: (eid[i], 0, live_j(i, j, nv))),
                      pl.BlockSpec((1, d, tf), lambda i, j, eid, nv: (eid[i], 0, live_j(i, j, nv))),
                      pl.BlockSpec((1, f, d), lambda i, j, eid, nv: (eid[i], 0, 0))],
            out_specs=pl.BlockSpec((rb, d), lambda i, j, eid, nv: (i, 0)),
            scratch_shapes=[pltpu.VMEM((nj, rb, tf), BF16)]),
        out_shape=jax.ShapeDtypeStruct((rows, d), BF16),
        compiler_params=_cparams("arbitrary", "arbitrary"),
        name="moe_group_ffn",
    )(eid, n_valid, hs, wg, wu, wd)


def _moe_combine_kernel(idx_ref, xm_ref, gate_ref, *refs):
    del idx_ref
    o_ref, ys_ref = refs[-2:]
    rows = refs[0].shape[0]
    for q, src_ref in enumerate(refs[:-2]):
        ys_ref[q * rows:(q + 1) * rows, :] = src_ref[...]
    gates = gate_ref[...]
    tm, n_e = gates.shape
    mt = ys_ref.shape[0]
    sel = gates > 0.0
    ti = lax.broadcasted_iota(jnp.int32, (tm, tm), 0)
    tj = lax.broadcasted_iota(jnp.int32, (tm, tm), 1)
    rank = jnp.dot((tj <= ti).astype(BF16), sel.astype(BF16), preferred_element_type=F32)
    _, _, d_hi, d_lo = _tile_routing(sel, rank, 1)
    rho = lax.broadcasted_iota(jnp.int32, (1, mt), 1).astype(F32)
    onehot = jnp.where(d_hi == rho, 1.0, jnp.where(d_lo == rho, 1.0, 0.0)).astype(BF16)
    o_ref[...] = xm_ref[...] + jnp.dot(onehot, ys_ref[...], preferred_element_type=F32)


def _moe_combine(xm, gates, ys_exp, dst_of_src, tm, mt):
    n, d = xm.shape
    n_e = gates.shape[1]
    bpt = mt // MOE_ROW_ALIGN
    piece = lambda q: pl.BlockSpec((MOE_ROW_ALIGN, d), lambda i, idx: (idx[i * bpt + q], 0))
    return pl.pallas_call(
        _moe_combine_kernel,
        grid_spec=pltpu.PrefetchScalarGridSpec(
            num_scalar_prefetch=1, grid=(n // tm,),
            in_specs=[pl.BlockSpec((tm, d), lambda i, idx: (i, 0)),
                      pl.BlockSpec((tm, n_e), lambda i, idx: (i, 0))] + [piece(q) for q in range(bpt)],
            out_specs=pl.BlockSpec((tm, d), lambda i, idx: (i, 0)),
            scratch_shapes=[pltpu.VMEM((mt, d), BF16)]),
        out_shape=jax.ShapeDtypeStruct((n, d), F32),
        compiler_params=_cparams("arbitrary"),
        name="moe_combine",
    )(dst_of_src, xm, gates, *([ys_exp] * bpt))


def _moe_sparse(h, xm, gates, wg, wu, wd, tf):
    n_e = gates.shape[1]
    tm = MOE_TILE
    mt = TOP_K * tm + n_e * MOE_ROW_ALIGN
    hs_tile, cnt = _moe_dispatch(h, gates, tm, mt)
    src_of_dst, dst_of_src, eid, n_valid, n_dst = _moe_tables(cnt[:, :, 0], mt, MOE_GROUP_ROWS)
    per_step = MOE_GROUP_ROWS // MOE_ROW_ALIGN
    hs_exp = _block_gather(src_of_dst, hs_tile, n_dst, per_step, "moe_gather_blocks")
    ys_exp = _moe_group_ffn(hs_exp, eid, n_valid, wg, wu, wd, MOE_GROUP_ROWS, tf)
    return _moe_combine(xm, gates, ys_exp, dst_of_src, tm, mt)


def _pick(n, candidates):
    for c in candidates:
        if n % c == 0:
            return c
    raise ValueError(f"no tile for {n}")


def _trunk(x, layers, cache_k, cache_v, state_pool, state_shift, state_wkv):
    prompt = cache_k is None
    b, t, d = x.shape
    n = b * t
    tm = _pick(n, (1024, 512, 256, 128))
    tm_ffn = _pick(n, (1024, 512, 256, 128))
    tf = _pick(layers[0]["wg"].shape[-1], (512, 256, 128))
    cq = CHUNK if prompt else t
    rq = _pick(t, (2 * CHUNK,)) if prompt else t
    n_part = _pick(t // rq, (4, 2, 1))
    pool_tb = _pick(t, (2048, 1024, 512, 256, 128, 64, 32))
    r_chunk = CHUNK if prompt else t
    r_tb = _pick(t, (512, 256, 128, 64, 32))
    x2 = x.reshape(n, d)
    nk, nv, npool, nshift, nwkv = [], [], [], [], []
    for l, lp in enumerate(layers):
        proj2 = _norm_matmul(x2, lp["norm1_g"], lp["w_in"], tm)
        proj = proj2.reshape(b, t, -1)
        c_pool = D_A + 2 * D_KV
        c_r = c_pool + D_POOL
        v_raw = proj[:, :, D_A + D_KV:c_pool]
        if prompt:
            cache_kv = None
            prefix = jnp.zeros((b, POOL_HALO, D_POOL), F32)
            prev = jnp.zeros((b, 1, D_R_IN), F32)
            st0 = jnp.zeros((b, HEAD_DIM, D_R), F32)
        else:
            cache_kv = jnp.concatenate([cache_k[l].reshape(b, WINDOW, D_KV),
                                        cache_v[l].reshape(b, WINDOW, D_KV)], axis=-1)
            prefix = jnp.pad(state_pool[l], ((0, 0), (POOL_HALO - POOL_CTX, 0), (0, 0)))
            prev = state_shift[l]
            st0 = _state_to_wide(state_wkv[l])
        ya, k_norm = _attention(proj, cache_kv, lp["q_gain"], lp["k_gain"], lp["sink"], cq, rq, n_part)
        yb = _pool_mix(proj, prefix, lp["pool_w"], lp["pool_scale"], 0 if prompt else POOL_CTX, pool_tb)
        yc, st_fin = _rwkv_mix(proj, prev, st0, lp, r_chunk, r_tb)
        keep = WINDOW if prompt else t
        nk.append(k_norm[:, t - keep:].reshape(b, keep, A_KV_HEADS, HEAD_DIM))
        nv.append(v_raw[:, t - keep:].reshape(b, keep, A_KV_HEADS, HEAD_DIM))
        npool.append(proj[:, t - POOL_CTX:, c_pool:c_r])
        nshift.append(proj[:, t - 1:, c_r:])
        nwkv.append(_wide_to_state(st_fin))
        outs = _mix_out(x2, ya.reshape(n, -1), yb.reshape(n, -1), yc.reshape(n, -1),
                        lp["w_out"], lp["norm2_g"], lp.get("router_w"), tm)
        if "router_w" in lp:
            xm, h2, gates = outs
            if n % MOE_TILE == 0 and n >= 2 * MOE_TILE:
                x2 = _moe_sparse(h2, xm, gates, lp["wg"], lp["wu"], lp["wd"], tf)
            else:
                tf_stream = _pick(lp["wg"].shape[-1], (1792, 512, 256, 128))
                x2 = _moe(h2, xm, gates, lp["wg"], lp["wu"], lp["wd"], tm_ffn, tf_stream)
        else:
            xm, h2 = outs
            x2 = _ffn(h2, xm, lp["wg"], lp["wu"], lp["wd"], tm_ffn, tf)
    return (x2.reshape(b, t, d), jnp.stack(nk), jnp.stack(nv), jnp.stack(npool),
            jnp.stack(nshift), jnp.stack(nwkv))


def kernel(x_prompt, x_sample, cache_k, cache_v, state_pool, state_shift, state_wkv, norm1_g, w_in, q_gain, k_gain, attn_sink, pool_w, pool_scale, shift_mu, decay_w0, decay_w2, iclr_a0, iclr_a2, gate_g2, k_k, k_a, r_k, lnx_g, lnx_b, w_out, norm2_g, ffn_wg, ffn_wu, ffn_wd, router_w, moe_wg, moe_wu, moe_wd):
    depth = w_in.shape[0]
    r_w = decay_w2.shape[1]
    layers = []
    for l in range(depth):
        lp = dict(
            norm1_g=norm1_g[l][None], w_in=w_in[l].astype(BF16),
            q_gain=q_gain[l][None], k_gain=k_gain[l][None], sink=attn_sink[l],
            pool_w=jax.scipy.linalg.block_diag(*[pool_w[l, gi] for gi in range(len(POOL_WINDOWS))]).astype(BF16),
            pool_scale=pool_scale[l][None],
            mu=shift_mu[l][None], w0=decay_w0[l][None], a0=iclr_a0[l][None],
            w2=jnp.pad(decay_w2[l], ((0, R_WA - r_w), (0, 0))).astype(BF16),
            a2=jnp.pad(iclr_a2[l], ((r_w, 0), (0, 0))).astype(BF16),
            g2=gate_g2[l].astype(BF16),
            k_k=k_k[l][None], k_a=k_a[l][None], r_k=r_k[l].reshape(1, D_R),
            lnx_g=lnx_g[l][None], lnx_b=lnx_b[l][None],
            w_out=w_out[l].astype(BF16), norm2_g=norm2_g[l][None])
        if l % 2 == 0:
            lp.update(wg=ffn_wg[l // 2].astype(BF16), wu=ffn_wu[l // 2].astype(BF16),
                      wd=ffn_wd[l // 2].astype(BF16))
        else:
            lp.update(router_w=router_w[l // 2], wg=moe_wg[l // 2].astype(BF16),
                      wu=moe_wu[l // 2].astype(BF16), wd=moe_wd[l // 2].astype(BF16))
        layers.append(lp)
    y_p, pk, pv, ppool, pshift, pwkv = _trunk(x_prompt, layers, None, None, None, None, None)
    y_s, sk, sv, spool, sshift, swkv = _trunk(x_sample, layers, cache_k, cache_v, state_pool,
                                              state_shift, state_wkv)
    return (y_p, y_s, pk, pv, ppool, pshift, pwkv, sk, sv, spool, sshift, swkv)
```

```python
import functools

import jax
import jax.numpy as jnp
from jax import lax
from jax.experimental import pallas as pl
from jax.experimental.pallas import tpu as pltpu

F32 = jnp.float32
BF16 = jnp.bfloat16

HEAD_DIM = 64
A_HEADS = 8
A_KV_HEADS = 2
A_GROUP = A_HEADS // A_KV_HEADS
D_A = A_HEADS * HEAD_DIM
D_KV = A_KV_HEADS * HEAD_DIM
WINDOW = 128
CHUNK = 64
POOL_WINDOWS = (2, 4, 8, 16)
POOL_CTX = 15
POOL_HALO = 16
D_POOL = 256
POOL_GW = D_POOL // len(POOL_WINDOWS)
D_R = 256
R_HEADS = D_R // HEAD_DIM
R_WA = 128
R_G = 128
D_R_IN = 3 * D_R + R_WA + R_G
TOP_K = 2
NORM_EPS = 1e-6
GN_EPS = 64e-5
NEG_INF = -1e30
LOG2_E = 1.4426950408889634
VMEM_LIMIT_BYTES = 56 * 1024 * 1024


def _cparams(*sem):
    return pltpu.CompilerParams(dimension_semantics=sem, vmem_limit_bytes=VMEM_LIMIT_BYTES)


def _mm(a, b):
    return jnp.dot(a.astype(BF16), b.astype(BF16), preferred_element_type=F32)


def _mm_nt(a, b):
    return lax.dot_general(a.astype(BF16), b.astype(BF16), (((1,), (1,)), ((), ())),
                           preferred_element_type=F32)


def _split2(x):
    hi = x.astype(BF16)
    lo = (x - hi.astype(F32)).astype(BF16)
    return hi, lo


def _split3(x):
    hi = x.astype(BF16)
    r1 = x - hi.astype(F32)
    mid = r1.astype(BF16)
    lo = (r1 - mid.astype(F32)).astype(BF16)
    return hi, mid, lo


def _mm_split_rhs(x, b):
    bb = b.astype(BF16)
    hi, lo = _split2(x)
    dot = lambda u: jnp.dot(u, bb, preferred_element_type=F32)
    return dot(hi) + dot(lo)


def _mm_split_lhs(a, x):
    ab = a.astype(BF16)
    hi, lo = _split2(x)
    dot = lambda u: jnp.dot(ab, u, preferred_element_type=F32)
    return dot(hi) + dot(lo)


def _sigmoid(x):
    return 1.0 / (1.0 + jnp.exp(-x))


def _silu(x):
    return x * _sigmoid(x)


def _norm_matmul_kernel(x_ref, g_ref, w_ref, o_ref):
    x = x_ref[...]
    h = x * lax.rsqrt(jnp.mean(x * x, axis=-1, keepdims=True) + NORM_EPS) * g_ref[...]
    o_ref[...] = jnp.dot(h.astype(BF16), w_ref[...], preferred_element_type=F32)


def _norm_matmul(x, g, w, tm):
    n, d = x.shape
    dout = w.shape[1]
    return pl.pallas_call(
        _norm_matmul_kernel,
        grid=(n // tm,),
        in_specs=[pl.BlockSpec((tm, d), lambda i: (i, 0)),
                  pl.BlockSpec((1, d), lambda i: (0, 0)),
                  pl.BlockSpec((d, dout), lambda i: (0, 0))],
        out_specs=pl.BlockSpec((tm, dout), lambda i: (i, 0)),
        out_shape=jax.ShapeDtypeStruct((n, dout), F32),
        compiler_params=_cparams("parallel"),
        name="norm_in_proj",
    )(x, g, w)


def _block_ones(n):
    r = lax.broadcasted_iota(jnp.int32, (n, n), 0) // HEAD_DIM
    c = lax.broadcasted_iota(jnp.int32, (n, n), 1) // HEAD_DIM
    return (r == c).astype(BF16)


def _head_rms_scale(z, ones):
    hi, lo = _split2(z * z)
    ss = jnp.dot(hi, ones, preferred_element_type=F32) + jnp.dot(lo, ones, preferred_element_type=F32)
    return lax.rsqrt(ss * (1.0 / HEAD_DIM) + NORM_EPS)


def _attn_kernel(sink_ref, q_ref, kv_ref, prev_ref, bias_ref, qg_ref, kg_ref, ya_ref, kn_ref,
                 *, rq, n_part, prev_is_raw):
    i = pl.program_id(1)
    kp = WINDOW + rq
    gw = A_GROUP * HEAD_DIM
    q = q_ref[0]
    kv = kv_ref[0]
    pv = prev_ref[0]
    ones_q = _block_ones(D_A)
    ones_k = _block_ones(D_KV)
    qn = (q * _head_rms_scale(q, ones_q) * qg_ref[...]).astype(BF16)
    k_cur = kv[:, 0:D_KV]
    k_cur = k_cur * _head_rms_scale(k_cur, ones_k) * kg_ref[...]
    kn_ref[0] = k_cur
    k_prev = pv[:, 0:D_KV]
    if prev_is_raw:
        k_prev = k_prev * _head_rms_scale(k_prev, ones_k) * kg_ref[...]
    k_all = jnp.concatenate([k_prev, k_cur], axis=0).astype(BF16)
    v_all = jnp.concatenate([pv[:, D_KV:], kv[:, D_KV:]], axis=0).astype(BF16)
    nq = A_GROUP * rq
    lane_in = lax.broadcasted_iota(jnp.int32, (D_KV, gw), 0)
    lane_out = lax.broadcasted_iota(jnp.int32, (D_KV, gw), 1)
    lane_out_t = lax.broadcasted_iota(jnp.int32, (HEAD_DIM, D_KV), 0)
    lane_in_t = lax.broadcasted_iota(jnp.int32, (HEAD_DIM, D_KV), 1)
    slot_lane = lax.broadcasted_iota(jnp.int32, (1, gw), 1) // HEAD_DIM
    slot_mask_bf = [(slot_lane == hh).astype(BF16) for hh in range(A_GROUP)]
    key_row = lax.broadcasted_iota(jnp.int32, (kp, nq), 0)
    col_head = lax.broadcasted_iota(jnp.int32, (1, nq), 1) // rq
    for g in range(A_KV_HEADS):
        select = (lane_in // HEAD_DIM == g) & (lane_in % HEAD_DIM == lane_out % HEAD_DIM)
        k_wide = jnp.dot(k_all, select.astype(BF16), preferred_element_type=F32).astype(BF16)
        v_t = lax.dot_general((lane_in_t == g * HEAD_DIM + lane_out_t).astype(BF16), v_all,
                              (((1,), (1,)), ((), ())), preferred_element_type=F32).astype(BF16)
        q_g = qn[:, g * gw:(g + 1) * gw]
        sink_row = jnp.zeros((1, nq), F32)
        for hh in range(A_GROUP):
            sink_row = jnp.where(col_head == hh, sink_ref[g * A_GROUP + hh], sink_row)
        bias_t = bias_ref[g]
        for p in range(n_part):
            rows = slice(p * rq, (p + 1) * rq)
            qs = jnp.concatenate([q_g[rows] * m for m in slot_mask_bf], axis=0)
            kb = k_wide[p * rq:p * rq + kp]
            s = lax.dot_general(kb, qs, (((1,), (1,)), ((), ())), preferred_element_type=F32) + bias_t
            if prev_is_raw and p * rq < WINDOW:
                n_pad = WINDOW - (i * n_part + p) * rq
                s = s + jnp.where(key_row < n_pad, NEG_INF, 0.0)
            m = jnp.maximum(jnp.max(s, axis=0, keepdims=True), sink_row)
            e = jnp.exp2(s - m)
            den = jnp.sum(e, axis=0, keepdims=True) + jnp.exp2(sink_row - m)
            prob = (e * (1.0 / den)).astype(BF16)
            vb_t = v_t[:, p * rq:p * rq + kp]
            out_t = jnp.concatenate(
                [jnp.dot(vb_t, prob[:, hh * rq:(hh + 1) * rq], preferred_element_type=F32)
                 for hh in range(A_GROUP)], axis=0)
            ya_ref[0, rows, g * gw:(g + 1) * gw] = out_t.T.astype(BF16)


def _attn_bias(rq, cq):
    kp = WINDOW + rq
    i = jnp.arange(rq)[:, None]
    j = jnp.arange(kp)[None, :]
    jb = j - cq * (i // cq)
    valid = (jb >= 0) & (jb < WINDOW + cq)
    dist = jnp.abs(WINDOW + (i % cq) - jb).astype(F32)
    slopes = jnp.exp2(-8.0 * jnp.arange(1, A_HEADS + 1, dtype=F32) / A_HEADS)
    bias = jnp.where(valid[None], -slopes[:, None, None] * dist[None], NEG_INF)
    return jnp.swapaxes(bias.reshape(A_KV_HEADS, A_GROUP * rq, kp), 1, 2) * LOG2_E


def _attention(proj, cache_kv, q_gain, k_gain, sink, cq, rq, n_part):
    b, t, _ = proj.shape
    tq = rq * n_part
    prompt = cache_kv is None
    kv_col = D_A // (2 * D_KV)
    if prompt:
        prev_arr = proj
        prev_spec = pl.BlockSpec((1, WINDOW, 2 * D_KV),
                                 lambda bi, i: (bi, jnp.maximum(i * (tq // WINDOW) - 1, 0), kv_col))
    else:
        prev_arr = cache_kv
        prev_spec = pl.BlockSpec((1, WINDOW, 2 * D_KV), lambda bi, i: (bi, 0, 0))
    kp = WINDOW + rq
    q_gain_t = jnp.tile(q_gain * (HEAD_DIM ** -0.5 * LOG2_E), (1, A_HEADS))
    k_gain_t = jnp.tile(k_gain, (1, A_KV_HEADS))
    kern = functools.partial(_attn_kernel, rq=rq, n_part=n_part, prev_is_raw=prompt)
    return pl.pallas_call(
        kern,
        grid=(b, t // tq),
        in_specs=[pl.BlockSpec(memory_space=pltpu.SMEM),
                  pl.BlockSpec((1, tq, D_A), lambda bi, i: (bi, i, 0)),
                  pl.BlockSpec((1, tq, 2 * D_KV), lambda bi, i: (bi, i, kv_col)),
                  prev_spec,
                  pl.BlockSpec((A_KV_HEADS, kp, A_GROUP * rq), lambda bi, i: (0, 0, 0)),
                  pl.BlockSpec((1, D_A), lambda bi, i: (0, 0)),
                  pl.BlockSpec((1, D_KV), lambda bi, i: (0, 0))],
        out_specs=[pl.BlockSpec((1, tq, D_A), lambda bi, i: (bi, i, 0)),
                   pl.BlockSpec((1, tq, D_KV), lambda bi, i: (bi, i, 0))],
        out_shape=[jax.ShapeDtypeStruct((b, t, D_A), BF16),
                   jax.ShapeDtypeStruct((b, t, D_KV), F32)],
        compiler_params=_cparams("parallel", "arbitrary"),
        name="swa_attention",
    )(sink * LOG2_E, proj, proj, prev_arr, _attn_bias(rq, cq), q_gain_t, k_gain_t)


def _pool_kernel(u_ref, halo_ref, prefix_ref, w_ref, scale_ref, yb_ref, *, n_prefix):
    i = pl.program_id(1)
    u = u_ref[0]
    tb = u.shape[0]
    halo = jnp.where(i == 0, prefix_ref[0], halo_ref[0])
    ext = jnp.concatenate([halo, u], axis=0)
    col = lax.broadcasted_iota(jnp.int32, (1, D_POOL), 1)
    pos = i * tb + lax.broadcasted_iota(jnp.int32, (tb, 1), 0)
    total = None
    count = None
    acc = ext
    span = 1
    for gi, w in enumerate(POOL_WINDOWS):
        while span < w:
            acc = acc + pltpu.roll(acc, span, axis=0)
            span *= 2
        in_group = (col >= gi * POOL_GW) & (col < (gi + 1) * POOL_GW)
        tail = acc[POOL_HALO:]
        total = jnp.where(in_group, tail, 0.0 if total is None else total)
        cnt = jnp.minimum(pos + (1 + n_prefix), w).astype(F32)
        count = jnp.where(in_group, cnt, 1.0 if count is None else count)
    d = total / count - u
    yb_ref[0] = (_mm(d, w_ref[...]) * scale_ref[...]).astype(BF16)


def _pool_mix(proj, prefix, w_blockdiag, scale, n_prefix, tb):
    b, t, _ = proj.shape
    col = (D_A + 2 * D_KV) // D_POOL
    kern = functools.partial(_pool_kernel, n_prefix=n_prefix)
    return pl.pallas_call(
        kern,
        grid=(b, t // tb),
        in_specs=[pl.BlockSpec((1, tb, D_POOL), lambda bi, i: (bi, i, col)),
                  pl.BlockSpec((1, POOL_HALO, D_POOL),
                               lambda bi, i: (bi, jnp.maximum(i * (tb // POOL_HALO) - 1, 0), col)),
                  pl.BlockSpec((1, POOL_HALO, D_POOL), lambda bi, i: (bi, 0, 0)),
                  pl.BlockSpec((D_POOL, D_POOL), lambda bi, i: (0, 0)),
                  pl.BlockSpec((1, D_POOL), lambda bi, i: (0, 0))],
        out_specs=pl.BlockSpec((1, tb, D_POOL), lambda bi, i: (bi, i, 0)),
        out_shape=jax.ShapeDtypeStruct((b, t, D_POOL), BF16),
        compiler_params=_cparams("parallel", "arbitrary"),
        name="pool_mix",
    )(proj, proj, prefix, w_blockdiag, scale)


def _rwkv_kernel(p_ref, prev_ref, st0_ref, mu_ref, w0_ref, a0_ref, w2_ref, a2_ref, g2_ref,
                 kk_ref, ka_ref, rk_ref, lng_ref, lnb_ref, yc_ref, st_ref, carry_ref, y_ref,
                 *, chunk):
    j = pl.program_id(1)

    @pl.when(j == 0)
    def _():
        carry_ref[...] = prev_ref[0]
        st_ref[0] = st0_ref[0]

    p = p_ref[0]
    tb = p.shape[0]
    n_chunk = tb // chunk
    hl = R_HEADS * chunk

    row = lax.broadcasted_iota(jnp.int32, (tb, 1), 0)
    p_prev = jnp.where(row == 0, carry_ref[...], pltpu.roll(p, 1, axis=0))
    carry_ref[...] = p[tb - 1:tb]
    xs = p + mu_ref[...] * (p_prev - p)
    r = xs[:, 0:D_R]
    k = xs[:, D_R:2 * D_R]
    v = xs[:, 2 * D_R:3 * D_R]
    wa = xs[:, 3 * D_R:3 * D_R + R_WA]
    gd = xs[:, 3 * D_R + R_WA:]

    z = -(w0_ref[...] + _mm(jnp.tanh(wa), w2_ref[...]))
    softplus = jnp.maximum(z, 0.0) + jnp.log(1.0 + jnp.exp(-jnp.abs(z)))
    lw = -jnp.exp(-softplus - 0.5)
    a = _sigmoid(a0_ref[...] + _mm(wa, a2_ref[...]))
    g = _mm(_sigmoid(gd), g2_ref[...])

    lane_r = lax.broadcasted_iota(jnp.int32, (D_R, D_R), 0) // HEAD_DIM
    lane_c = lax.broadcasted_iota(jnp.int32, (D_R, D_R), 1) // HEAD_DIM
    head_ones = (lane_r == lane_c).astype(F32)
    seg_sum = lambda t: _mm_split_rhs(t, head_ones)

    kk = k * kk_ref[...]
    kk = kk / jnp.maximum(jnp.sqrt(seg_sum(kk * kk)), 1e-12)
    k2 = k * (1.0 + (a - 1.0) * ka_ref[...])
    bb = kk * a

    ti = lax.broadcasted_iota(jnp.int32, (tb, tb), 0)
    tj = lax.broadcasted_iota(jnp.int32, (tb, tb), 1)
    cum = _mm_split_lhs(((ti // chunk == tj // chunk) & (tj <= ti)).astype(F32), lw)

    wi = lax.broadcasted_iota(jnp.int32, (chunk, hl), 0)
    wj = lax.broadcasted_iota(jnp.int32, (chunk, hl), 1) % chunk
    strict = wj < wi
    incl = wj <= wi
    eye_w = (wj == wi).astype(F32)
    diag_k = (lax.broadcasted_iota(jnp.int32, (HEAD_DIM, D_R), 0)
              == lax.broadcasted_iota(jnp.int32, (HEAD_DIM, D_R), 1) % HEAD_DIM)
    lane_k = lax.broadcasted_iota(jnp.int32, (1, D_R), 1) // HEAD_DIM
    lane_t = lax.broadcasted_iota(jnp.int32, (1, hl), 1) // chunk
    mask_k = [(lane_k == h).astype(F32) for h in range(R_HEADS)]
    mask_k_bf = [m.astype(BF16) for m in mask_k]
    mask_t_bf = [(lane_t == h).astype(BF16) for h in range(R_HEADS)]

    def blockdiag(t, masks):
        t16 = t.astype(BF16)
        return jnp.concatenate([t16 * m for m in masks], axis=0)

    def wide_transpose(t):
        tt = jnp.concatenate([t * m for m in mask_k], axis=0).T
        out = tt[0:HEAD_DIM]
        for h in range(1, R_HEADS):
            out = out + tt[h * HEAD_DIM:(h + 1) * HEAD_DIM]
        return out.astype(BF16)

    chunks = []
    for c in range(n_chunk):
        sl = slice(c * chunk, (c + 1) * chunk)
        cum_c = cum[sl]
        cum_last = cum_c[chunk - 1:chunk]
        g_in = jnp.exp(cum_c)
        g_prev = jnp.exp(cum_c - lw[sl])
        g_inv = jnp.exp(-cum_c)
        g_out = jnp.exp(cum_last - cum_c)
        a_n = (kk[sl] * g_prev).astype(BF16)
        r_n = r[sl] * g_in
        ch = dict(sl=sl, r_n=r_n, g_last=jnp.exp(cum_last),
                  a_s=blockdiag(a_n, mask_k_bf),
                  v_s=blockdiag(v[sl], mask_k_bf),
                  bo_w=wide_transpose(bb[sl] * g_out),
                  ko_w=wide_transpose(k2[sl] * g_out))
        ar = jnp.concatenate([a_n, r_n.astype(BF16)], axis=0)
        m_b = _mm_nt(ar, blockdiag(bb[sl] * g_inv, mask_k_bf))
        m_k = _mm_nt(ar, blockdiag(k2[sl] * g_inv, mask_k_bf))
        m_ab = jnp.where(strict, m_b[:chunk], 0.0)
        ch.update(m_rb=jnp.where(incl, m_b[chunk:], 0.0).astype(BF16),
                  m_ak=jnp.where(strict, m_k[:chunk], 0.0).astype(BF16),
                  m_rk=jnp.where(incl, m_k[chunk:], 0.0).astype(BF16),
                  t_inv=eye_w - m_ab, pw=m_ab.astype(BF16))
        chunks.append(ch)

    for ch in chunks:
        ch["pw"] = _mm(ch["pw"], blockdiag(ch["pw"], mask_t_bf)).astype(BF16)
    n = 2
    while n < chunk:
        for ch in chunks:
            rhs = blockdiag(ch["pw"], mask_t_bf)
            if 2 * n < chunk:
                both = _mm(jnp.concatenate([ch["pw"], ch["t_inv"].astype(BF16)], axis=0), rhs)
                ch["pw"] = both[:chunk].astype(BF16)
                ch["t_inv"] = ch["t_inv"] + both[chunk:]
            else:
                ch["t_inv"] = ch["t_inv"] + _mm(ch["t_inv"], rhs)
        n *= 2

    for ch in chunks:
        t_inv = ch["t_inv"].astype(BF16)
        ch["a_bar"] = blockdiag(_mm(t_inv, ch["a_s"]), mask_k_bf)
        on_v = _mm(jnp.concatenate([ch["m_ak"], ch["m_rk"], ch["ko_w"]], axis=0), ch["v_s"])
        ch["mrk_v"], ch["ko_v"] = on_v[chunk:2 * chunk], on_v[2 * chunk:]
        ch["u0"] = blockdiag(-_mm(t_inv, blockdiag(on_v[:chunk], mask_k_bf)), mask_k_bf)
    for ch in chunks:
        lhs = jnp.concatenate([ch["m_rb"], ch["bo_w"]], axis=0)
        on_a = _mm(lhs, ch["a_bar"])
        on_u = _mm(lhs, ch["u0"])
        ch["r_bar"] = (ch["r_n"] - on_a[:chunk]).astype(BF16)
        ch["y0"] = ch["mrk_v"] + on_u[:chunk]
        ch["g_w"] = (jnp.where(diag_k, ch["g_last"], 0.0) - on_a[chunk:]).astype(BF16)
        ch["h_w"] = on_u[chunk:] + ch["ko_v"]

    st_w = st_ref[0]
    for ch in chunks:
        on_st = _mm(jnp.concatenate([ch["r_bar"], ch["g_w"]], axis=0), blockdiag(st_w, mask_k_bf))
        y_ref[ch["sl"], :] = ch["y0"] + on_st[:chunk]
        st_w = on_st[chunk:] + ch["h_w"]
    st_ref[0] = st_w

    y = y_ref[...]
    mean = seg_sum(y) * (1.0 / HEAD_DIM)
    d = y - mean
    var = seg_sum(d * d) * (1.0 / HEAD_DIM)
    yn = d * lax.rsqrt(var + GN_EPS) * lng_ref[...] + lnb_ref[...]
    bonus = seg_sum(r * k2 * rk_ref[...]) * v
    yc_ref[0] = ((yn + bonus) * g).astype(BF16)


def _rwkv_mix(proj, prev, st0, lp, chunk, tb):
    b, t, _ = proj.shape
    col = (D_A + 2 * D_KV + D_POOL) // D_R_IN
    row = lambda n: pl.BlockSpec((1, n), lambda bi, i: (0, 0))
    full = lambda s: pl.BlockSpec(s, lambda bi, i: (0,) * len(s))
    kern = functools.partial(_rwkv_kernel, chunk=chunk)
    return pl.pallas_call(
        kern,
        grid=(b, t // tb),
        in_specs=[pl.BlockSpec((1, tb, D_R_IN), lambda bi, i: (bi, i, col)),
                  pl.BlockSpec((1, 1, D_R_IN), lambda bi, i: (bi, 0, 0)),
                  pl.BlockSpec((1, HEAD_DIM, D_R), lambda bi, i: (bi, 0, 0)),
                  row(D_R_IN), row(D_R), row(D_R),
                  full((R_WA, D_R)), full((R_WA, D_R)), full((R_G, D_R)),
                  row(D_R), row(D_R), row(D_R), row(D_R), row(D_R)],
        out_specs=[pl.BlockSpec((1, tb, D_R), lambda bi, i: (bi, i, 0)),
                   pl.BlockSpec((1, HEAD_DIM, D_R), lambda bi, i: (bi, 0, 0))],
        out_shape=[jax.ShapeDtypeStruct((b, t, D_R), BF16),
                   jax.ShapeDtypeStruct((b, HEAD_DIM, D_R), F32)],
        scratch_shapes=[pltpu.VMEM((1, D_R_IN), F32), pltpu.VMEM((tb, D_R), F32)],
        compiler_params=_cparams("parallel", "arbitrary"),
        name="rwkv7_mix",
    )(proj, prev, st0, lp["mu"], lp["w0"], lp["a0"], lp["w2"], lp["a2"], lp["g2"],
      lp["k_k"], lp["k_a"], lp["r_k"], lp["lnx_g"], lp["lnx_b"])


def _state_to_wide(s):
    b = s.shape[0]
    return jnp.transpose(s, (0, 3, 1, 2)).reshape(b, HEAD_DIM, D_R)


def _wide_to_state(st):
    b = st.shape[0]
    return jnp.transpose(st.reshape(b, HEAD_DIM, R_HEADS, HEAD_DIM), (0, 2, 3, 1))


def _mix_out_kernel(x_ref, ya_ref, yb_ref, yc_ref, w_ref, g_ref, *rest, route):
    if route:
        rw_ref, xm_ref, h_ref, gate_ref = rest
    else:
        xm_ref, h_ref = rest
    dot = lambda u, lo, hi: jnp.dot(u[...], w_ref[lo:hi, :], preferred_element_type=F32)
    xm = (x_ref[...] + dot(ya_ref, 0, D_A) + dot(yb_ref, D_A, D_A + D_POOL)
          + dot(yc_ref, D_A + D_POOL, D_A + D_POOL + D_R))
    xm_ref[...] = xm
    h = xm * lax.rsqrt(jnp.mean(xm * xm, axis=-1, keepdims=True) + NORM_EPS) * g_ref[...]
    h_hi = h.astype(BF16)
    h_ref[...] = h_hi
    if route:
        rw_t = rw_ref[...]
        n_e = rw_t.shape[0]
        rw_hi = rw_t.astype(BF16).astype(F32)
        rw2 = jnp.concatenate([rw_hi, rw_t - rw_hi], axis=0).astype(BF16)
        h_lo = (h - h_hi.astype(F32)).astype(BF16)
        nt = lambda a, b: lax.dot_general(a, b, (((1,), (1,)), ((), ())), preferred_element_type=F32)
        on_hi = nt(rw2, h_hi)
        logits = on_hi[:n_e] + (on_hi[n_e:] + nt(rw2, h_lo)[:n_e])
        row = lax.broadcasted_iota(jnp.int32, logits.shape, 0).astype(F32)
        m1 = jnp.max(logits, axis=0, keepdims=True)
        i1 = jnp.min(jnp.where(logits == m1, row, float(n_e)), axis=0, keepdims=True)
        rest_l = jnp.where(row == i1, -jnp.inf, logits)
        m2 = jnp.max(rest_l, axis=0, keepdims=True)
        i2 = jnp.min(jnp.where(rest_l == m2, row, float(n_e)), axis=0, keepdims=True)
        e2 = jnp.exp(m2 - m1)
        gates_t = (jnp.where(row == i1, 1.0 / (1.0 + e2), 0.0)
                   + jnp.where(row == i2, e2 / (1.0 + e2), 0.0))
        gate_ref[...] = gates_t.T


def _mix_out(x, ya, yb, yc, w_out, g, router_w, tm):
    n, d = x.shape
    route = router_w is not None
    tile = lambda w: pl.BlockSpec((tm, w), lambda i: (i, 0))
    in_specs = [tile(d), tile(D_A), tile(D_POOL), tile(D_R),
                pl.BlockSpec(w_out.shape, lambda i: (0, 0)),
                pl.BlockSpec((1, d), lambda i: (0, 0))]
    out_specs = [tile(d), tile(d)]
    out_shape = [jax.ShapeDtypeStruct((n, d), F32), jax.ShapeDtypeStruct((n, d), BF16)]
    args = [x, ya, yb, yc, w_out, g]
    if route:
        n_e = router_w.shape[1]
        in_specs.append(pl.BlockSpec((n_e, d), lambda i: (0, 0)))
        out_specs.append(tile(n_e))
        out_shape.append(jax.ShapeDtypeStruct((n, n_e), F32))
        args.append(router_w.T)
    return pl.pallas_call(
        functools.partial(_mix_out_kernel, route=route),
        grid=(n // tm,),
        in_specs=in_specs, out_specs=out_specs, out_shape=out_shape,
        compiler_params=_cparams("parallel"),
        name="mix_out_proj",
    )(*args)


def _swiglu_act(h, wg, wu):
    return (_silu(jnp.dot(h, wg, preferred_element_type=F32))
            * jnp.dot(h, wu, preferred_element_type=F32)).astype(BF16)


def _down_proj(act_ref, wd, tf):
    out = None
    for jj in range(act_ref.shape[0]):
        part = jnp.dot(act_ref[jj], wd[jj * tf:(jj + 1) * tf, :], preferred_element_type=F32)
        out = part if out is None else out + part
    return out


def _ffn_kernel(h_ref, xm_ref, wg_ref, wu_ref, wd_ref, o_ref, act_ref):
    j = pl.program_id(1)
    tf = wg_ref.shape[1]
    act_ref[j] = _swiglu_act(h_ref[...], wg_ref[...], wu_ref[...])

    @pl.when(j == pl.num_programs(1) - 1)
    def _():
        o_ref[...] = xm_ref[...] + _down_proj(act_ref, wd_ref, tf)


def _ffn(h, xm, wg, wu, wd, tm, tf):
    n, d = xm.shape
    f = wg.shape[1]
    return pl.pallas_call(
        _ffn_kernel,
        grid=(n // tm, f // tf),
        in_specs=[pl.BlockSpec((tm, d), lambda i, j: (i, 0)),
                  pl.BlockSpec((tm, d), lambda i, j: (i, 0)),
                  pl.BlockSpec((d, tf), lambda i, j: (0, j)),
                  pl.BlockSpec((d, tf), lambda i, j: (0, j)),
                  pl.BlockSpec((f, d), lambda i, j: (0, 0))],
        out_specs=pl.BlockSpec((tm, d), lambda i, j: (i, 0)),
        out_shape=jax.ShapeDtypeStruct((n, d), F32),
        scratch_shapes=[pltpu.VMEM((f // tf, tm, tf), BF16)],
        compiler_params=_cparams("parallel", "arbitrary"),
        name="swiglu_ffn",
    )(h, xm, wg, wu, wd)


def _moe_kernel(h_ref, xm_ref, gate_ref, wg_ref, wu_ref, wd_ref, o_ref, acc_ref):
    e = pl.program_id(1)
    j = pl.program_id(2)

    @pl.when((e == 0) & (j == 0))
    def _():
        acc_ref[...] = xm_ref[...]

    gates = gate_ref[...]
    lane = lax.broadcasted_iota(jnp.int32, gates.shape, 1)
    gate = jnp.sum(jnp.where(lane == e, gates, 0.0), axis=-1, keepdims=True)
    h = h_ref[...]
    act = (_silu(jnp.dot(h, wg_ref[0], preferred_element_type=F32))
           * jnp.dot(h, wu_ref[0], preferred_element_type=F32))
    acc_ref[...] += gate * jnp.dot(act.astype(BF16), wd_ref[0], preferred_element_type=F32)

    @pl.when((e == pl.num_programs(1) - 1) & (j == pl.num_programs(2) - 1))
    def _():
        o_ref[...] = acc_ref[...]


def _moe(h, xm, gates, wg, wu, wd, tm, tf):
    n, d = xm.shape
    n_e, _, f = wg.shape
    return pl.pallas_call(
        _moe_kernel,
        grid=(n // tm, n_e, f // tf),
        in_specs=[pl.BlockSpec((tm, d), lambda i, e, j: (i, 0)),
                  pl.BlockSpec((tm, d), lambda i, e, j: (i, 0)),
                  pl.BlockSpec((tm, n_e), lambda i, e, j: (i, 0)),
                  pl.BlockSpec((1, d, tf), lambda i, e, j: (e, 0, j)),
                  pl.BlockSpec((1, d, tf), lambda i, e, j: (e, 0, j)),
                  pl.BlockSpec((1, tf, d), lambda i, e, j: (e, j, 0))],
        out_specs=pl.BlockSpec((tm, d), lambda i, e, j: (i, 0)),
        out_shape=jax.ShapeDtypeStruct((n, d), F32),
        scratch_shapes=[pltpu.VMEM((tm, d), F32)],
        compiler_params=_cparams("parallel", "arbitrary", "arbitrary"),
        name="moe_ffn",
    )(h, xm, gates, wg, wu, wd)


MOE_ROW_ALIGN = 16
MOE_GATE_LANES = 128
MOE_TILE = 512
MOE_GROUP_ROWS = 1024
_UNSELECTED = 1e9


def _tile_routing(gates_sel, rank, axis):
    n_e = rank.shape[axis]
    cnt = jnp.max(rank, axis=1 - axis, keepdims=True)
    padded = jnp.floor((cnt + (MOE_ROW_ALIGN - 1)) * (1.0 / MOE_ROW_ALIGN)) * MOE_ROW_ALIGN
    offs, run = [], jnp.zeros((1, 1), F32)
    for e in range(n_e):
        offs.append(run)
        run = run + (padded[e:e + 1] if axis == 0 else padded[:, e:e + 1])
    off = jnp.concatenate(offs, axis=axis)
    dest = jnp.where(gates_sel, off + rank - 1.0, -1.0)
    d_hi = jnp.max(dest, axis=axis, keepdims=True)
    d_lo = jnp.min(jnp.where(gates_sel, dest, _UNSELECTED), axis=axis, keepdims=True)
    d_lo = jnp.where(d_lo == d_hi, -2.0, d_lo)
    return cnt, dest, d_hi, d_lo


def _moe_dispatch_kernel(h_ref, gate_ref, hs_ref, cnt_ref, *, chunk_rows):
    h = h_ref[...]
    gates = gate_ref[...]
    tm, n_e = gates.shape
    d = h.shape[1]
    mt = hs_ref.shape[0]
    eye = (lax.broadcasted_iota(jnp.int32, (n_e, n_e), 0)
           == lax.broadcasted_iota(jnp.int32, (n_e, n_e), 1)).astype(BF16)
    to_rows = lambda u: lax.dot_general(eye, u, (((1,), (1,)), ((), ())), preferred_element_type=F32)
    g_hi3, g_mid3, g_lo3 = _split3(gates)
    gate_row = to_rows(g_hi3) + to_rows(g_mid3) + to_rows(g_lo3)
    sel_row = gate_row > 0.0
    ti = lax.broadcasted_iota(jnp.int32, (tm, tm), 0)
    tj = lax.broadcasted_iota(jnp.int32, (tm, tm), 1)
    rank = jnp.dot(sel_row.astype(BF16), (ti <= tj).astype(BF16), preferred_element_type=F32)
    cnt, dest, d_hi, d_lo = _tile_routing(sel_row, rank, 0)
    cnt_ref[0] = jnp.broadcast_to(cnt, cnt_ref.shape[1:])
    g_hi = jnp.sum(jnp.where(dest == d_hi, gate_row, 0.0), axis=0, keepdims=True)
    g_lo = jnp.sum(jnp.where(dest == d_lo, gate_row, 0.0), axis=0, keepdims=True)
    lane = lax.broadcasted_iota(jnp.int32, (chunk_rows, MOE_GATE_LANES), 1)
    for c0 in range(0, mt, chunk_rows):
        rho = (c0 + lax.broadcasted_iota(jnp.int32, (chunk_rows, 1), 0)).astype(F32)
        is_hi = d_hi == rho
        is_lo = d_lo == rho
        onehot = jnp.where(is_hi, 1.0, jnp.where(is_lo, 1.0, 0.0)).astype(BF16)
        rows = jnp.dot(onehot, h, preferred_element_type=F32)
        g = jnp.sum(jnp.where(is_hi, g_hi, jnp.where(is_lo, g_lo, 0.0)), axis=1, keepdims=True)
        p0, p1, p2 = [t.astype(F32) for t in _split3(g)]
        gcols = jnp.where(lane == 0, p0, jnp.where(lane == 1, p1, jnp.where(lane == 2, p2, 0.0)))
        hs_ref[c0:c0 + chunk_rows, 0:d] = rows.astype(BF16)
        hs_ref[c0:c0 + chunk_rows, d:d + MOE_GATE_LANES] = gcols.astype(BF16)


def _moe_dispatch(h, gates, tm, mt):
    n, d = h.shape
    n_e = gates.shape[1]
    n_tiles = n // tm
    return pl.pallas_call(
        functools.partial(_moe_dispatch_kernel, chunk_rows=_pick(mt, (384, 256, 128, 64, 32, 16))),
        grid=(n_tiles,),
        in_specs=[pl.BlockSpec((tm, d), lambda i: (i, 0)),
                  pl.BlockSpec((tm, n_e), lambda i: (i, 0))],
        out_specs=[pl.BlockSpec((mt, d + MOE_GATE_LANES), lambda i: (i, 0)),
                   pl.BlockSpec((1, n_e, 128), lambda i: (i, 0, 0))],
        out_shape=[jax.ShapeDtypeStruct((n_tiles * mt, d + MOE_GATE_LANES), BF16),
                   jax.ShapeDtypeStruct((n_tiles, n_e, 128), F32)],
        compiler_params=_cparams("parallel"),
        name="moe_dispatch",
    )(h, gates)


def _moe_tables(cnt, mt, group_rows):
    i32 = jnp.int32
    n_tiles, n_e = cnt.shape
    bpt = mt // MOE_ROW_ALIGN
    g = group_rows // MOE_ROW_ALIGN
    nblk = jnp.ceil(cnt / MOE_ROW_ALIGN).astype(i32)
    cum = jnp.cumsum(nblk, axis=1)
    off_blk = cum - nblk
    src_base = jnp.arange(n_tiles, dtype=i32)[:, None] * bpt + off_blk
    tot = jnp.sum(nblk, axis=0)
    totp = ((tot + g - 1) // g) * g
    es = jnp.cumsum(totp) - totp
    nblk_t = nblk.T
    seg_start = es[:, None] + jnp.cumsum(nblk_t, axis=1) - nblk_t
    n_src = n_tiles * bpt
    n_dst = -(-(n_src + n_e * (g - 1)) // g) * g
    dblk = jnp.arange(n_dst, dtype=i32)[:, None]
    s0, sl, ss = seg_start.reshape(1, -1), nblk_t.reshape(1, -1), src_base.T.reshape(1, -1)
    hit = (dblk >= s0) & (dblk < s0 + sl)
    src_of_dst = jnp.sum(jnp.where(hit, ss + dblk - s0, 0), axis=1)
    n_rb = n_dst // g
    eid = jnp.sum(jnp.arange(n_rb, dtype=i32)[:, None] * g >= es[None, :], axis=1).astype(i32) - 1
    n_valid = ((es[-1] + totp[-1]) // g).reshape(1)
    o_of = jnp.arange(bpt, dtype=i32)[None, :, None]
    off3, len3 = off_blk[:, None, :], nblk[:, None, :]
    hit3 = (o_of >= off3) & (o_of < off3 + len3)
    dst_of_src = jnp.sum(jnp.where(hit3, seg_start.T[:, None, :] + o_of - off3, 0), axis=2).reshape(-1)
    return src_of_dst.astype(i32), dst_of_src.astype(i32), eid, n_valid.astype(i32), n_dst


def _block_gather_kernel(idx_ref, *refs):
    del idx_ref
    out_ref = refs[-1]
    rows = refs[0].shape[0]
    for q, src_ref in enumerate(refs[:-1]):
        out_ref[q * rows:(q + 1) * rows, :] = src_ref[...]


def _block_gather(idx, src, n_blocks, per_step, name):
    rows = MOE_ROW_ALIGN
    width = src.shape[1]
    pick = lambda q: pl.BlockSpec((rows, width), lambda i, idx: (idx[i * per_step + q], 0))
    return pl.pallas_call(
        _block_gather_kernel,
        grid_spec=pltpu.PrefetchScalarGridSpec(
            num_scalar_prefetch=1, grid=(n_blocks // per_step,),
            in_specs=[pick(q) for q in range(per_step)],
            out_specs=pl.BlockSpec((per_step * rows, width), lambda i, idx: (i, 0))),
        out_shape=jax.ShapeDtypeStruct((n_blocks * rows, width), src.dtype),
        compiler_params=_cparams("arbitrary"),
        name=name,
    )(idx, *([src] * per_step))


def _moe_group_kernel(eid_ref, nv_ref, hs_ref, wg_ref, wu_ref, wd_ref, ys_ref, act_ref):
    del eid_ref
    i = pl.program_id(0)
    j = pl.program_id(1)
    d = ys_ref.shape[1]
    tf = wg_ref.shape[2]

    @pl.when(i < nv_ref[0])
    def _():
        act_ref[j] = _swiglu_act(hs_ref[:, 0:d], wg_ref[0], wu_ref[0])

        @pl.when(j == pl.num_programs(1) - 1)
        def _():
            gp = hs_ref[:, d:d + MOE_GATE_LANES].astype(F32)
            gate = gp[:, 0:1] + gp[:, 1:2] + gp[:, 2:3]
            ys_ref[...] = (_down_proj(act_ref, wd_ref[0], tf) * gate).astype(BF16)

    @pl.when((i >= nv_ref[0]) & (j == 0))
    def _():
        ys_ref[...] = jnp.zeros(ys_ref.shape, BF16)


def _moe_group_ffn(hs, eid, n_valid, wg, wu, wd, rb, tf):
    rows, dw = hs.shape
    d = dw - MOE_GATE_LANES
    f = wg.shape[2]
    nj = f // tf
    live_j = lambda i, j, nv: jnp.where(i < nv[0], j, nj - 1)
    return pl.pallas_call(
        _moe_group_kernel,
        grid_spec=pltpu.PrefetchScalarGridSpec(
            num_scalar_prefetch=2, grid=(rows // rb, nj),
            in_specs=[pl.BlockSpec((rb, dw), lambda i, j, eid, nv: (jnp.minimum(i, nv[0] - 1), 0)),
                      pl.BlockSpec((1, d, tf), lambda i, j, eid, nv: (eid[i], 0, live_j(i, j, nv))),
                      pl.BlockSpec((1, d, tf), lambda i, j, eid, nv: (eid[i], 0, live_j(i, j, nv))),
                      pl.BlockSpec((1, f, d), lambda i, j, eid, nv: (eid[i], 0, 0))],
            out_specs=pl.BlockSpec((rb, d), lambda i, j, eid, nv: (i, 0)),
            scratch_shapes=[pltpu.VMEM((nj, rb, tf), BF16)]),
        out_shape=jax.ShapeDtypeStruct((rows, d), BF16),
        compiler_params=_cparams("arbitrary", "arbitrary"),
        name="moe_group_ffn",
    )(eid, n_valid, hs, wg, wu, wd)


def _moe_combine_kernel(idx_ref, xm_ref, gate_ref, *refs):
    del idx_ref
    o_ref, ys_ref = refs[-2:]
    rows = refs[0].shape[0]
    for q, src_ref in enumerate(refs[:-2]):
        ys_ref[q * rows:(q + 1) * rows, :] = src_ref[...]
    gates = gate_ref[...]
    tm, n_e = gates.shape
    mt = ys_ref.shape[0]
    sel = gates > 0.0
    ti = lax.broadcasted_iota(jnp.int32, (tm, tm), 0)
    tj = lax.broadcasted_iota(jnp.int32, (tm, tm), 1)
    rank = jnp.dot((tj <= ti).astype(BF16), sel.astype(BF16), preferred_element_type=F32)
    _, _, d_hi, d_lo = _tile_routing(sel, rank, 1)
    rho = lax.broadcasted_iota(jnp.int32, (1, mt), 1).astype(F32)
    onehot = jnp.where(d_hi == rho, 1.0, jnp.where(d_lo == rho, 1.0, 0.0)).astype(BF16)
    o_ref[...] = xm_ref[...] + jnp.dot(onehot, ys_ref[...], preferred_element_type=F32)


def _moe_combine(xm, gates, ys_exp, dst_of_src, tm, mt):
    n, d = xm.shape
    n_e = gates.shape[1]
    bpt = mt // MOE_ROW_ALIGN
    piece = lambda q: pl.BlockSpec((MOE_ROW_ALIGN, d), lambda i, idx: (idx[i * bpt + q], 0))
    return pl.pallas_call(
        _moe_combine_kernel,
        grid_spec=pltpu.PrefetchScalarGridSpec(
            num_scalar_prefetch=1, grid=(n // tm,),
            in_specs=[pl.BlockSpec((tm, d), lambda i, idx: (i, 0)),
                      pl.BlockSpec((tm, n_e), lambda i, idx: (i, 0))] + [piece(q) for q in range(bpt)],
            out_specs=pl.BlockSpec((tm, d), lambda i, idx: (i, 0)),
            scratch_shapes=[pltpu.VMEM((mt, d), BF16)]),
        out_shape=jax.ShapeDtypeStruct((n, d), F32),
        compiler_params=_cparams("arbitrary"),
        name="moe_combine",
    )(dst_of_src, xm, gates, *([ys_exp] * bpt))


def _moe_sparse(h, xm, gates, wg, wu, wd, tf):
    n_e = gates.shape[1]
    tm = MOE_TILE
    mt = TOP_K * tm + n_e * MOE_ROW_ALIGN
    hs_tile, cnt = _moe_dispatch(h, gates, tm, mt)
    src_of_dst, dst_of_src, eid, n_valid, n_dst = _moe_tables(cnt[:, :, 0], mt, MOE_GROUP_ROWS)
    per_step = MOE_GROUP_ROWS // MOE_ROW_ALIGN
    hs_exp = _block_gather(src_of_dst, hs_tile, n_dst, per_step, "moe_gather_blocks")
    ys_exp = _moe_group_ffn(hs_exp, eid, n_valid, wg, wu, wd, MOE_GROUP_ROWS, tf)
    return _moe_combine(xm, gates, ys_exp, dst_of_src, tm, mt)


def _pick(n, candidates):
    for c in candidates:
        if n % c == 0:
            return c
    raise ValueError(f"no tile for {n}")


def _trunk(x, layers, cache_k, cache_v, state_pool, state_shift, state_wkv):
    prompt = cache_k is None
    b, t, d = x.shape
    n = b * t
    tm = _pick(n, (1024, 512, 256, 128))
    tm_ffn = _pick(n, (1024, 512, 256, 128))
    tf = _pick(layers[0]["wg"].shape[-1], (512, 256, 128))
    cq = CHUNK if prompt else t
    rq = _pick(t, (2 * CHUNK,)) if prompt else t
    n_part = _pick(t // rq, (8, 4, 2, 1))
    pool_tb = _pick(t, (2048, 1024, 512, 256, 128, 64, 32))
    r_chunk = CHUNK if prompt else t
    r_tb = _pick(t, (512, 256, 128, 64, 32))
    x2 = x.reshape(n, d)
    nk, nv, npool, nshift, nwkv = [], [], [], [], []
    for l, lp in enumerate(layers):
        proj2 = _norm_matmul(x2, lp["norm1_g"], lp["w_in"], tm)
        proj = proj2.reshape(b, t, -1)
        c_pool = D_A + 2 * D_KV
        c_r = c_pool + D_POOL
        v_raw = proj[:, :, D_A + D_KV:c_pool]
        if prompt:
            cache_kv = None
            prefix = jnp.zeros((b, POOL_HALO, D_POOL), F32)
            prev = jnp.zeros((b, 1, D_R_IN), F32)
            st0 = jnp.zeros((b, HEAD_DIM, D_R), F32)
        else:
            cache_kv = jnp.concatenate([cache_k[l].reshape(b, WINDOW, D_KV),
                                        cache_v[l].reshape(b, WINDOW, D_KV)], axis=-1)
            prefix = jnp.pad(state_pool[l], ((0, 0), (POOL_HALO - POOL_CTX, 0), (0, 0)))
            prev = state_shift[l]
            st0 = _state_to_wide(state_wkv[l])
        ya, k_norm = _attention(proj, cache_kv, lp["q_gain"], lp["k_gain"], lp["sink"], cq, rq, n_part)
        yb = _pool_mix(proj, prefix, lp["pool_w"], lp["pool_scale"], 0 if prompt else POOL_CTX, pool_tb)
        yc, st_fin = _rwkv_mix(proj, prev, st0, lp, r_chunk, r_tb)
        keep = WINDOW if prompt else t
        nk.append(k_norm[:, t - keep:].reshape(b, keep, A_KV_HEADS, HEAD_DIM))
        nv.append(v_raw[:, t - keep:].reshape(b, keep, A_KV_HEADS, HEAD_DIM))
        npool.append(proj[:, t - POOL_CTX:, c_pool:c_r])
        nshift.append(proj[:, t - 1:, c_r:])
        nwkv.append(_wide_to_state(st_fin))
        outs = _mix_out(x2, ya.reshape(n, -1), yb.reshape(n, -1), yc.reshape(n, -1),
                        lp["w_out"], lp["norm2_g"], lp.get("router_w"), tm)
        if "router_w" in lp:
            xm, h2, gates = outs
            if n % MOE_TILE == 0 and n >= 2 * MOE_TILE:
                x2 = _moe_sparse(h2, xm, gates, lp["wg"], lp["wu"], lp["wd"], tf)
            else:
                tf_stream = _pick(lp["wg"].shape[-1], (1792, 512, 256, 128))
                x2 = _moe(h2, xm, gates, lp["wg"], lp["wu"], lp["wd"], tm_ffn, tf_stream)
        else:
            xm, h2 = outs
            x2 = _ffn(h2, xm, lp["wg"], lp["wu"], lp["wd"], tm_ffn, tf)
    return (x2.reshape(b, t, d), jnp.stack(nk), jnp.stack(nv), jnp.stack(npool),
            jnp.stack(nshift), jnp.stack(nwkv))


def kernel(x_prompt, x_sample, cache_k, cache_v, state_pool, state_shift, state_wkv, norm1_g, w_in, q_gain, k_gain, attn_sink, pool_w, pool_scale, shift_mu, decay_w0, decay_w2, iclr_a0, iclr_a2, gate_g2, k_k, k_a, r_k, lnx_g, lnx_b, w_out, norm2_g, ffn_wg, ffn_wu, ffn_wd, router_w, moe_wg, moe_wu, moe_wd):
    depth = w_in.shape[0]
    r_w = decay_w2.shape[1]
    layers = []
    for l in range(depth):
        lp = dict(
            norm1_g=norm1_g[l][None], w_in=w_in[l].astype(BF16),
            q_gain=q_gain[l][None], k_gain=k_gain[l][None], sink=attn_sink[l],
            pool_w=jax.scipy.linalg.block_diag(*[pool_w[l, gi] for gi in range(len(POOL_WINDOWS))]).astype(BF16),
            pool_scale=pool_scale[l][None],
            mu=shift_mu[l][None], w0=decay_w0[l][None], a0=iclr_a0[l][None],
            w2=jnp.pad(decay_w2[l], ((0, R_WA - r_w), (0, 0))).astype(BF16),
            a2=jnp.pad(iclr_a2[l], ((r_w, 0), (0, 0))).astype(BF16),
            g2=gate_g2[l].astype(BF16),
            k_k=k_k[l][None], k_a=k_a[l][None], r_k=r_k[l].reshape(1, D_R),
            lnx_g=lnx_g[l][None], lnx_b=lnx_b[l][None],
            w_out=w_out[l].astype(BF16), norm2_g=norm2_g[l][None])
        if l % 2 == 0:
            lp.update(wg=ffn_wg[l // 2].astype(BF16), wu=ffn_wu[l // 2].astype(BF16),
                      wd=ffn_wd[l // 2].astype(BF16))
        else:
            lp.update(router_w=router_w[l // 2], wg=moe_wg[l // 2].astype(BF16),
                      wu=moe_wu[l // 2].astype(BF16), wd=moe_wd[l // 2].astype(BF16))
        layers.append(lp)
    y_p, pk, pv, ppool, pshift, pwkv = _trunk(x_prompt, layers, None, None, None, None, None)
    y_s, sk, sv, spool, sshift, swkv = _trunk(x_sample, layers, cache_k, cache_v, state_pool,
                                              state_shift, state_wkv)
    return (y_p, y_s, pk, pv, ppool, pshift, pwkv, sk, sv, spool, sshift, swkv)
```

```python
import functools

import jax
import jax.numpy as jnp
from jax import lax
from jax.experimental import pallas as pl
from jax.experimental.pallas import tpu as pltpu

F32 = jnp.float32
BF16 = jnp.bfloat16

HEAD_DIM = 64
A_HEADS = 8
A_KV_HEADS = 2
A_GROUP = A_HEADS // A_KV_HEADS
D_A = A_HEADS * HEAD_DIM
D_KV = A_KV_HEADS * HEAD_DIM
WINDOW = 128
CHUNK = 64
POOL_WINDOWS = (2, 4, 8, 16)
POOL_CTX = 15
POOL_HALO = 16
D_POOL = 256
POOL_GW = D_POOL // len(POOL_WINDOWS)
D_R = 256
R_HEADS = D_R // HEAD_DIM
R_WA = 128
R_G = 128
D_R_IN = 3 * D_R + R_WA + R_G
TOP_K = 2
NORM_EPS = 1e-6
GN_EPS = 64e-5
NEG_INF = -1e30
LOG2_E = 1.4426950408889634
VMEM_LIMIT_BYTES = 56 * 1024 * 1024


def _cparams(*sem):
    return pltpu.CompilerParams(dimension_semantics=sem, vmem_limit_bytes=VMEM_LIMIT_BYTES)


def _mm(a, b):
    return jnp.dot(a.astype(BF16), b.astype(BF16), preferred_element_type=F32)


def _mm_nt(a, b):
    return lax.dot_general(a.astype(BF16), b.astype(BF16), (((1,), (1,)), ((), ())),
                           preferred_element_type=F32)


def _split2(x):
    hi = x.astype(BF16)
    lo = (x - hi.astype(F32)).astype(BF16)
    return hi, lo


def _split3(x):
    hi = x.astype(BF16)
    r1 = x - hi.astype(F32)
    mid = r1.astype(BF16)
    lo = (r1 - mid.astype(F32)).astype(BF16)
    return hi, mid, lo


def _mm_split_rhs(x, b):
    bb = b.astype(BF16)
    hi, lo = _split2(x)
    dot = lambda u: jnp.dot(u, bb, preferred_element_type=F32)
    return dot(hi) + dot(lo)


def _mm_split_lhs(a, x):
    ab = a.astype(BF16)
    hi, lo = _split2(x)
    dot = lambda u: jnp.dot(ab, u, preferred_element_type=F32)
    return dot(hi) + dot(lo)


def _sigmoid(x):
    return 1.0 / (1.0 + jnp.exp(-x))


def _silu(x):
    return x * _sigmoid(x)


def _norm_matmul_kernel(x_ref, g_ref, w_ref, o_ref):
    x = x_ref[...]
    h = x * lax.rsqrt(jnp.mean(x * x, axis=-1, keepdims=True) + NORM_EPS) * g_ref[...]
    o_ref[...] = jnp.dot(h.astype(BF16), w_ref[...], preferred_element_type=F32)


def _norm_matmul(x, g, w, tm):
    n, d = x.shape
    dout = w.shape[1]
    return pl.pallas_call(
        _norm_matmul_kernel,
        grid=(n // tm,),
        in_specs=[pl.BlockSpec((tm, d), lambda i: (i, 0)),
                  pl.BlockSpec((1, d), lambda i: (0, 0)),
                  pl.BlockSpec((d, dout), lambda i: (0, 0))],
        out_specs=pl.BlockSpec((tm, dout), lambda i: (i, 0)),
        out_shape=jax.ShapeDtypeStruct((n, dout), F32),
        compiler_params=_cparams("parallel"),
        name="norm_in_proj",
    )(x, g, w)


def _block_ones(n):
    r = lax.broadcasted_iota(jnp.int32, (n, n), 0) // HEAD_DIM
    c = lax.broadcasted_iota(jnp.int32, (n, n), 1) // HEAD_DIM
    return (r == c).astype(BF16)


def _head_rms_scale(z, ones):
    hi, lo = _split2(z * z)
    ss = jnp.dot(hi, ones, preferred_element_type=F32) + jnp.dot(lo, ones, preferred_element_type=F32)
    return lax.rsqrt(ss * (1.0 / HEAD_DIM) + NORM_EPS)


def _attn_kernel(sink_ref, q_ref, kv_ref, prev_ref, bias_ref, qg_ref, kg_ref, ya_ref, kn_ref,
                 *, rq, n_part, prev_is_raw):
    i = pl.program_id(1)
    kp = WINDOW + rq
    gw = A_GROUP * HEAD_DIM
    q = q_ref[0]
    kv = kv_ref[0]
    pv = prev_ref[0]
    ones_q = _block_ones(D_A)
    ones_k = _block_ones(D_KV)
    qn = (q * _head_rms_scale(q, ones_q) * qg_ref[...]).astype(BF16)
    k_cur = kv[:, 0:D_KV]
    k_cur = k_cur * _head_rms_scale(k_cur, ones_k) * kg_ref[...]
    kn_ref[0] = k_cur
    k_prev = pv[:, 0:D_KV]
    if prev_is_raw:
        k_prev = k_prev * _head_rms_scale(k_prev, ones_k) * kg_ref[...]
    k_all = jnp.concatenate([k_prev, k_cur], axis=0).astype(BF16)
    v_all = jnp.concatenate([pv[:, D_KV:], kv[:, D_KV:]], axis=0).astype(BF16)
    nq = A_GROUP * rq
    lane_in = lax.broadcasted_iota(jnp.int32, (D_KV, gw), 0)
    lane_out = lax.broadcasted_iota(jnp.int32, (D_KV, gw), 1)
    lane_out_t = lax.broadcasted_iota(jnp.int32, (HEAD_DIM, D_KV), 0)
    lane_in_t = lax.broadcasted_iota(jnp.int32, (HEAD_DIM, D_KV), 1)
    slot_lane = lax.broadcasted_iota(jnp.int32, (1, gw), 1) // HEAD_DIM
    slot_mask_bf = [(slot_lane == hh).astype(BF16) for hh in range(A_GROUP)]
    key_row = lax.broadcasted_iota(jnp.int32, (kp, nq), 0)
    col_head = lax.broadcasted_iota(jnp.int32, (1, nq), 1) // rq
    for g in range(A_KV_HEADS):
        select = (lane_in // HEAD_DIM == g) & (lane_in % HEAD_DIM == lane_out % HEAD_DIM)
        k_wide = jnp.dot(k_all, select.astype(BF16), preferred_element_type=F32).astype(BF16)
        v_t = lax.dot_general((lane_in_t == g * HEAD_DIM + lane_out_t).astype(BF16), v_all,
                              (((1,), (1,)), ((), ())), preferred_element_type=F32).astype(BF16)
        q_g = qn[:, g * gw:(g + 1) * gw]
        sink_row = jnp.zeros((1, nq), F32)
        for hh in range(A_GROUP):
            sink_row = jnp.where(col_head == hh, sink_ref[g * A_GROUP + hh], sink_row)
        bias_t = bias_ref[g]
        for p in range(n_part):
            rows = slice(p * rq, (p + 1) * rq)
            qs = jnp.concatenate([q_g[rows] * m for m in slot_mask_bf], axis=0)
            kb = k_wide[p * rq:p * rq + kp]
            s = lax.dot_general(kb, qs, (((1,), (1,)), ((), ())), preferred_element_type=F32) + bias_t
            if prev_is_raw and p * rq < WINDOW:
                n_pad = WINDOW - (i * n_part + p) * rq
                s = s + jnp.where(key_row < n_pad, NEG_INF, 0.0)
            m = jnp.maximum(jnp.max(s, axis=0, keepdims=True), sink_row)
            e = jnp.exp2(s - m)
            den = jnp.sum(e, axis=0, keepdims=True) + jnp.exp2(sink_row - m)
            prob = (e * (1.0 / den)).astype(BF16)
            vb_t = v_t[:, p * rq:p * rq + kp]
            out_t = jnp.concatenate(
                [jnp.dot(vb_t, prob[:, hh * rq:(hh + 1) * rq], preferred_element_type=F32)
                 for hh in range(A_GROUP)], axis=0)
            ya_ref[0, rows, g * gw:(g + 1) * gw] = out_t.T.astype(BF16)


def _attn_bias(rq, cq):
    kp = WINDOW + rq
    i = jnp.arange(rq)[:, None]
    j = jnp.arange(kp)[None, :]
    jb = j - cq * (i // cq)
    valid = (jb >= 0) & (jb < WINDOW + cq)
    dist = jnp.abs(WINDOW + (i % cq) - jb).astype(F32)
    slopes = jnp.exp2(-8.0 * jnp.arange(1, A_HEADS + 1, dtype=F32) / A_HEADS)
    bias = jnp.where(valid[None], -slopes[:, None, None] * dist[None], NEG_INF)
    return jnp.swapaxes(bias.reshape(A_KV_HEADS, A_GROUP * rq, kp), 1, 2) * LOG2_E


def _attention(proj, cache_kv, q_gain, k_gain, sink, cq, rq, n_part):
    b, t, _ = proj.shape
    tq = rq * n_part
    prompt = cache_kv is None
    kv_col = D_A // (2 * D_KV)
    if prompt:
        prev_arr = proj
        prev_spec = pl.BlockSpec((1, WINDOW, 2 * D_KV),
                                 lambda bi, i: (bi, jnp.maximum(i * (tq // WINDOW) - 1, 0), kv_col))
    else:
        prev_arr = cache_kv
        prev_spec = pl.BlockSpec((1, WINDOW, 2 * D_KV), lambda bi, i: (bi, 0, 0))
    kp = WINDOW + rq
    q_gain_t = jnp.tile(q_gain * (HEAD_DIM ** -0.5 * LOG2_E), (1, A_HEADS))
    k_gain_t = jnp.tile(k_gain, (1, A_KV_HEADS))
    kern = functools.partial(_attn_kernel, rq=rq, n_part=n_part, prev_is_raw=prompt)
    return pl.pallas_call(
        kern,
        grid=(b, t // tq),
        in_specs=[pl.BlockSpec(memory_space=pltpu.SMEM),
                  pl.BlockSpec((1, tq, D_A), lambda bi, i: (bi, i, 0)),
                  pl.BlockSpec((1, tq, 2 * D_KV), lambda bi, i: (bi, i, kv_col)),
                  prev_spec,
                  pl.BlockSpec((A_KV_HEADS, kp, A_GROUP * rq), lambda bi, i: (0, 0, 0)),
                  pl.BlockSpec((1, D_A), lambda bi, i: (0, 0)),
                  pl.BlockSpec((1, D_KV), lambda bi, i: (0, 0))],
        out_specs=[pl.BlockSpec((1, tq, D_A), lambda bi, i: (bi, i, 0)),
                   pl.BlockSpec((1, tq, D_KV), lambda bi, i: (bi, i, 0))],
        out_shape=[jax.ShapeDtypeStruct((b, t, D_A), BF16),
                   jax.ShapeDtypeStruct((b, t, D_KV), F32)],
        compiler_params=_cparams("parallel", "arbitrary"),
        name="swa_attention",
    )(sink * LOG2_E, proj, proj, prev_arr, _attn_bias(rq, cq), q_gain_t, k_gain_t)


def _pool_kernel(u_ref, halo_ref, prefix_ref, w_ref, scale_ref, yb_ref, *, n_prefix):
    i = pl.program_id(1)
    u = u_ref[0]
    tb = u.shape[0]
    halo = jnp.where(i == 0, prefix_ref[0], halo_ref[0])
    ext = jnp.concatenate([halo, u], axis=0)
    col = lax.broadcasted_iota(jnp.int32, (1, D_POOL), 1)
    pos = i * tb + lax.broadcasted_iota(jnp.int32, (tb, 1), 0)
    total = None
    count = None
    acc = ext
    span = 1
    for gi, w in enumerate(POOL_WINDOWS):
        while span < w:
            acc = acc + pltpu.roll(acc, span, axis=0)
            span *= 2
        in_group = (col >= gi * POOL_GW) & (col < (gi + 1) * POOL_GW)
        tail = acc[POOL_HALO:]
        total = jnp.where(in_group, tail, 0.0 if total is None else total)
        cnt = jnp.minimum(pos + (1 + n_prefix), w).astype(F32)
        count = jnp.where(in_group, cnt, 1.0 if count is None else count)
    d = total / count - u
    yb_ref[0] = (_mm(d, w_ref[...]) * scale_ref[...]).astype(BF16)


def _pool_mix(proj, prefix, w_blockdiag, scale, n_prefix, tb):
    b, t, _ = proj.shape
    col = (D_A + 2 * D_KV) // D_POOL
    kern = functools.partial(_pool_kernel, n_prefix=n_prefix)
    return pl.pallas_call(
        kern,
        grid=(b, t // tb),
        in_specs=[pl.BlockSpec((1, tb, D_POOL), lambda bi, i: (bi, i, col)),
                  pl.BlockSpec((1, POOL_HALO, D_POOL),
                               lambda bi, i: (bi, jnp.maximum(i * (tb // POOL_HALO) - 1, 0), col)),
                  pl.BlockSpec((1, POOL_HALO, D_POOL), lambda bi, i: (bi, 0, 0)),
                  pl.BlockSpec((D_POOL, D_POOL), lambda bi, i: (0, 0)),
                  pl.BlockSpec((1, D_POOL), lambda bi, i: (0, 0))],
        out_specs=pl.BlockSpec((1, tb, D_POOL), lambda bi, i: (bi, i, 0)),
        out_shape=jax.ShapeDtypeStruct((b, t, D_POOL), BF16),
        compiler_params=_cparams("parallel", "arbitrary"),
        name="pool_mix",
    )(proj, proj, prefix, w_blockdiag, scale)


def _rwkv_kernel(p_ref, prev_ref, st0_ref, mu_ref, w0_ref, a0_ref, w2_ref, a2_ref, g2_ref,
                 kk_ref, ka_ref, rk_ref, lng_ref, lnb_ref, yc_ref, st_ref, carry_ref, y_ref,
                 *, chunk):
    j = pl.program_id(1)

    @pl.when(j == 0)
    def _():
        carry_ref[...] = prev_ref[0]
        st_ref[0] = st0_ref[0]

    p = p_ref[0]
    tb = p.shape[0]
    n_chunk = tb // chunk
    hl = R_HEADS * chunk

    row = lax.broadcasted_iota(jnp.int32, (tb, 1), 0)
    p_prev = jnp.where(row == 0, carry_ref[...], pltpu.roll(p, 1, axis=0))
    carry_ref[...] = p[tb - 1:tb]
    xs = p + mu_ref[...] * (p_prev - p)
    r = xs[:, 0:D_R]
    k = xs[:, D_R:2 * D_R]
    v = xs[:, 2 * D_R:3 * D_R]
    wa = xs[:, 3 * D_R:3 * D_R + R_WA]
    gd = xs[:, 3 * D_R + R_WA:]

    z = -(w0_ref[...] + _mm(jnp.tanh(wa), w2_ref[...]))
    softplus = jnp.maximum(z, 0.0) + jnp.log(1.0 + jnp.exp(-jnp.abs(z)))
    lw = -jnp.exp(-softplus - 0.5)
    a = _sigmoid(a0_ref[...] + _mm(wa, a2_ref[...]))
    g = _mm(_sigmoid(gd), g2_ref[...])

    lane_r = lax.broadcasted_iota(jnp.int32, (D_R, D_R), 0) // HEAD_DIM
    lane_c = lax.broadcasted_iota(jnp.int32, (D_R, D_R), 1) // HEAD_DIM
    head_ones = (lane_r == lane_c).astype(F32)
    seg_sum = lambda t: _mm_split_rhs(t, head_ones)

    kk = k * kk_ref[...]
    kk = kk / jnp.maximum(jnp.sqrt(seg_sum(kk * kk)), 1e-12)
    k2 = k * (1.0 + (a - 1.0) * ka_ref[...])
    bb = kk * a

    ti = lax.broadcasted_iota(jnp.int32, (tb, tb), 0)
    tj = lax.broadcasted_iota(jnp.int32, (tb, tb), 1)
    cum = _mm_split_lhs(((ti // chunk == tj // chunk) & (tj <= ti)).astype(F32), lw)

    wi = lax.broadcasted_iota(jnp.int32, (chunk, hl), 0)
    wj = lax.broadcasted_iota(jnp.int32, (chunk, hl), 1) % chunk
    strict = wj < wi
    incl = wj <= wi
    eye_w = (wj == wi).astype(F32)
    diag_k = (lax.broadcasted_iota(jnp.int32, (HEAD_DIM, D_R), 0)
              == lax.broadcasted_iota(jnp.int32, (HEAD_DIM, D_R), 1) % HEAD_DIM)
    lane_k = lax.broadcasted_iota(jnp.int32, (1, D_R), 1) // HEAD_DIM
    lane_t = lax.broadcasted_iota(jnp.int32, (1, hl), 1) // chunk
    mask_k = [(lane_k == h).astype(F32) for h in range(R_HEADS)]
    mask_k_bf = [m.astype(BF16) for m in mask_k]
    mask_t_bf = [(lane_t == h).astype(BF16) for h in range(R_HEADS)]

    def blockdiag(t, masks):
        t16 = t.astype(BF16)
        return jnp.concatenate([t16 * m for m in masks], axis=0)

    def wide_transpose(t):
        tt = jnp.concatenate([t * m for m in mask_k], axis=0).T
        out = tt[0:HEAD_DIM]
        for h in range(1, R_HEADS):
            out = out + tt[h * HEAD_DIM:(h + 1) * HEAD_DIM]
        return out.astype(BF16)

    chunks = []
    for c in range(n_chunk):
        sl = slice(c * chunk, (c + 1) * chunk)
        cum_c = cum[sl]
        cum_last = cum_c[chunk - 1:chunk]
        g_in = jnp.exp(cum_c)
        g_prev = jnp.exp(cum_c - lw[sl])
        g_inv = jnp.exp(-cum_c)
        g_out = jnp.exp(cum_last - cum_c)
        a_n = (kk[sl] * g_prev).astype(BF16)
        r_n = r[sl] * g_in
        ch = dict(sl=sl, r_n=r_n, g_last=jnp.exp(cum_last),
                  a_s=blockdiag(a_n, mask_k_bf),
                  v_s=blockdiag(v[sl], mask_k_bf),
                  bo_w=wide_transpose(bb[sl] * g_out),
                  ko_w=wide_transpose(k2[sl] * g_out))
        ar = jnp.concatenate([a_n, r_n.astype(BF16)], axis=0)
        m_b = _mm_nt(ar, blockdiag(bb[sl] * g_inv, mask_k_bf))
        m_k = _mm_nt(ar, blockdiag(k2[sl] * g_inv, mask_k_bf))
        m_ab = jnp.where(strict, m_b[:chunk], 0.0)
        ch.update(m_rb=jnp.where(incl, m_b[chunk:], 0.0).astype(BF16),
                  m_ak=jnp.where(strict, m_k[:chunk], 0.0).astype(BF16),
                  m_rk=jnp.where(incl, m_k[chunk:], 0.0).astype(BF16),
                  t_inv=eye_w - m_ab, pw=m_ab.astype(BF16))
        chunks.append(ch)

    for ch in chunks:
        ch["pw"] = _mm(ch["pw"], blockdiag(ch["pw"], mask_t_bf)).astype(BF16)
    n = 2
    while n < chunk:
        for ch in chunks:
            rhs = blockdiag(ch["pw"], mask_t_bf)
            if 2 * n < chunk:
                both = _mm(jnp.concatenate([ch["pw"], ch["t_inv"].astype(BF16)], axis=0), rhs)
                ch["pw"] = both[:chunk].astype(BF16)
                ch["t_inv"] = ch["t_inv"] + both[chunk:]
            else:
                ch["t_inv"] = ch["t_inv"] + _mm(ch["t_inv"], rhs)
        n *= 2

    for ch in chunks:
        t_inv = ch["t_inv"].astype(BF16)
        ch["a_bar"] = blockdiag(_mm(t_inv, ch["a_s"]), mask_k_bf)
        on_v = _mm(jnp.concatenate([ch["m_ak"], ch["m_rk"], ch["ko_w"]], axis=0), ch["v_s"])
        ch["mrk_v"], ch["ko_v"] = on_v[chunk:2 * chunk], on_v[2 * chunk:]
        ch["u0"] = blockdiag(-_mm(t_inv, blockdiag(on_v[:chunk], mask_k_bf)), mask_k_bf)
    for ch in chunks:
        lhs = jnp.concatenate([ch["m_rb"], ch["bo_w"]], axis=0)
        on_a = _mm(lhs, ch["a_bar"])
        on_u = _mm(lhs, ch["u0"])
        ch["r_bar"] = (ch["r_n"] - on_a[:chunk]).astype(BF16)
        ch["y0"] = ch["mrk_v"] + on_u[:chunk]
        ch["g_w"] = (jnp.where(diag_k, ch["g_last"], 0.0) - on_a[chunk:]).astype(BF16)
        ch["h_w"] = on_u[chunk:] + ch["ko_v"]

    st_w = st_ref[0]
    for ch in chunks:
        on_st = _mm(jnp.concatenate([ch["r_bar"], ch["g_w"]], axis=0), blockdiag(st_w, mask_k_bf))
        y_ref[ch["sl"], :] = ch["y0"] + on_st[:chunk]
        st_w = on_st[chunk:] + ch["h_w"]
    st_ref[0] = st_w

    y = y_ref[...]
    mean = seg_sum(y) * (1.0 / HEAD_DIM)
    d = y - mean
    var = seg_sum(d * d) * (1.0 / HEAD_DIM)
    yn = d * lax.rsqrt(var + GN_EPS) * lng_ref[...] + lnb_ref[...]
    bonus = seg_sum(r * k2 * rk_ref[...]) * v
    yc_ref[0] = ((yn + bonus) * g).astype(BF16)


def _rwkv_mix(proj, prev, st0, lp, chunk, tb):
    b, t, _ = proj.shape
    col = (D_A + 2 * D_KV + D_POOL) // D_R_IN
    row = lambda n: pl.BlockSpec((1, n), lambda bi, i: (0, 0))
    full = lambda s: pl.BlockSpec(s, lambda bi, i: (0,) * len(s))
    kern = functools.partial(_rwkv_kernel, chunk=chunk)
    return pl.pallas_call(
        kern,
        grid=(b, t // tb),
        in_specs=[pl.BlockSpec((1, tb, D_R_IN), lambda bi, i: (bi, i, col)),
                  pl.BlockSpec((1, 1, D_R_IN), lambda bi, i: (bi, 0, 0)),
                  pl.BlockSpec((1, HEAD_DIM, D_R), lambda bi, i: (bi, 0, 0)),
                  row(D_R_IN), row(D_R), row(D_R),
                  full((R_WA, D_R)), full((R_WA, D_R)), full((R_G, D_R)),
                  row(D_R), row(D_R), row(D_R), row(D_R), row(D_R)],
        out_specs=[pl.BlockSpec((1, tb, D_R), lambda bi, i: (bi, i, 0)),
                   pl.BlockSpec((1, HEAD_DIM, D_R), lambda bi, i: (bi, 0, 0))],
        out_shape=[jax.ShapeDtypeStruct((b, t, D_R), BF16),
                   jax.ShapeDtypeStruct((b, HEAD_DIM, D_R), F32)],
        scratch_shapes=[pltpu.VMEM((1, D_R_IN), F32), pltpu.VMEM((tb, D_R), F32)],
        compiler_params=_cparams("parallel", "arbitrary"),
        name="rwkv7_mix",
    )(proj, prev, st0, lp["mu"], lp["w0"], lp["a0"], lp["w2"], lp["a2"], lp["g2"],
      lp["k_k"], lp["k_a"], lp["r_k"], lp["lnx_g"], lp["lnx_b"])


def _state_to_wide(s):
    b = s.shape[0]
    return jnp.transpose(s, (0, 3, 1, 2)).reshape(b, HEAD_DIM, D_R)


def _wide_to_state(st):
    b = st.shape[0]
    return jnp.transpose(st.reshape(b, HEAD_DIM, R_HEADS, HEAD_DIM), (0, 2, 3, 1))


def _mix_out_kernel(x_ref, ya_ref, yb_ref, yc_ref, w_ref, g_ref, *rest, route):
    if route:
        rw_ref, xm_ref, h_ref, gate_ref = rest
    else:
        xm_ref, h_ref = rest
    dot = lambda u, lo, hi: jnp.dot(u[...], w_ref[lo:hi, :], preferred_element_type=F32)
    xm = (x_ref[...] + dot(ya_ref, 0, D_A) + dot(yb_ref, D_A, D_A + D_POOL)
          + dot(yc_ref, D_A + D_POOL, D_A + D_POOL + D_R))
    xm_ref[...] = xm
    h = xm * lax.rsqrt(jnp.mean(xm * xm, axis=-1, keepdims=True) + NORM_EPS) * g_ref[...]
    h_hi = h.astype(BF16)
    h_ref[...] = h_hi
    if route:
        rw_t = rw_ref[...]
        n_e = rw_t.shape[0]
        rw_hi = rw_t.astype(BF16).astype(F32)
        rw2 = jnp.concatenate([rw_hi, rw_t - rw_hi], axis=0).astype(BF16)
        h_lo = (h - h_hi.astype(F32)).astype(BF16)
        nt = lambda a, b: lax.dot_general(a, b, (((1,), (1,)), ((), ())), preferred_element_type=F32)
        on_hi = nt(rw2, h_hi)
        logits = on_hi[:n_e] + (on_hi[n_e:] + nt(rw2, h_lo)[:n_e])
        row = lax.broadcasted_iota(jnp.int32, logits.shape, 0).astype(F32)
        m1 = jnp.max(logits, axis=0, keepdims=True)
        i1 = jnp.min(jnp.where(logits == m1, row, float(n_e)), axis=0, keepdims=True)
        rest_l = jnp.where(row == i1, -jnp.inf, logits)
        m2 = jnp.max(rest_l, axis=0, keepdims=True)
        i2 = jnp.min(jnp.where(rest_l == m2, row, float(n_e)), axis=0, keepdims=True)
        e2 = jnp.exp(m2 - m1)
        gates_t = (jnp.where(row == i1, 1.0 / (1.0 + e2), 0.0)
                   + jnp.where(row == i2, e2 / (1.0 + e2), 0.0))
        gate_ref[...] = gates_t.T


def _mix_out(x, ya, yb, yc, w_out, g, router_w, tm):
    n, d = x.shape
    route = router_w is not None
    tile = lambda w: pl.BlockSpec((tm, w), lambda i: (i, 0))
    in_specs = [tile(d), tile(D_A), tile(D_POOL), tile(D_R),
                pl.BlockSpec(w_out.shape, lambda i: (0, 0)),
                pl.BlockSpec((1, d), lambda i: (0, 0))]
    out_specs = [tile(d), tile(d)]
    out_shape = [jax.ShapeDtypeStruct((n, d), F32), jax.ShapeDtypeStruct((n, d), BF16)]
    args = [x, ya, yb, yc, w_out, g]
    if route:
        n_e = router_w.shape[1]
        in_specs.append(pl.BlockSpec((n_e, d), lambda i: (0, 0)))
        out_specs.append(tile(n_e))
        out_shape.append(jax.ShapeDtypeStruct((n, n_e), F32))
        args.append(router_w.T)
    return pl.pallas_call(
        functools.partial(_mix_out_kernel, route=route),
        grid=(n // tm,),
        in_specs=in_specs, out_specs=out_specs, out_shape=out_shape,
        compiler_params=_cparams("parallel"),
        name="mix_out_proj",
    )(*args)


def _swiglu_act(h, wg, wu):
    return (_silu(jnp.dot(h, wg, preferred_element_type=F32))
            * jnp.dot(h, wu, preferred_element_type=F32)).astype(BF16)


def _down_proj(act_ref, wd, tf):
    out = None
    for jj in range(act_ref.shape[0]):
        part = jnp.dot(act_ref[jj], wd[jj * tf:(jj + 1) * tf, :], preferred_element_type=F32)
        out = part if out is None else out + part
    return out


def _ffn_kernel(h_ref, xm_ref, wg_ref, wu_ref, wd_ref, o_ref, act_ref):
    j = pl.program_id(1)
    tf = wg_ref.shape[1]
    act_ref[j] = _swiglu_act(h_ref[...], wg_ref[...], wu_ref[...])

    @pl.when(j == pl.num_programs(1) - 1)
    def _():
        o_ref[...] = xm_ref[...] + _down_proj(act_ref, wd_ref, tf)


def _ffn(h, xm, wg, wu, wd, tm, tf):
    n, d = xm.shape
    f = wg.shape[1]
    return pl.pallas_call(
        _ffn_kernel,
        grid=(n // tm, f // tf),
        in_specs=[pl.BlockSpec((tm, d), lambda i, j: (i, 0)),
                  pl.BlockSpec((tm, d), lambda i, j: (i, 0)),
                  pl.BlockSpec((d, tf), lambda i, j: (0, j)),
                  pl.BlockSpec((d, tf), lambda i, j: (0, j)),
                  pl.BlockSpec((f, d), lambda i, j: (0, 0))],
        out_specs=pl.BlockSpec((tm, d), lambda i, j: (i, 0)),
        out_shape=jax.ShapeDtypeStruct((n, d), F32),
        scratch_shapes=[pltpu.VMEM((f // tf, tm, tf), BF16)],
        compiler_params=_cparams("parallel", "arbitrary"),
        name="swiglu_ffn",
    )(h, xm, wg, wu, wd)


def _moe_kernel(h_ref, xm_ref, gate_ref, wg_ref, wu_ref, wd_ref, o_ref, acc_ref):
    e = pl.program_id(1)
    j = pl.program_id(2)

    @pl.when((e == 0) & (j == 0))
    def _():
        acc_ref[...] = xm_ref[...]

    gates = gate_ref[...]
    lane = lax.broadcasted_iota(jnp.int32, gates.shape, 1)
    gate = jnp.sum(jnp.where(lane == e, gates, 0.0), axis=-1, keepdims=True)
    h = h_ref[...]
    act = (_silu(jnp.dot(h, wg_ref[0], preferred_element_type=F32))
           * jnp.dot(h, wu_ref[0], preferred_element_type=F32))
    acc_ref[...] += gate * jnp.dot(act.astype(BF16), wd_ref[0], preferred_element_type=F32)

    @pl.when((e == pl.num_programs(1) - 1) & (j == pl.num_programs(2) - 1))
    def _():
        o_ref[...] = acc_ref[...]


def _moe(h, xm, gates, wg, wu, wd, tm, tf):
    n, d = xm.shape
    n_e, _, f = wg.shape
    return pl.pallas_call(
        _moe_kernel,
        grid=(n // tm, n_e, f // tf),
        in_specs=[pl.BlockSpec((tm, d), lambda i, e, j: (i, 0)),
                  pl.BlockSpec((tm, d), lambda i, e, j: (i, 0)),
                  pl.BlockSpec((tm, n_e), lambda i, e, j: (i, 0)),
                  pl.BlockSpec((1, d, tf), lambda i, e, j: (e, 0, j)),
                  pl.BlockSpec((1, d, tf), lambda i, e, j: (e, 0, j)),
                  pl.BlockSpec((1, tf, d), lambda i, e, j: (e, j, 0))],
        out_specs=pl.BlockSpec((tm, d), lambda i, e, j: (i, 0)),
        out_shape=jax.ShapeDtypeStruct((n, d), F32),
        scratch_shapes=[pltpu.VMEM((tm, d), F32)],
        compiler_params=_cparams("parallel", "arbitrary", "arbitrary"),
        name="moe_ffn",
    )(h, xm, gates, wg, wu, wd)


MOE_ROW_ALIGN = 16
MOE_GATE_LANES = 128
MOE_TILE = 512
MOE_GROUP_ROWS = 1024
_UNSELECTED = 1e9


def _tile_routing(gates_sel, rank):
    n_e = rank.shape[0]
    cnt = jnp.max(rank, axis=1, keepdims=True)
    padded = jnp.floor((cnt + (MOE_ROW_ALIGN - 1)) * (1.0 / MOE_ROW_ALIGN)) * MOE_ROW_ALIGN
    offs, run = [], jnp.zeros((1, 1), F32)
    for e in range(n_e):
        offs.append(run)
        run = run + padded[e:e + 1]
    off = jnp.concatenate(offs, axis=0)
    dest = jnp.where(gates_sel, off + rank - 1.0, -1.0)
    d_hi = jnp.max(dest, axis=0, keepdims=True)
    d_lo = jnp.min(jnp.where(gates_sel, dest, _UNSELECTED), axis=0, keepdims=True)
    d_lo = jnp.where(d_lo == d_hi, -2.0, d_lo)
    return cnt, dest, d_hi, d_lo


def _moe_dispatch_kernel(h_ref, gate_ref, hs_ref, cnt_ref, dest_ref, *, chunk_rows):
    h = h_ref[...]
    gates = gate_ref[...]
    tm, n_e = gates.shape
    d = h.shape[1]
    mt = hs_ref.shape[0]
    eye = (lax.broadcasted_iota(jnp.int32, (n_e, n_e), 0)
           == lax.broadcasted_iota(jnp.int32, (n_e, n_e), 1)).astype(BF16)
    to_rows = lambda u: lax.dot_general(eye, u, (((1,), (1,)), ((), ())), preferred_element_type=F32)
    g_hi3, g_mid3, g_lo3 = _split3(gates)
    gate_row = to_rows(g_hi3) + to_rows(g_mid3) + to_rows(g_lo3)
    sel_row = gate_row > 0.0
    ti = lax.broadcasted_iota(jnp.int32, (tm, tm), 0)
    tj = lax.broadcasted_iota(jnp.int32, (tm, tm), 1)
    rank = jnp.dot(sel_row.astype(BF16), (ti <= tj).astype(BF16), preferred_element_type=F32)
    cnt, dest, d_hi, d_lo = _tile_routing(sel_row, rank)
    cnt_ref[0] = jnp.broadcast_to(cnt, cnt_ref.shape[1:])
    dest_ref[0] = jnp.concatenate([d_hi, d_lo, jnp.zeros((n_e - 2, tm), F32)], axis=0)
    g_hi = jnp.sum(jnp.where(dest == d_hi, gate_row, 0.0), axis=0, keepdims=True)
    g_lo = jnp.sum(jnp.where(dest == d_lo, gate_row, 0.0), axis=0, keepdims=True)
    lane = lax.broadcasted_iota(jnp.int32, (chunk_rows, MOE_GATE_LANES), 1)
    for c0 in range(0, mt, chunk_rows):
        rho = (c0 + lax.broadcasted_iota(jnp.int32, (chunk_rows, 1), 0)).astype(F32)
        is_hi = d_hi == rho
        is_lo = d_lo == rho
        onehot = jnp.where(is_hi, 1.0, jnp.where(is_lo, 1.0, 0.0)).astype(BF16)
        rows = jnp.dot(onehot, h, preferred_element_type=F32)
        g = jnp.sum(jnp.where(is_hi, g_hi, jnp.where(is_lo, g_lo, 0.0)), axis=1, keepdims=True)
        p0, p1, p2 = [t.astype(F32) for t in _split3(g)]
        gcols = jnp.where(lane == 0, p0, jnp.where(lane == 1, p1, jnp.where(lane == 2, p2, 0.0)))
        hs_ref[c0:c0 + chunk_rows, 0:d] = rows.astype(BF16)
        hs_ref[c0:c0 + chunk_rows, d:d + MOE_GATE_LANES] = gcols.astype(BF16)


def _moe_dispatch(h, gates, tm, mt):
    n, d = h.shape
    n_e = gates.shape[1]
    n_tiles = n // tm
    return pl.pallas_call(
        functools.partial(_moe_dispatch_kernel, chunk_rows=_pick(mt, (384, 256, 128, 64, 32, 16))),
        grid=(n_tiles,),
        in_specs=[pl.BlockSpec((tm, d), lambda i: (i, 0)),
                  pl.BlockSpec((tm, n_e), lambda i: (i, 0))],
        out_specs=[pl.BlockSpec((mt, d + MOE_GATE_LANES), lambda i: (i, 0)),
                   pl.BlockSpec((1, n_e, 128), lambda i: (i, 0, 0)),
                   pl.BlockSpec((1, n_e, tm), lambda i: (i, 0, 0))],
        out_shape=[jax.ShapeDtypeStruct((n_tiles * mt, d + MOE_GATE_LANES), BF16),
                   jax.ShapeDtypeStruct((n_tiles, n_e, 128), F32),
                   jax.ShapeDtypeStruct((n_tiles, n_e, tm), F32)],
        compiler_params=_cparams("parallel"),
        name="moe_dispatch",
    )(h, gates)


def _moe_tables(cnt, mt, group_rows):
    i32 = jnp.int32
    n_tiles, n_e = cnt.shape
    bpt = mt // MOE_ROW_ALIGN
    g = group_rows // MOE_ROW_ALIGN
    nblk = jnp.ceil(cnt / MOE_ROW_ALIGN).astype(i32)
    cum = jnp.cumsum(nblk, axis=1)
    off_blk = cum - nblk
    src_base = jnp.arange(n_tiles, dtype=i32)[:, None] * bpt + off_blk
    tot = jnp.sum(nblk, axis=0)
    totp = ((tot + g - 1) // g) * g
    es = jnp.cumsum(totp) - totp
    nblk_t = nblk.T
    seg_start = es[:, None] + jnp.cumsum(nblk_t, axis=1) - nblk_t
    n_src = n_tiles * bpt
    n_dst = -(-(n_src + n_e * (g - 1)) // g) * g
    dblk = jnp.arange(n_dst, dtype=i32)[:, None]
    s0, sl, ss = seg_start.reshape(1, -1), nblk_t.reshape(1, -1), src_base.T.reshape(1, -1)
    hit = (dblk >= s0) & (dblk < s0 + sl)
    src_of_dst = jnp.sum(jnp.where(hit, ss + dblk - s0, 0), axis=1)
    n_rb = n_dst // g
    eid = jnp.sum(jnp.arange(n_rb, dtype=i32)[:, None] * g >= es[None, :], axis=1).astype(i32) - 1
    n_valid = ((es[-1] + totp[-1]) // g).reshape(1)
    o_of = jnp.arange(bpt, dtype=i32)[None, :, None]
    off3, len3 = off_blk[:, None, :], nblk[:, None, :]
    hit3 = (o_of >= off3) & (o_of < off3 + len3)
    dst_of_src = jnp.sum(jnp.where(hit3, seg_start.T[:, None, :] + o_of - off3, 0), axis=2).reshape(-1)
    return src_of_dst.astype(i32), dst_of_src.astype(i32), eid, n_valid.astype(i32), n_dst


def _block_gather_kernel(idx_ref, *refs):
    del idx_ref
    out_ref = refs[-1]
    rows = refs[0].shape[0]
    for q, src_ref in enumerate(refs[:-1]):
        out_ref[q * rows:(q + 1) * rows, :] = src_ref[...]


def _block_gather(idx, src, n_blocks, per_step, name):
    rows = MOE_ROW_ALIGN
    width = src.shape[1]
    pick = lambda q: pl.BlockSpec((rows, width), lambda i, idx: (idx[i * per_step + q], 0))
    return pl.pallas_call(
        _block_gather_kernel,
        grid_spec=pltpu.PrefetchScalarGridSpec(
            num_scalar_prefetch=1, grid=(n_blocks // per_step,),
            in_specs=[pick(q) for q in range(per_step)],
            out_specs=pl.BlockSpec((per_step * rows, width), lambda i, idx: (i, 0))),
        out_shape=jax.ShapeDtypeStruct((n_blocks * rows, width), src.dtype),
        compiler_params=_cparams("arbitrary"),
        name=name,
    )(idx, *([src] * per_step))


def _moe_group_kernel(eid_ref, nv_ref, hs_ref, wg_ref, wu_ref, wd_ref, ys_ref, act_ref):
    del eid_ref
    i = pl.program_id(0)
    j = pl.program_id(1)
    d = ys_ref.shape[1]
    tf = wg_ref.shape[2]

    @pl.when(i < nv_ref[0])
    def _():
        act_ref[j] = _swiglu_act(hs_ref[:, 0:d], wg_ref[0], wu_ref[0])

        @pl.when(j == pl.num_programs(1) - 1)
        def _():
            gp = hs_ref[:, d:d + MOE_GATE_LANES].astype(F32)
            gate = gp[:, 0:1] + gp[:, 1:2] + gp[:, 2:3]
            ys_ref[...] = (_down_proj(act_ref, wd_ref[0], tf) * gate).astype(BF16)

    @pl.when((i >= nv_ref[0]) & (j == 0))
    def _():
        ys_ref[...] = jnp.zeros(ys_ref.shape, BF16)


def _moe_group_ffn(hs, eid, n_valid, wg, wu, wd, rb, tf):
    rows, dw = hs.shape
    d = dw - MOE_GATE_LANES
    f = wg.shape[2]
    nj = f // tf
    live_j = lambda i, j, nv: jnp.where(i < nv[0], j, nj - 1)
    return pl.pallas_call(
        _moe_group_kernel,
        grid_spec=pltpu.PrefetchScalarGridSpec(
            num_scalar_prefetch=2, grid=(rows // rb, nj),
            in_specs=[pl.BlockSpec((rb, dw), lambda i, j, eid, nv: (jnp.minimum(i, nv[0] - 1), 0)),
                      pl.BlockSpec((1, d, tf), lambda i, j, eid, nv: (eid[i], 0, live_j(i, j, nv))),
                      pl.BlockSpec((1, d, tf), lambda i, j, eid, nv: (eid[i], 0, live_j(i, j, nv))),
                      pl.BlockSpec((1, f, d), lambda i, j, eid, nv: (eid[i], 0, 0))],
            out_specs=pl.BlockSpec((rb, d), lambda i, j, eid, nv: (i, 0)),
            scratch_shapes=[pltpu.VMEM((nj, rb, tf), BF16)]),
        out_shape=jax.ShapeDtypeStruct((rows, d), BF16),
        compiler_params=_cparams("arbitrary", "arbitrary"),
        name="moe_group_ffn",
    )(eid, n_valid, hs, wg, wu, wd)


def _moe_combine_kernel(idx_ref, xm_ref, dest_ref, *refs):
    del idx_ref
    o_ref, ys_ref = refs[-2:]
    rows = refs[0].shape[0]
    for q, src_ref in enumerate(refs[:-2]):
        ys_ref[q * rows:(q + 1) * rows, :] = src_ref[...]
    mt = ys_ref.shape[0]
    dest = dest_ref[0].T
    d_hi = dest[:, 0:1]
    d_lo = dest[:, 1:2]
    rho = lax.broadcasted_iota(jnp.int32, (1, mt), 1).astype(F32)
    onehot = jnp.where(d_hi == rho, 1.0, jnp.where(d_lo == rho, 1.0, 0.0)).astype(BF16)
    o_ref[...] = xm_ref[...] + jnp.dot(onehot, ys_ref[...], preferred_element_type=F32)


def _moe_combine(xm, dest, ys_exp, dst_of_src, tm, mt):
    n, d = xm.shape
    n_e = dest.shape[1]
    bpt = mt // MOE_ROW_ALIGN
    piece = lambda q: pl.BlockSpec((MOE_ROW_ALIGN, d), lambda i, idx: (idx[i * bpt + q], 0))
    return pl.pallas_call(
        _moe_combine_kernel,
        grid_spec=pltpu.PrefetchScalarGridSpec(
            num_scalar_prefetch=1, grid=(n // tm,),
            in_specs=[pl.BlockSpec((tm, d), lambda i, idx: (i, 0)),
                      pl.BlockSpec((1, n_e, tm), lambda i, idx: (i, 0, 0))]
            + [piece(q) for q in range(bpt)],
            out_specs=pl.BlockSpec((tm, d), lambda i, idx: (i, 0)),
            scratch_shapes=[pltpu.VMEM((mt, d), BF16)]),
        out_shape=jax.ShapeDtypeStruct((n, d), F32),
        compiler_params=_cparams("arbitrary"),
        name="moe_combine",
    )(dst_of_src, xm, dest, *([ys_exp] * bpt))


def _moe_sparse(h, xm, gates, wg, wu, wd, tf):
    n_e = gates.shape[1]
    tm = MOE_TILE
    mt = TOP_K * tm + n_e * MOE_ROW_ALIGN
    hs_tile, cnt, dest = _moe_dispatch(h, gates, tm, mt)
    src_of_dst, dst_of_src, eid, n_valid, n_dst = _moe_tables(cnt[:, :, 0], mt, MOE_GROUP_ROWS)
    per_step = MOE_GROUP_ROWS // MOE_ROW_ALIGN
    hs_exp = _block_gather(src_of_dst, hs_tile, n_dst, per_step, "moe_gather_blocks")
    ys_exp = _moe_group_ffn(hs_exp, eid, n_valid, wg, wu, wd, MOE_GROUP_ROWS, tf)
    return _moe_combine(xm, dest, ys_exp, dst_of_src, tm, mt)


def _pick(n, candidates):
    for c in candidates:
        if n % c == 0:
            return c
    raise ValueError(f"no tile for {n}")


def _trunk(x, layers, cache_k, cache_v, state_pool, state_shift, state_wkv):
    prompt = cache_k is None
    b, t, d = x.shape
    n = b * t
    tm = _pick(n, (1024, 512, 256, 128))
    tm_ffn = _pick(n, (1024, 512, 256, 128))
    tf = _pick(layers[0]["wg"].shape[-1], (512, 256, 128))
    cq = CHUNK if prompt else t
    rq = _pick(t, (2 * CHUNK,)) if prompt else t
    n_part = _pick(t // rq, (8, 4, 2, 1))
    pool_tb = _pick(t, (2048, 1024, 512, 256, 128, 64, 32))
    r_chunk = CHUNK if prompt else t
    r_tb = _pick(t, (512, 256, 128, 64, 32))
    x2 = x.reshape(n, d)
    nk, nv, npool, nshift, nwkv = [], [], [], [], []
    for l, lp in enumerate(layers):
        proj2 = _norm_matmul(x2, lp["norm1_g"], lp["w_in"], tm)
        proj = proj2.reshape(b, t, -1)
        c_pool = D_A + 2 * D_KV
        c_r = c_pool + D_POOL
        v_raw = proj[:, :, D_A + D_KV:c_pool]
        if prompt:
            cache_kv = None
            prefix = jnp.zeros((b, POOL_HALO, D_POOL), F32)
            prev = jnp.zeros((b, 1, D_R_IN), F32)
            st0 = jnp.zeros((b, HEAD_DIM, D_R), F32)
        else:
            cache_kv = jnp.concatenate([cache_k[l].reshape(b, WINDOW, D_KV),
                                        cache_v[l].reshape(b, WINDOW, D_KV)], axis=-1)
            prefix = jnp.pad(state_pool[l], ((0, 0), (POOL_HALO - POOL_CTX, 0), (0, 0)))
            prev = state_shift[l]
            st0 = _state_to_wide(state_wkv[l])
        ya, k_norm = _attention(proj, cache_kv, lp["q_gain"], lp["k_gain"], lp["sink"], cq, rq, n_part)
        yb = _pool_mix(proj, prefix, lp["pool_w"], lp["pool_scale"], 0 if prompt else POOL_CTX, pool_tb)
        yc, st_fin = _rwkv_mix(proj, prev, st0, lp, r_chunk, r_tb)
        keep = WINDOW if prompt else t
        nk.append(k_norm[:, t - keep:].reshape(b, keep, A_KV_HEADS, HEAD_DIM))
        nv.append(v_raw[:, t - keep:].reshape(b, keep, A_KV_HEADS, HEAD_DIM))
        npool.append(proj[:, t - POOL_CTX:, c_pool:c_r])
        nshift.append(proj[:, t - 1:, c_r:])
        nwkv.append(_wide_to_state(st_fin))
        outs = _mix_out(x2, ya.reshape(n, -1), yb.reshape(n, -1), yc.reshape(n, -1),
                        lp["w_out"], lp["norm2_g"], lp.get("router_w"), tm)
        if "router_w" in lp:
            xm, h2, gates = outs
            if n % MOE_TILE == 0 and n >= 2 * MOE_TILE:
                x2 = _moe_sparse(h2, xm, gates, lp["wg"], lp["wu"], lp["wd"], tf)
            else:
                tf_stream = _pick(lp["wg"].shape[-1], (1792, 512, 256, 128))
                x2 = _moe(h2, xm, gates, lp["wg"], lp["wu"], lp["wd"], tm_ffn, tf_stream)
        else:
            xm, h2 = outs
            x2 = _ffn(h2, xm, lp["wg"], lp["wu"], lp["wd"], tm_ffn, tf)
    return (x2.reshape(b, t, d), jnp.stack(nk), jnp.stack(nv), jnp.stack(npool),
            jnp.stack(nshift), jnp.stack(nwkv))


def kernel(x_prompt, x_sample, cache_k, cache_v, state_pool, state_shift, state_wkv, norm1_g, w_in, q_gain, k_gain, attn_sink, pool_w, pool_scale, shift_mu, decay_w0, decay_w2, iclr_a0, iclr_a2, gate_g2, k_k, k_a, r_k, lnx_g, lnx_b, w_out, norm2_g, ffn_wg, ffn_wu, ffn_wd, router_w, moe_wg, moe_wu, moe_wd):
    depth = w_in.shape[0]
    r_w = decay_w2.shape[1]
    layers = []
    for l in range(depth):
        lp = dict(
            norm1_g=norm1_g[l][None], w_in=w_in[l].astype(BF16),
            q_gain=q_gain[l][None], k_gain=k_gain[l][None], sink=attn_sink[l],
            pool_w=jax.scipy.linalg.block_diag(*[pool_w[l, gi] for gi in range(len(POOL_WINDOWS))]).astype(BF16),
            pool_scale=pool_scale[l][None],
            mu=shift_mu[l][None], w0=decay_w0[l][None], a0=iclr_a0[l][None],
            w2=jnp.pad(decay_w2[l], ((0, R_WA - r_w), (0, 0))).astype(BF16),
            a2=jnp.pad(iclr_a2[l], ((r_w, 0), (0, 0))).astype(BF16),
            g2=gate_g2[l].astype(BF16),
            k_k=k_k[l][None], k_a=k_a[l][None], r_k=r_k[l].reshape(1, D_R),
            lnx_g=lnx_g[l][None], lnx_b=lnx_b[l][None],
            w_out=w_out[l].astype(BF16), norm2_g=norm2_g[l][None])
        if l % 2 == 0:
            lp.update(wg=ffn_wg[l // 2].astype(BF16), wu=ffn_wu[l // 2].astype(BF16),
                      wd=ffn_wd[l // 2].astype(BF16))
        else:
            lp.update(router_w=router_w[l // 2], wg=moe_wg[l // 2].astype(BF16),
                      wu=moe_wu[l // 2].astype(BF16), wd=moe_wd[l // 2].astype(BF16))
        layers.append(lp)
    y_p, pk, pv, ppool, pshift, pwkv = _trunk(x_prompt, layers, None, None, None, None, None)
    y_s, sk, sv, spool, sshift, swkv = _trunk(x_sample, layers, cache_k, cache_v, state_pool,
                                              state_shift, state_wkv)
    return (y_p, y_s, pk, pv, ppool, pshift, pwkv, sk, sv, spool, sshift, swkv)
```
